```python
import math
import jax, jax.numpy as jnp
from jax import lax
import numpy as np


D_MODEL = 1024
BATCH = 16
SEQ = 256
DEPTH = 4
DEC_BATCH = 8
DEC_SEQ = 1024
PAST_LEN = 256

GRID_W = 64
N_MIXERS = 3
N_FNET_LAYERS = (DEPTH + 2) // 3
N_RWKV_LAYERS = (DEPTH + 1) // 3
N_NA_LAYERS = DEPTH // 3
N_MOD = 6
EPS = 1e-6

FNET_GROUPS = 4
FNET_GROUP_DIM = D_MODEL // FNET_GROUPS

RWKV_HEAD_DIM = 64
RWKV_HEADS = D_MODEL // RWKV_HEAD_DIM
RWKV_DECAY_LORA = 64
RWKV_AAA_LORA = 64
RWKV_GATE_LORA = 128
RWKV_LNX_EPS = 64e-5

NA_HEAD_DIM = 64
NA_HEADS = D_MODEL // NA_HEAD_DIM
NA_WIN_ROWS_MAX = 8
NA_WIN_COLS = 16
NA_RPB_ROWS = 2 * NA_WIN_ROWS_MAX - 1
NA_RPB_COLS = 2 * NA_WIN_COLS - 1
NA_SCALE = NA_HEAD_DIM ** -0.5
NEG_INF = -1e30

PEER_N_KEYS = 128
PEER_N_EXPERTS = PEER_N_KEYS * PEER_N_KEYS
PEER_HEADS = 8
PEER_QUERY_DIM = 256
PEER_HALF_DIM = PEER_QUERY_DIM // 2
PEER_TOPK = 16
PEER_CHUNK = 128

kernel_name = 'hybrid_fnet_rwkv7_natten_peer_diffusion_step'


def rmsnorm(x, g):
    xf = x.astype(jnp.float32)
    y = xf * lax.rsqrt(jnp.mean(xf * xf, axis=-1, keepdims=True) + EPS)
    return (y * g.astype(jnp.float32)).astype(x.dtype)


def ada_params(cond, w, b):
    m = jax.nn.silu(cond) @ w + b
    return m.reshape(cond.shape[0], N_MOD, D_MODEL)


def modulate(x, g, shift, scale):
    return rmsnorm(x, g) * (1 + scale[:, None, :]) + shift[:, None, :]


def fourier_mix(h, w_in, w_out):
    B, T, _ = h.shape
    u = (h @ w_in).reshape(B, T, FNET_GROUPS, FNET_GROUP_DIM).astype(jnp.float32)
    f = jnp.fft.fft2(u, axes=(1, 3), norm='ortho').real
    return f.reshape(B, T, D_MODEL).astype(h.dtype) @ w_out


def centred_shift(x):
    zero = jnp.zeros_like(x[:, :1])
    prev = jnp.concatenate([zero, x[:, :-1]], axis=1)
    nxt = jnp.concatenate([x[:, 1:], zero], axis=1)
    return 0.5 * (prev + nxt)


def wkv_scan(s0, r, w, k, v, a, b):
    def step(S, inp):
        r_t, w_t, k_t, v_t, a_t, b_t = inp
        sa = jnp.einsum('dbhvk,dbhk->dbhv', S, a_t)
        S = S * w_t[..., None, :] + sa[..., None] * b_t[..., None, :] + v_t[..., None] * k_t[..., None, :]
        y = jnp.einsum('dbhvk,dbhk->dbhv', S, r_t)
        return S, y
    xs = tuple(jnp.moveaxis(t, 2, 0) for t in (r, w, k, v, a, b))
    S, ys = lax.scan(step, s0, xs)
    return S, jnp.moveaxis(ys, 0, 2)


def rwkv_mix(h, s0, mu, w_r, w_k, w_v, w_o, w0, w1, w2, a0, a1, a2, g1, g2, k_k, k_a, r_k, lnx_g, lnx_b):
    B, T, _ = h.shape
    H, N = RWKV_HEADS, RWKV_HEAD_DIM
    f32 = jnp.float32
    xx = centred_shift(h) - h
    xr, xw, xk, xv, xa, xg = (h + xx * mu[i] for i in range(6))
    r = (xr @ w_r).astype(f32)
    k = (xk @ w_k).astype(f32)
    v = (xv @ w_v).astype(f32)
    g = jax.nn.sigmoid(xg @ g1) @ g2
    w_raw = w0[:, None, None, :] + jnp.einsum('jbtr,jrd->jbtd', jnp.tanh(jnp.einsum('btd,jdr->jbtr', xw, w1)), w2)
    decay = jnp.exp(-jnp.exp(-jax.nn.softplus(-w_raw.astype(f32)) - 0.5))
    a = jax.nn.sigmoid((a0[:, None, None, :] + jnp.einsum('jbtr,jrd->jbtd', jnp.einsum('btd,jdr->jbtr', xa, a1), a2)).astype(f32))
    kk = (k * k_k).reshape(B, T, H, N)
    kk = (kk * lax.rsqrt(jnp.sum(kk * kk, axis=-1, keepdims=True) + 1e-12)).reshape(B, T, D_MODEL)
    k_dir = k * (1 + (a - 1) * k_a)
    b_dir = kk * a

    def heads(t):
        return t.reshape(t.shape[:-1] + (H, N))

    def orient(t):
        return jnp.stack([t[0], jnp.flip(t[1], axis=1)])

    def both(t):
        return jnp.stack([t, t])

    S, y = wkv_scan(s0.astype(f32), orient(heads(both(r))), orient(heads(decay)), orient(heads(k_dir)),
                    orient(heads(both(v))), orient(heads(both(-kk))), orient(heads(b_dir)))
    o = y[0] + jnp.flip(y[1], axis=1)
    mean = jnp.mean(o, axis=-1, keepdims=True)
    var = jnp.mean(jnp.square(o - mean), axis=-1, keepdims=True)
    o = (o - mean) * lax.rsqrt(var + RWKV_LNX_EPS) * heads(lnx_g.astype(f32)) + heads(lnx_b.astype(f32))
    bonus = jnp.sum(heads(r)[None] * heads(k_dir) * r_k.astype(f32), axis=-1, keepdims=True) * heads(v)[None]
    o = o + jnp.sum(bonus, axis=0)
    out = (o.reshape(B, T, D_MODEL).astype(h.dtype) * g) @ w_o
    return out, S.astype(h.dtype)


def split_qkv(h, w_qkv):
    B, T, _ = h.shape
    qkv = (h @ w_qkv).reshape(B, T, 3, NA_HEADS, NA_HEAD_DIM)
    return qkv[:, :, 0], qkv[:, :, 1], qkv[:, :, 2]


def na_context(h, w_qkv, w_o):
    B, T, _ = h.shape
    q, k, v = split_qkv(h, w_qkv)
    s = jnp.einsum('bqhd,bkhd->bhqk', q, k).astype(jnp.float32) * NA_SCALE
    p = jax.nn.softmax(s, axis=-1).astype(v.dtype)
    o = jnp.einsum('bhqk,bkhd->bqhd', p, v).reshape(B, T, D_MODEL)
    return o @ w_o, k, v


def na_latent(h, ck, cv, w_qkv, w_o, rpb):
    B, T, _ = h.shape
    rows = T // GRID_W
    wr = min(NA_WIN_ROWS_MAX, rows)
    q, k, v = split_qkv(h, w_qkv)
    rs = np.clip(np.arange(rows) - wr // 2, 0, rows - wr)
    row_idx = rs[:, None] + np.arange(wr)[None, :]
    row_off = row_idx - np.arange(rows)[:, None] + (NA_WIN_ROWS_MAX - 1)
    qc = np.arange(GRID_W)
    cs = np.clip(qc - NA_WIN_COLS // 2, 0, GRID_W - NA_WIN_COLS)
    kc = np.arange(GRID_W)
    col_valid = (kc[None, :] >= cs[:, None]) & (kc[None, :] < cs[:, None] + NA_WIN_COLS)
    col_off = np.clip(kc[None, :] - qc[:, None], -(NA_WIN_COLS - 1), NA_WIN_COLS - 1) + NA_WIN_COLS - 1
    mask = np.broadcast_to(col_valid[:, None, :], (GRID_W, wr, GRID_W)).reshape(GRID_W, wr * GRID_W)
    bias = rpb.astype(jnp.float32)[:, row_off[:, None, :, None], col_off[None, :, None, :]]
    bias = bias.reshape(NA_HEADS, rows, GRID_W, wr * GRID_W)
    qg = q.reshape(B, rows, GRID_W, NA_HEADS, NA_HEAD_DIM)
    kg = k.reshape(B, rows, GRID_W, NA_HEADS, NA_HEAD_DIM)[:, row_idx].reshape(B, rows, wr * GRID_W, NA_HEADS, NA_HEAD_DIM)
    vg = v.reshape(B, rows, GRID_W, NA_HEADS, NA_HEAD_DIM)[:, row_idx].reshape(B, rows, wr * GRID_W, NA_HEADS, NA_HEAD_DIM)
    s_loc = jnp.einsum('brqhd,brkhd->bhrqk', qg, kg).astype(jnp.float32) * NA_SCALE + bias[None]
    s_loc = jnp.where(mask[None, None, None], s_loc, NEG_INF)
    s_ctx = jnp.einsum('brqhd,bshd->bhrqs', qg, ck).astype(jnp.float32) * NA_SCALE
    p = jax.nn.softmax(jnp.concatenate([s_loc, s_ctx], axis=-1), axis=-1).astype(v.dtype)
    n_loc = wr * GRID_W
    o = jnp.einsum('bhrqk,brkhd->brqhd', p[..., :n_loc], vg) + jnp.einsum('bhrqs,bshd->brqhd', p[..., n_loc:], cv)
    return o.reshape(B, T, D_MODEL) @ w_o


def peer_ffn(h, w_q, sub_keys, expert_u, expert_v):
    B, T, _ = h.shape
    n_tok = B * T
    x = h.reshape(n_tok, D_MODEL)
    q = (x @ w_q).reshape(n_tok, PEER_HEADS, 2, PEER_HALF_DIM)
    s = jnp.einsum('thcd,hcnd->thcn', q, sub_keys).astype(jnp.float32)
    sv, si = lax.top_k(s, PEER_TOPK)
    cand = (sv[:, :, 0, :, None] + sv[:, :, 1, None, :]).reshape(n_tok, PEER_HEADS, PEER_TOPK * PEER_TOPK)
    cand_idx = (si[:, :, 0, :, None] * PEER_N_KEYS + si[:, :, 1, None, :]).reshape(n_tok, PEER_HEADS, PEER_TOPK * PEER_TOPK)
    top_s, pos = lax.top_k(cand, PEER_TOPK)
    idx = jnp.take_along_axis(cand_idx, pos, axis=-1)
    gates = jax.nn.softmax(top_s, axis=-1)
    n_chunks = n_tok // PEER_CHUNK
    e_per_tok = PEER_HEADS * PEER_TOPK

    def chunk_fn(args):
        xc, ic, gc = args
        act = jax.nn.gelu(jnp.einsum('cd,ced->ce', xc, expert_u[ic]).astype(jnp.float32), approximate=False)
        wgt = (gc * act).astype(xc.dtype)
        return jnp.einsum('ce,ced->cd', wgt, expert_v[ic])

    y = lax.map(chunk_fn, (x.reshape(n_chunks, PEER_CHUNK, D_MODEL),
                           idx.reshape(n_chunks, PEER_CHUNK, e_per_tok),
                           gates.reshape(n_chunks, PEER_CHUNK, e_per_tok)))
    return y.reshape(B, T, D_MODEL)


def setup_inputs(seed: int = 0) -> dict:
    key = jax.random.key(seed)
    ks = iter(jax.random.split(key, 48))
    f32 = jnp.float32
    D = D_MODEL
    inv = D ** -0.5

    def nrm(shape, scale=1.0):
        return jax.random.normal(next(ks), shape, f32) * scale

    return {
        'x_prompt': nrm((BATCH, SEQ, D)),
        'x_sample': nrm((DEC_BATCH, DEC_SEQ, D)),
        'c': nrm((DEC_BATCH, D)),
        'state_wkv': nrm((DEC_BATCH, N_RWKV_LAYERS, 2, RWKV_HEADS, RWKV_HEAD_DIM, RWKV_HEAD_DIM), 0.5),
        'cache_k': nrm((DEC_BATCH, N_NA_LAYERS, PAST_LEN, NA_HEADS, NA_HEAD_DIM)),
        'cache_v': nrm((DEC_BATCH, N_NA_LAYERS, PAST_LEN, NA_HEADS, NA_HEAD_DIM)),
        'c_ctx': nrm((D,)),
        'ada_w': nrm((DEPTH, D, N_MOD * D), 0.5 * inv),
        'ada_b': nrm((DEPTH, N_MOD * D), 0.02),
        'norm_mix': 1.0 + nrm((DEPTH, D), 0.05),
        'norm_ffn': 1.0 + nrm((DEPTH, D), 0.05),
        'fnet_w_in': nrm((N_FNET_LAYERS, D, D), inv),
        'fnet_w_out': nrm((N_FNET_LAYERS, D, D), inv),
        'rwkv_mu': jax.random.uniform(next(ks), (N_RWKV_LAYERS, 6, D), f32),
        'rwkv_w_r': nrm((N_RWKV_LAYERS, D, D), inv),
        'rwkv_w_k': nrm((N_RWKV_LAYERS, D, D), inv),
        'rwkv_w_v': nrm((N_RWKV_LAYERS, D, D), inv),
        'rwkv_w_o': nrm((N_RWKV_LAYERS, D, D), inv),
        'rwkv_w0': nrm((N_RWKV_LAYERS, 2, D), 0.5),
        'rwkv_w1': nrm((N_RWKV_LAYERS, 2, D, RWKV_DECAY_LORA), inv),
        'rwkv_w2': nrm((N_RWKV_LAYERS, 2, RWKV_DECAY_LORA, D), 0.5 * RWKV_DECAY_LORA ** -0.5),
        'rwkv_a0': nrm((N_RWKV_LAYERS, 2, D), 0.5),
        'rwkv_a1': nrm((N_RWKV_LAYERS, 2, D, RWKV_AAA_LORA), inv),
        'rwkv_a2': nrm((N_RWKV_LAYERS, 2, RWKV_AAA_LORA, D), 0.5 * RWKV_AAA_LORA ** -0.5),
        'rwkv_g1': nrm((N_RWKV_LAYERS, D, RWKV_GATE_LORA), inv),
        'rwkv_g2': nrm((N_RWKV_LAYERS, RWKV_GATE_LORA, D), RWKV_GATE_LORA ** -0.5),
        'rwkv_k_k': 0.85 + nrm((N_RWKV_LAYERS, D), 0.05),
        'rwkv_k_a': 1.0 + nrm((N_RWKV_LAYERS, D), 0.05),
        'rwkv_r_k': nrm((N_RWKV_LAYERS, RWKV_HEADS, RWKV_HEAD_DIM), 0.1),
        'rwkv_lnx_g': 1.0 + nrm((N_RWKV_LAYERS, D), 0.05),
        'rwkv_lnx_b': nrm((N_RWKV_LAYERS, D), 0.02),
        'na_w_qkv': nrm((N_NA_LAYERS, D, 3 * D), inv),
        'na_w_o': nrm((N_NA_LAYERS, D, D), inv),
        'na_rpb': nrm((N_NA_LAYERS, NA_HEADS, NA_RPB_ROWS, NA_RPB_COLS), 0.1),
        'peer_w_q': nrm((DEPTH, D, PEER_HEADS * PEER_QUERY_DIM), inv),
        'peer_sub_keys': nrm((DEPTH, PEER_HEADS, 2, PEER_N_KEYS, PEER_HALF_DIM), PEER_HALF_DIM ** -0.5),
        'peer_u': nrm((DEPTH, PEER_N_EXPERTS, D), inv),
        'peer_v': nrm((DEPTH, PEER_N_EXPERTS, D), 0.5),
        'final_norm': 1.0 + nrm((D,), 0.05),
    }


def reference(x_prompt, x_sample, c, state_wkv, cache_k, cache_v, c_ctx, ada_w, ada_b, norm_mix, norm_ffn,
              fnet_w_in, fnet_w_out, rwkv_mu, rwkv_w_r, rwkv_w_k, rwkv_w_v, rwkv_w_o, rwkv_w0, rwkv_w1, rwkv_w2,
              rwkv_a0, rwkv_a1, rwkv_a2, rwkv_g1, rwkv_g2, rwkv_k_k, rwkv_k_a, rwkv_r_k, rwkv_lnx_g, rwkv_lnx_b,
              na_w_qkv, na_w_o, na_rpb, peer_w_q, peer_sub_keys, peer_u, peer_v, final_norm):
    xc = x_prompt
    xs = x_sample
    new_wkv, new_k, new_v = [], [], []
    for l in range(DEPTH):
        kind, j = l % N_MIXERS, l // N_MIXERS
        m_ctx = ada_params(c_ctx[None, :], ada_w[l], ada_b[l])
        m_lat = ada_params(c, ada_w[l], ada_b[l])
        hc = modulate(xc, norm_mix[l], m_ctx[:, 0], m_ctx[:, 1])
        hs = modulate(xs, norm_mix[l], m_lat[:, 0], m_lat[:, 1])
        if kind == 0:
            yc = fourier_mix(hc, fnet_w_in[j], fnet_w_out[j])
            ys = fourier_mix(hs, fnet_w_in[j], fnet_w_out[j])
        elif kind == 1:
            p = (rwkv_mu[j], rwkv_w_r[j], rwkv_w_k[j], rwkv_w_v[j], rwkv_w_o[j], rwkv_w0[j], rwkv_w1[j], rwkv_w2[j],
                 rwkv_a0[j], rwkv_a1[j], rwkv_a2[j], rwkv_g1[j], rwkv_g2[j], rwkv_k_k[j], rwkv_k_a[j], rwkv_r_k[j],
                 rwkv_lnx_g[j], rwkv_lnx_b[j])
            s_zero = jnp.zeros((2, xc.shape[0], RWKV_HEADS, RWKV_HEAD_DIM, RWKV_HEAD_DIM), jnp.float32)
            yc, sc = rwkv_mix(hc, s_zero, *p)
            ys, _ = rwkv_mix(hs, jnp.moveaxis(state_wkv[:, j], 1, 0), *p)
            new_wkv.append(jnp.moveaxis(sc, 0, 1))
        else:
            yc, kc, vc = na_context(hc, na_w_qkv[j], na_w_o[j])
            ys = na_latent(hs, cache_k[:, j], cache_v[:, j], na_w_qkv[j], na_w_o[j], na_rpb[j])
            new_k.append(kc)
            new_v.append(vc)
        xc = xc + m_ctx[:, 2][:, None, :] * yc
        xs = xs + m_lat[:, 2][:, None, :] * ys
        hc = modulate(xc, norm_ffn[l], m_ctx[:, 3], m_ctx[:, 4])
        hs = modulate(xs, norm_ffn[l], m_lat[:, 3], m_lat[:, 4])
        xc = xc + m_ctx[:, 5][:, None, :] * peer_ffn(hc, peer_w_q[l], peer_sub_keys[l], peer_u[l], peer_v[l])
        xs = xs + m_lat[:, 5][:, None, :] * peer_ffn(hs, peer_w_q[l], peer_sub_keys[l], peer_u[l], peer_v[l])
    y_prompt = rmsnorm(xc, final_norm)
    y_sample = rmsnorm(xs, final_norm)
    return (y_prompt, y_sample, jnp.stack(new_wkv, axis=1), jnp.stack(new_k, axis=1), jnp.stack(new_v, axis=1))
```

```python
import functools
import math

import numpy as np
import jax
import jax.numpy as jnp
from jax import lax
from jax.experimental import pallas as pl
from jax.experimental.pallas import tpu as pltpu

F32 = jnp.float32
BF16 = jnp.bfloat16

D = 1024
N_MOD = 6
EPS = 1e-6
HEAD = 64
N_HEADS = D // HEAD
LNX_EPS = 64e-5
GRID_W = 64
WIN_ROWS = 8
WIN_COLS = 16
NA_SCALE = HEAD ** -0.5
NEG_INF = -1e30
FNET_GROUPS = 4
FNET_GD = D // FNET_GROUPS
PEER_KEYS = 128
PEER_HEADS = 8
PEER_TOPK = 16
N_EXPERTS = PEER_KEYS * PEER_KEYS
SCAN_L = 128

VMEM_LIMIT = 56 * 1024 * 1024


def _cp(sem, vmem=VMEM_LIMIT):
    return pltpu.CompilerParams(dimension_semantics=sem, vmem_limit_bytes=vmem)


def _dot(a, b):
    return jnp.dot(a, b, preferred_element_type=F32)


def _dot_nt(a, b):
    return lax.dot_general(a, b, (((1,), (1,)), ((), ())), preferred_element_type=F32)


def _split_dot(x, w):
    hi = x.astype(BF16)
    lo = (x - hi.astype(F32)).astype(BF16)
    return _dot(hi, w) + _dot(lo, w)


def _ada_kernel(c_ref, w_ref, b_ref, o_ref):
    c = c_ref[...]
    s = c * jax.nn.sigmoid(c)
    o_ref[0] = _dot(s.astype(BF16), w_ref[0].astype(BF16)) + b_ref[0]


def ada_all(cond16, ada_w, ada_b):
    depth = ada_w.shape[0]
    tn = 1024
    return pl.pallas_call(
        _ada_kernel,
        out_shape=jax.ShapeDtypeStruct((depth, 16, N_MOD * D), F32),
        grid=(depth, N_MOD * D // tn),
        in_specs=[pl.BlockSpec((16, D), lambda l, j: (0, 0)),
                  pl.BlockSpec((1, D, tn), lambda l, j: (l, 0, j)),
                  pl.BlockSpec((1, 1, tn), lambda l, j: (l, 0, j))],
        out_specs=pl.BlockSpec((1, 16, tn), lambda l, j: (l, 0, j)),
        compiler_params=_cp(("parallel", "parallel")),
        name="ada",
    )(cond16, ada_w, ada_b.reshape(depth, 1, N_MOD * D))


def _modulate(x, g, shift, scale):
    y = x * lax.rsqrt(jnp.mean(x * x, axis=-1, keepdims=True) + EPS)
    return (y * g) * (1 + scale) + shift


def _norm_mod_kernel(x_ref, g_ref, m_ref, o_ref, *, which):
    h = _modulate(x_ref[...], g_ref[...], m_ref[0, which:which + 1, :], m_ref[0, which + 1:which + 2, :])
    o_ref[...] = h.astype(o_ref.dtype)


def _norm_mod_t_kernel(x_ref, g_ref, m_ref, o_ref, ot_ref, *, which):
    h = _modulate(x_ref[...], g_ref[...], m_ref[0, which:which + 1, :], m_ref[0, which + 1:which + 2, :])
    o_ref[...] = h.astype(o_ref.dtype)
    ot_ref[...] = h.T.astype(ot_ref.dtype)


def norm_mod(x, g, mods, tpc, which, transposed=False, tm=512):
    m = x.shape[0]
    in_specs = [pl.BlockSpec((tm, D), lambda i: (i, 0)),
                pl.BlockSpec((1, D), lambda i: (0, 0)),
                pl.BlockSpec((1, N_MOD, D), lambda i: ((i * tm) // tpc, 0, 0))]
    if not transposed:
        return pl.pallas_call(
            functools.partial(_norm_mod_kernel, which=which),
            out_shape=jax.ShapeDtypeStruct((m, D), BF16),
            grid=(m // tm,), in_specs=in_specs,
            out_specs=pl.BlockSpec((tm, D), lambda i: (i, 0)),
            compiler_params=_cp(("parallel",)), name="norm_mod",
        )(x, g.reshape(1, D), mods)
    return pl.pallas_call(
        functools.partial(_norm_mod_t_kernel, which=which),
        out_shape=(jax.ShapeDtypeStruct((m, D), BF16), jax.ShapeDtypeStruct((D, m), BF16)),
        grid=(m // tm,), in_specs=in_specs,
        out_specs=(pl.BlockSpec((tm, D), lambda i: (i, 0)), pl.BlockSpec((D, tm), lambda i: (0, i))),
        compiler_params=_cp(("parallel",)), name="norm_mod_t",
    )(x, g.reshape(1, D), mods)


def _final_norm_kernel(x_ref, g_ref, o_ref):
    x = x_ref[...]
    o_ref[...] = x * lax.rsqrt(jnp.mean(x * x, axis=-1, keepdims=True) + EPS) * g_ref[...]


def rms_final(x, g, tm=512):
    m = x.shape[0]
    return pl.pallas_call(
        _final_norm_kernel,
        out_shape=jax.ShapeDtypeStruct((m, D), F32),
        grid=(m // tm,),
        in_specs=[pl.BlockSpec((tm, D), lambda i: (i, 0)), pl.BlockSpec((1, D), lambda i: (0, 0))],
        out_specs=pl.BlockSpec((tm, D), lambda i: (i, 0)),
        compiler_params=_cp(("parallel",)), name="final_norm",
    )(x, g.reshape(1, D))


def _mm_kernel(a_ref, w_ref, o_ref):
    o_ref[...] = _dot(a_ref[...], w_ref[...]).astype(o_ref.dtype)


def _mm_res_kernel(a_ref, w_ref, res_ref, m_ref, o_ref, *, gate):
    o_ref[...] = res_ref[...] + m_ref[0, gate:gate + 1, :] * _dot(a_ref[...], w_ref[...])


def matmul(a, w, out_dtype=F32, tm=512, tn=1024):
    m, n = a.shape[0], w.shape[1]
    tn = min(tn, n)
    return pl.pallas_call(
        _mm_kernel,
        out_shape=jax.ShapeDtypeStruct((m, n), out_dtype),
        grid=(m // tm, n // tn),
        in_specs=[pl.BlockSpec((tm, D), lambda i, j: (i, 0)), pl.BlockSpec((D, tn), lambda i, j: (0, j))],
        out_specs=pl.BlockSpec((tm, tn), lambda i, j: (i, j)),
        compiler_params=_cp(("parallel", "parallel")), name="matmul",
    )(a, w)


def matmul_res(a, w, res, mods, tpc, gate, tm=512):
    m = a.shape[0]
    return pl.pallas_call(
        functools.partial(_mm_res_kernel, gate=gate),
        out_shape=jax.ShapeDtypeStruct((m, D), F32),
        grid=(m // tm,),
        in_specs=[pl.BlockSpec((tm, D), lambda i: (i, 0)),
                  pl.BlockSpec((D, D), lambda i: (0, 0)),
                  pl.BlockSpec((tm, D), lambda i: (i, 0)),
                  pl.BlockSpec((1, N_MOD, D), lambda i: ((i * tm) // tpc, 0, 0))],
        out_specs=pl.BlockSpec((tm, D), lambda i: (i, 0)),
        compiler_params=_cp(("parallel",)), name="matmul_res",
    )(a, w, res, mods)


def _dft_mats(t):
    def cs(n):
        k = np.arange(n)
        ang = 2.0 * np.pi * ((k[:, None] * k[None, :]) % n) / n
        s = 1.0 / math.sqrt(n)
        return np.cos(ang) * s, np.sin(ang) * s
    cc, sc = cs(FNET_GD)
    ct, st = cs(t)
    return (jnp.asarray(np.concatenate([cc, sc], axis=1), BF16), jnp.asarray(ct, BF16), jnp.asarray(st, BF16))


def _dft_kernel(u_ref, cs_ref, ct_ref, st_ref, o_ref):
    p = _dot(u_ref[...].astype(BF16), cs_ref[...])
    pc = p[:, :FNET_GD].astype(BF16)
    ps = p[:, FNET_GD:].astype(BF16)
    o_ref[...] = (_dot(ct_ref[...], pc) - _dot(st_ref[...], ps)).astype(o_ref.dtype)


def fnet_dft(u, t):
    m = u.shape[0]
    cs, ct, st = _dft_mats(t)
    return pl.pallas_call(
        _dft_kernel,
        out_shape=jax.ShapeDtypeStruct((m, D), BF16),
        grid=(m // t, FNET_GROUPS),
        in_specs=[pl.BlockSpec((t, FNET_GD), lambda s, g: (s, g)),
                  pl.BlockSpec((FNET_GD, 2 * FNET_GD), lambda s, g: (0, 0)),
                  pl.BlockSpec((t, t), lambda s, g: (0, 0)),
                  pl.BlockSpec((t, t), lambda s, g: (0, 0))],
        out_specs=pl.BlockSpec((t, FNET_GD), lambda s, g: (s, g)),
        compiler_params=_cp(("parallel", "parallel")), name="fnet_dft",
    )(u, cs, ct, st)


def _head_ones():
    i = np.arange(D) // HEAD
    return jnp.asarray(i[:, None] == i[None, :], BF16)


def _rwkv_proj_kernel(x_ref, xp_ref, xn_ref, g_ref, m_ref, mu_ref, wr_ref, wk_ref, wv_ref, g1_ref, g2_ref,
                      w0_ref, w1_ref, w2_ref, a0_ref, a1_ref, a2_ref, kk_ref, ka_ref, rk_ref, ones_ref,
                      r_out, v_out, kkn_out, g_out, bonus_out, lw_out, kd_out, bd_out, *, tm, t):
    i = pl.program_id(0)
    shift = m_ref[0, 0:1, :]
    scale = m_ref[0, 1:2, :]
    g = g_ref[...]
    h = _modulate(x_ref[...], g, shift, scale)
    first = (i * tm) % t == 0
    last = ((i + 1) * tm) % t == 0
    hp = jnp.where(first, 0.0, _modulate(xp_ref[7:8, :], g, shift, scale))
    hn = jnp.where(last, 0.0, _modulate(xn_ref[0:1, :], g, shift, scale))
    row = lax.broadcasted_iota(jnp.int32, (tm, 1), 0)
    prev = jnp.where(row == 0, hp, pltpu.roll(h, 1, axis=0))
    nxt = jnp.where(row == tm - 1, hn, pltpu.roll(h, tm - 1, axis=0))
    xx = 0.5 * (prev + nxt) - h

    def mix(j):
        return (h + xx * mu_ref[j:j + 1, :]).astype(BF16)

    r = _dot(mix(0), wr_ref[...])
    k = _dot(mix(2), wk_ref[...])
    v = _dot(mix(3), wv_ref[...])
    gate = _dot(jax.nn.sigmoid(_dot(mix(5), g1_ref[...])).astype(BF16), g2_ref[...])
    xw = mix(1)
    xa = mix(4)
    ones = ones_ref[...]
    kk = k * kk_ref[...]
    kk = kk * lax.rsqrt(_split_dot(kk * kk, ones) + 1e-12)
    ksum = jnp.zeros_like(k)
    for j in range(2):
        w_raw = w0_ref[j:j + 1, :] + _dot(jnp.tanh(_dot(xw, w1_ref[j])).astype(BF16), w2_ref[j])
        lw_out[j] = -jnp.exp(-jax.nn.softplus(-w_raw) - 0.5)
        a = jax.nn.sigmoid(a0_ref[j:j + 1, :] + _dot(_dot(xa, a1_ref[j]).astype(BF16), a2_ref[j]))
        kd = k * (1 + (a - 1) * ka_ref[...])
        kd_out[j] = kd
        bd_out[j] = kk * a
        ksum = ksum + kd
    r_out[...] = r
    v_out[...] = v
    kkn_out[...] = kk
    g_out[...] = gate
    bonus_out[...] = _split_dot(r * ksum * rk_ref[...], ones) * v


def rwkv_proj(x, norm_g, mods, tpc, t, p, tm=256):
    m = x.shape[0]
    nb8 = m // 8
    full = lambda *shape: pl.BlockSpec(shape, lambda i: (0,) * len(shape))
    tok = pl.BlockSpec((tm, D), lambda i: (i, 0))
    tok2 = pl.BlockSpec((2, tm, D), lambda i: (0, i, 0))
    in_specs = [tok,
                pl.BlockSpec((8, D), lambda i: (jnp.maximum(i * (tm // 8) - 1, 0), 0)),
                pl.BlockSpec((8, D), lambda i: (jnp.minimum((i + 1) * (tm // 8), nb8 - 1), 0)),
                full(1, D),
                pl.BlockSpec((1, N_MOD, D), lambda i: ((i * tm) // tpc, 0, 0)),
                full(6, D), full(D, D), full(D, D), full(D, D), full(D, 128), full(128, D),
                full(2, D), full(2, D, 64), full(2, 64, D), full(2, D), full(2, D, 64), full(2, 64, D),
                full(1, D), full(1, D), full(1, D), full(D, D)]
    sd = jax.ShapeDtypeStruct
    return pl.pallas_call(
        functools.partial(_rwkv_proj_kernel, tm=tm, t=t),
        out_shape=(sd((m, D), F32),) * 5 + (sd((2, m, D), F32),) * 3,
        grid=(m // tm,), in_specs=in_specs,
        out_specs=(tok,) * 5 + (tok2,) * 3,
        compiler_params=_cp(("parallel",)), name="rwkv_proj",
    )(x, x, x, norm_g.reshape(1, D), mods, p["mu"], p["w_r"], p["w_k"], p["w_v"], p["g1"], p["g2"],
      p["w0"], p["w1"], p["w2"], p["a0"], p["a1"], p["a2"], p["k_k"], p["k_a"], p["r_k"], _head_ones())


def _rwkv_scan_kernel(r_ref, v_ref, kk_ref, lw_ref, kd_ref, b_ref, z0_ref, y_ref, zout_ref, z_scr, *, n_chunks):
    L = SCAN_L
    d = pl.program_id(1)
    c = pl.program_id(2)

    @pl.when(c == 0)
    def _():
        z_scr[...] = z0_ref[0, 0]

    row = lax.broadcasted_iota(jnp.int32, (L, L), 0)
    col = lax.broadcasted_iota(jnp.int32, (L, L), 1)
    fwd = d == 0
    order = (col - row) * (1 - 2 * d)
    before = order < 0
    upto = order <= 0
    cum_mat = upto.astype(BF16)
    same_head = (row // HEAD) == (col // HEAD)
    lane = lax.broadcasted_iota(jnp.int32, (1, 2 * HEAD), 1)
    head_mask = (lane < HEAD, lane >= HEAD)
    n_double = int(math.log2(L))

    def pair(p, carry):
        sl = pl.ds(pl.multiple_of(p * 2 * HEAD, 2 * HEAD), 2 * HEAD)
        r = r_ref[:, sl]
        v = v_ref[:, sl]
        lw = lw_ref[0, :, sl]
        cum = _split_dot_left(cum_mat, lw)
        tot = jnp.where(fwd, cum[L - 1:L, :], cum[0:1, :])
        alpha = -kk_ref[:, sl] * jnp.exp(cum - lw)
        inv = jnp.exp(-cum)
        beta = (b_ref[0, :, sl] * inv).astype(BF16)
        kappa = (kd_ref[0, :, sl] * inv).astype(BF16)
        rho = r * jnp.exp(cum)
        z = z_scr[p]
        zb = z.astype(BF16)
        rhs0 = _dot_nt(alpha.astype(BF16), zb)
        y = _dot_nt(rho.astype(BF16), zb)
        u_all = jnp.zeros((L, 2 * HEAD), F32)
        for hm in head_mask:
            am = jnp.where(hm, alpha, 0.0).astype(BF16)
            rm = jnp.where(hm, rho, 0.0).astype(BF16)
            vm = jnp.where(hm, v, 0.0).astype(BF16)
            n = jnp.where(before, _dot_nt(am, beta), 0.0)
            x = jnp.where(hm, rhs0, 0.0) + _dot(jnp.where(before, _dot_nt(am, kappa), 0.0).astype(BF16), vm)
            for it in range(n_double):
                nb = n.astype(BF16)
                x = x + _dot(nb, x.astype(BF16))
                if it + 1 < n_double:
                    n = _dot(nb, nb)
            y = y + _dot(jnp.where(upto, _dot_nt(rm, beta), 0.0).astype(BF16), x.astype(BF16))
            y = y + _dot(jnp.where(upto, _dot_nt(rm, kappa), 0.0).astype(BF16), vm)
            u_all = u_all + x
        inc = _dot(u_all.T.astype(BF16), beta) + _dot(v.T.astype(BF16), kappa)
        z_scr[p] = jnp.where(same_head, z + inc, 0.0) * jnp.exp(tot)
        y_ref[0, :, sl] = y
        return carry

    lax.fori_loop(0, N_HEADS // 2, pair, 0)

    @pl.when(c == n_chunks - 1)
    def _():
        zout_ref[0, 0] = z_scr[...]


def _split_dot_left(w, x):
    hi = x.astype(BF16)
    lo = (x - hi.astype(F32)).astype(BF16)
    return _dot(w, hi) + _dot(w, lo)


def rwkv_scan(r, v, kk, lw, kd, bd, z0, t):
    m = r.shape[0]
    n_seq = m // t
    nc = t // SCAN_L

    def blk(s, d, c):
        return s * nc + c + d * (nc - 1 - 2 * c)

    tok = pl.BlockSpec((SCAN_L, D), lambda s, d, c: (blk(s, d, c), 0))
    tok2 = pl.BlockSpec((1, SCAN_L, D), lambda s, d, c: (d, blk(s, d, c), 0))
    zspec = pl.BlockSpec((1, 1, N_HEADS // 2, 2 * HEAD, 2 * HEAD), lambda s, d, c: (s, d, 0, 0, 0))
    return pl.pallas_call(
        functools.partial(_rwkv_scan_kernel, n_chunks=nc),
        out_shape=(jax.ShapeDtypeStruct((2, m, D), F32), jax.ShapeDtypeStruct(z0.shape, F32)),
        grid=(n_seq, 2, nc),
        in_specs=[tok, tok, tok, tok2, tok2, tok2, zspec],
        out_specs=(tok2, zspec),
        scratch_shapes=[pltpu.VMEM((N_HEADS // 2, 2 * HEAD, 2 * HEAD), F32)],
        compiler_params=_cp(("parallel", "parallel", "arbitrary")), name="rwkv_scan",
    )(r, v, kk, lw, kd, bd, z0)


def _rwkv_post_kernel(y_ref, bonus_ref, g_ref, lg_ref, lb_ref, ones_ref, o_ref):
    ones = ones_ref[...]
    o = y_ref[0] + y_ref[1]
    cen = o - _split_dot(o, ones) * (1.0 / HEAD)
    var = _split_dot(cen * cen, ones) * (1.0 / HEAD)
    o = cen * lax.rsqrt(var + LNX_EPS) * lg_ref[...] + lb_ref[...] + bonus_ref[...]
    o_ref[...] = (o * g_ref[...]).astype(o_ref.dtype)


def rwkv_post(y, bonus, g, lnx_g, lnx_b, tm=256):
    m = bonus.shape[0]
    tok = pl.BlockSpec((tm, D), lambda i: (i, 0))
    row = pl.BlockSpec((1, D), lambda i: (0, 0))
    return pl.pallas_call(
        _rwkv_post_kernel,
        out_shape=jax.ShapeDtypeStruct((m, D), BF16),
        grid=(m // tm,),
        in_specs=[pl.BlockSpec((2, tm, D), lambda i: (0, i, 0)), tok, tok, row, row,
                  pl.BlockSpec((D, D), lambda i: (0, 0))],
        out_specs=tok,
        compiler_params=_cp(("parallel",)), name="rwkv_post",
    )(y, bonus, g, lnx_g.reshape(1, D), lnx_b.reshape(1, D), _head_ones())


def _pair_states(s):
    n = s.shape[0]
    s = s.reshape(n, 2, N_HEADS // 2, 2, HEAD, HEAD)
    z = jnp.zeros((n, 2, N_HEADS // 2, 2, HEAD, 2, HEAD), F32)
    z = z.at[:, :, :, 0, :, 0, :].set(s[:, :, :, 0]).at[:, :, :, 1, :, 1, :].set(s[:, :, :, 1])
    return z.reshape(n, 2, N_HEADS // 2, 2 * HEAD, 2 * HEAD)


def _unpair_states(z):
    n = z.shape[0]
    z = z.reshape(n, 2, N_HEADS // 2, 2, HEAD, 2, HEAD)
    s = jnp.stack([z[:, :, :, 0, :, 0, :], z[:, :, :, 1, :, 1, :]], axis=3)
    return s.reshape(n, 2, N_HEADS, HEAD, HEAD)


def rwkv_layer(x, norm_g, mods, tpc, t, s0, p):
    r, v, kk, g, bonus, lw, kd, bd = rwkv_proj(x, norm_g, mods, tpc, t, p)
    y, zf = rwkv_scan(r, v, kk, lw, kd, bd, _pair_states(s0), t)
    o = rwkv_post(y, bonus, g, p["lnx_g"], p["lnx_b"])
    return matmul_res(o, p["w_o"], x, mods, tpc, 2), _unpair_states(zf)


def _softmax_rows(parts):
    m = parts[0].max(axis=-1, keepdims=True)
    for s in parts[1:]:
        m = jnp.maximum(m, s.max(axis=-1, keepdims=True))
    es = [jnp.exp(s - m) for s in parts]
    den = es[0].sum(axis=-1, keepdims=True)
    for e in es[1:]:
        den = den + e.sum(axis=-1, keepdims=True)
    inv = 1.0 / den
    return [(e * inv).astype(BF16) for e in es]


def _na_ctx_kernel(q_ref, k_ref, v_ref, o_ref):
    for h in range(N_HEADS):
        sl = slice(h * HEAD, (h + 1) * HEAD)
        q = q_ref[:, sl].astype(BF16)
        k = k_ref[:, sl].astype(BF16)
        v = v_ref[:, sl].astype(BF16)
        (p,) = _softmax_rows([_dot_nt(q, k) * NA_SCALE])
        o_ref[:, sl] = _dot(p, v).astype(o_ref.dtype)


def na_ctx_attn(qkv, t):
    m = qkv.shape[0]
    return pl.pallas_call(
        _na_ctx_kernel,
        out_shape=jax.ShapeDtypeStruct((m, D), BF16),
        grid=(m // t,),
        in_specs=[pl.BlockSpec((t, D), lambda b: (b, 0)),
                  pl.BlockSpec((t, D), lambda b: (b, 1)),
                  pl.BlockSpec((t, D), lambda b: (b, 2))],
        out_specs=pl.BlockSpec((t, D), lambda b: (b, 0)),
        compiler_params=_cp(("parallel",)), name="na_ctx_attn",
    )(qkv, qkv, qkv)


def _win_start(r, rows):
    return jnp.clip(r - WIN_ROWS // 2, 0, rows - WIN_ROWS)


def _na_lat_kernel(q_ref, k_ref, v_ref, ck_ref, cv_ref, bias_ref, o_ref, *, rows):
    r = pl.program_id(1)
    start = pl.multiple_of(_win_start(r, rows) * GRID_W, GRID_W)
    n_loc = WIN_ROWS * GRID_W
    qc = lax.broadcasted_iota(jnp.int32, (GRID_W, n_loc), 0)
    kc = lax.broadcasted_iota(jnp.int32, (GRID_W, n_loc), 1) % GRID_W
    cs = jnp.clip(qc - WIN_COLS // 2, 0, GRID_W - WIN_COLS)
    valid = (kc >= cs) & (kc < cs + WIN_COLS)
    for h in range(N_HEADS):
        sl = slice(h * HEAD, (h + 1) * HEAD)
        q = q_ref[:, sl].astype(BF16)
        kw = k_ref[pl.ds(start, n_loc), sl].astype(BF16)
        vw = v_ref[pl.ds(start, n_loc), sl].astype(BF16)
        s_loc = jnp.where(valid, _dot_nt(q, kw) * NA_SCALE + bias_ref[0, h], NEG_INF)
        s_ctx = _dot_nt(q, ck_ref[:, sl].astype(BF16)) * NA_SCALE
        p_loc, p_ctx = _softmax_rows([s_loc, s_ctx])
        o_ref[:, sl] = (_dot(p_loc, vw) + _dot(p_ctx, cv_ref[:, sl].astype(BF16))).astype(o_ref.dtype)


def _na_bias(rpb, rows):
    wr = min(WIN_ROWS, rows)
    qc = np.arange(GRID_W)
    col_off = np.clip(qc[None, :] - qc[:, None], -(WIN_COLS - 1), WIN_COLS - 1) + WIN_COLS - 1
    ro = np.arange(WIN_ROWS)[:, None] + np.arange(wr)[None, :]
    b = rpb.astype(F32)[:, ro[:, None, :, None], col_off[None, :, None, :]]
    return jnp.transpose(b, (1, 0, 2, 3, 4)).reshape(WIN_ROWS, N_HEADS, GRID_W, wr * GRID_W)


def na_lat_attn(qkv, ck, cv, rpb, t, past):
    m = qkv.shape[0]
    rows = t // GRID_W
    assert rows >= WIN_ROWS
    n_seq = m // t
    n_loc = WIN_ROWS * GRID_W

    def bias_idx(b, r):
        return (_win_start(r, rows) - r + WIN_ROWS - 1, 0, 0, 0)

    return pl.pallas_call(
        functools.partial(_na_lat_kernel, rows=rows),
        out_shape=jax.ShapeDtypeStruct((m, D), BF16),
        grid=(n_seq, rows),
        in_specs=[pl.BlockSpec((GRID_W, D), lambda b, r: (b * rows + r, 0)),
                  pl.BlockSpec((t, D), lambda b, r: (b, 1)),
                  pl.BlockSpec((t, D), lambda b, r: (b, 2)),
                  pl.BlockSpec((past, D), lambda b, r: (b, 0)),
                  pl.BlockSpec((past, D), lambda b, r: (b, 0)),
                  pl.BlockSpec((1, N_HEADS, GRID_W, n_loc), bias_idx)],
        out_specs=pl.BlockSpec((GRID_W, D), lambda b, r: (b * rows + r, 0)),
        compiler_params=_cp(("parallel", "arbitrary")), name="na_lat_attn",
    )(qkv, qkv, qkv, ck, cv, _na_bias(rpb, rows))


def _top16_rows(x):
    n = x.shape[0]
    rowid = lax.broadcasted_iota(jnp.int32, x.shape, 0)
    out = []
    for _ in range(PEER_TOPK):
        m = x.max(axis=0, keepdims=True)
        out.append(m)
        first = jnp.where(x == m, rowid, n).min(axis=0, keepdims=True)
        x = jnp.where(rowid == first, -jnp.inf, x)
    return out


def _kth_largest_stats(x):
    m0 = x.max(axis=0, keepdims=True)
    remaining = jnp.full_like(m0, float(PEER_TOPK))
    z = jnp.zeros_like(m0)
    thr = m0
    for _ in range(PEER_TOPK):
        m = x.max(axis=0, keepdims=True)
        eq = x == m
        take = jnp.minimum(jnp.where(eq, 1.0, 0.0).sum(axis=0, keepdims=True), remaining)
        active = remaining > 0
        z = z + jnp.where(active, take * jnp.exp(m - m0), 0.0)
        thr = jnp.where(active, m, thr)
        remaining = remaining - take
        x = jnp.where(eq, -jnp.inf, x)
    return m0, thr, z


def _peer_stats_kernel(q_ref, keys_ref, s1_out, s2_out, e1_out, e2_out, thr_out):
    for h in range(PEER_HEADS):
        s, tops = [], []
        for c in range(2):
            hc = 2 * h + c
            q = q_ref[:, hc * PEER_KEYS:(hc + 1) * PEER_KEYS].astype(BF16)
            sc = _dot_nt(keys_ref[hc], q)
            (s1_out, s2_out)[c][h] = sc
            s.append(sc)
            tops.append(_top16_rows(sc))
        a1 = jnp.concatenate(tops[0], axis=0)
        a2 = jnp.concatenate(tops[1], axis=0)
        pieces = [a2 + tops[0][0], a2[:8] + tops[0][1]]
        pieces += [a2[:8] + tops[0][j] for j in range(2, 8)]
        pieces.append(a1[8:] + tops[1][0])
        cand = jnp.concatenate(pieces, axis=0)
        _, thr, z = _kth_largest_stats(cand)
        thr_out[h:h + 1, :] = thr
        e1_out[h] = jnp.exp(s[0] - tops[0][0]) / z
        e2_out[h] = jnp.exp(s[1] - tops[1][0])


def peer_stats(q, keys, tm=256):
    m = q.shape[0]
    nq = 2 * PEER_HEADS * PEER_KEYS
    sd = jax.ShapeDtypeStruct
    big = pl.BlockSpec((PEER_HEADS, PEER_KEYS, tm), lambda i: (0, 0, i))
    return pl.pallas_call(
        _peer_stats_kernel,
        out_shape=(sd((PEER_HEADS, PEER_KEYS, m), F32),) * 4 + (sd((PEER_HEADS, m), F32),),
        grid=(m // tm,),
        in_specs=[pl.BlockSpec((tm, nq), lambda i: (i, 0)),
                  pl.BlockSpec((2 * PEER_HEADS, PEER_KEYS, PEER_KEYS), lambda i: (0, 0, 0))],
        out_specs=(big,) * 4 + (pl.BlockSpec((PEER_HEADS, tm), lambda i: (0, i)),),
        compiler_params=_cp(("parallel",)), name="peer_stats",
    )(q, keys)


def _gelu(x):
    return 0.5 * x * (1.0 + lax.erf(x * (1.0 / math.sqrt(2.0))))


def _peer_dense_kernel(xt_ref, u_ref, vt_ref, s1_ref, s2_ref, e1_ref, e2_ref, thr_ref, res_ref, m_ref, o_ref,
                       a_scr, w_scr, acc_scr, *, tm, tn, gate):
    j = pl.program_id(1)

    @pl.when(j == 0)
    def _():
        acc_scr[...] = jnp.zeros_like(acc_scr)

    a_scr[...] = _dot(u_ref[...], xt_ref[...])
    for il in range(tn // PEER_KEYS):
        rows = slice(il * PEER_KEYS, (il + 1) * PEER_KEYS)
        for cb in range(tm // 128):
            lanes = slice(cb * 128, (cb + 1) * 128)
            g = jnp.zeros((PEER_KEYS, 128), F32)
            for h in range(PEER_HEADS):
                s1 = s1_ref[h, il:il + 1, lanes]
                e1 = e1_ref[h, il:il + 1, lanes]
                hit = (s2_ref[h, :, lanes] + s1) >= thr_ref[h:h + 1, lanes]
                g = g + jnp.where(hit, e2_ref[h, :, lanes] * e1, 0.0)
            w_scr[rows, lanes] = (g * _gelu(a_scr[rows, lanes])).astype(BF16)
    acc_scr[...] += _dot(vt_ref[...], w_scr[...])

    @pl.when(j == pl.num_programs(1) - 1)
    def _():
        o_ref[...] = res_ref[...] + m_ref[0, gate:gate + 1, :] * acc_scr[...].T


def peer_dense(xt, u, vt, s1, s2, e1, e2, thr, res, mods, tpc, gate, tm=256, tn=1024):
    m = xt.shape[1]
    n1 = tn // PEER_KEYS
    part = pl.BlockSpec((PEER_HEADS, n1, tm), lambda i, j: (0, j, i))
    full = pl.BlockSpec((PEER_HEADS, PEER_KEYS, tm), lambda i, j: (0, 0, i))
    return pl.pallas_call(
        functools.partial(_peer_dense_kernel, tm=tm, tn=tn, gate=gate),
        out_shape=jax.ShapeDtypeStruct((m, D), F32),
        grid=(m // tm, N_EXPERTS // tn),
        in_specs=[pl.BlockSpec((D, tm), lambda i, j: (0, i)),
                  pl.BlockSpec((tn, D), lambda i, j: (j, 0)),
                  pl.BlockSpec((D, tn), lambda i, j: (0, j)),
                  part, full, part, full,
                  pl.BlockSpec((PEER_HEADS, tm), lambda i, j: (0, i)),
                  pl.BlockSpec((tm, D), lambda i, j: (i, 0)),
                  pl.BlockSpec((1, N_MOD, D), lambda i, j: ((i * tm) // tpc, 0, 0))],
        out_specs=pl.BlockSpec((tm, D), lambda i, j: (i, 0)),
        scratch_shapes=[pltpu.VMEM((tn, tm), F32), pltpu.VMEM((tn, tm), BF16), pltpu.VMEM((D, tm), F32)],
        compiler_params=_cp(("parallel", "arbitrary")), name="peer_dense",
    )(xt, u, vt, s1, s2, e1, e2, thr, res, mods)


def peer_layer(x, norm_g, mods, tpc, wq, keys, u, vt):
    h, ht = norm_mod(x, norm_g, mods, tpc, 3, transposed=True)
    q = matmul(h, wq)
    s1, s2, e1, e2, thr = peer_stats(q, keys)
    return peer_dense(ht, u, vt, s1, s2, e1, e2, thr, x, mods, tpc, 5)


def kernel(x_prompt, x_sample, c, state_wkv, cache_k, cache_v, c_ctx, ada_w, ada_b, norm_mix, norm_ffn, fnet_w_in, fnet_w_out, rwkv_mu, rwkv_w_r, rwkv_w_k, rwkv_w_v, rwkv_w_o, rwkv_w0, rwkv_w1, rwkv_w2, rwkv_a0, rwkv_a1, rwkv_a2, rwkv_g1, rwkv_g2, rwkv_k_k, rwkv_k_a, rwkv_r_k, rwkv_lnx_g, rwkv_lnx_b, na_w_qkv, na_w_o, na_rpb, peer_w_q, peer_sub_keys, peer_u, peer_v, final_norm):
    nb_c, t_c, _ = x_prompt.shape
    nb_s, t_s, _ = x_sample.shape
    depth = ada_w.shape[0]
    past = cache_k.shape[2]
    bf = lambda w: w.astype(BF16)

    cond = jnp.concatenate([c_ctx[None, :], c, jnp.zeros((16 - 1 - nb_s, D), F32)], axis=0)
    mods_all = ada_all(cond, ada_w, ada_b).reshape(depth, 16, N_MOD, D)

    xc = x_prompt.reshape(nb_c * t_c, D)
    xs = x_sample.reshape(nb_s * t_s, D)
    tpc_c, tpc_s = nb_c * t_c, t_s
    new_wkv, new_k, new_v = [], [], []

    for l in range(depth):
        kind, j = l % 3, l // 3
        mc = mods_all[l, 0:1]
        ms = mods_all[l, 1:1 + nb_s]
        if kind == 0:
            w_in, w_out = bf(fnet_w_in[j]), bf(fnet_w_out[j])
            outs = []
            for x, m, tpc, t in ((xc, mc, tpc_c, t_c), (xs, ms, tpc_s, t_s)):
                h = norm_mod(x, norm_mix[l], m, tpc, 0)
                f = fnet_dft(matmul(h, w_in), t)
                outs.append(matmul_res(f, w_out, x, m, tpc, 2))
            xc, xs = outs
        elif kind == 1:
            p = dict(mu=rwkv_mu[j], w_r=bf(rwkv_w_r[j]), w_k=bf(rwkv_w_k[j]), w_v=bf(rwkv_w_v[j]), w_o=bf(rwkv_w_o[j]),
                     w0=rwkv_w0[j], w1=bf(rwkv_w1[j]), w2=bf(rwkv_w2[j]), a0=rwkv_a0[j], a1=bf(rwkv_a1[j]),
                     a2=bf(rwkv_a2[j]), g1=bf(rwkv_g1[j]), g2=bf(rwkv_g2[j]), k_k=rwkv_k_k[j].reshape(1, D),
                     k_a=rwkv_k_a[j].reshape(1, D), r_k=rwkv_r_k[j].reshape(1, D), lnx_g=rwkv_lnx_g[j],
                     lnx_b=rwkv_lnx_b[j])
            s_zero = jnp.zeros((nb_c, 2, N_HEADS, HEAD, HEAD), F32)
            xc, sc = rwkv_layer(xc, norm_mix[l], mc, tpc_c, t_c, s_zero, p)
            xs, _ = rwkv_layer(xs, norm_mix[l], ms, tpc_s, t_s, state_wkv[:, j], p)
            new_wkv.append(sc)
        else:
            w_qkv, w_o = bf(na_w_qkv[j]), bf(na_w_o[j])
            qkv_c = matmul(norm_mod(xc, norm_mix[l], mc, tpc_c, 0), w_qkv)
            qkv_s = matmul(norm_mod(xs, norm_mix[l], ms, tpc_s, 0), w_qkv)
            new_k.append(qkv_c[:, D:2 * D].reshape(nb_c, t_c, N_HEADS, HEAD))
            new_v.append(qkv_c[:, 2 * D:].reshape(nb_c, t_c, N_HEADS, HEAD))
            oc = na_ctx_attn(qkv_c, t_c)
            os_ = na_lat_attn(qkv_s, cache_k[:, j].reshape(nb_s * past, D), cache_v[:, j].reshape(nb_s * past, D),
                              na_rpb[j], t_s, past)
            xc = matmul_res(oc, w_o, xc, mc, tpc_c, 2)
            xs = matmul_res(os_, w_o, xs, ms, tpc_s, 2)
        wq = bf(peer_w_q[l])
        keys = bf(peer_sub_keys[l]).reshape(2 * PEER_HEADS, PEER_KEYS, PEER_KEYS)
        u = bf(peer_u[l])
        vt = bf(peer_v[l]).T
        xc = peer_layer(xc, norm_ffn[l], mc, tpc_c, wq, keys, u, vt)
        xs = peer_layer(xs, norm_ffn[l], ms, tpc_s, wq, keys, u, vt)

    y_prompt = rms_final(xc, final_norm).reshape(nb_c, t_c, D)
    y_sample = rms_final(xs, final_norm).reshape(nb_s, t_s, D)
    return (y_prompt, y_sample, jnp.stack(new_wkv, axis=1), jnp.stack(new_k, axis=1), jnp.stack(new_v, axis=1))
```

```python
import functools
import math

import numpy as np
import jax
import jax.numpy as jnp
from jax import lax
from jax.experimental import pallas as pl
from jax.experimental.pallas import tpu as pltpu

F32 = jnp.float32
BF16 = jnp.bfloat16

D = 1024
N_MOD = 6
EPS = 1e-6
HEAD = 64
N_HEADS = D // HEAD
LNX_EPS = 64e-5
GRID_W = 64
WIN_ROWS = 8
WIN_COLS = 16
NA_SCALE = HEAD ** -0.5
NEG_INF = -1e30
FNET_GROUPS = 4
FNET_GD = D // FNET_GROUPS
PEER_KEYS = 128
PEER_HEADS = 8
PEER_TOPK = 16
N_EXPERTS = PEER_KEYS * PEER_KEYS
PEER_ROWS = 32
SCAN_L = 128

VMEM_LIMIT = 56 * 1024 * 1024


def _cp(sem, vmem=VMEM_LIMIT):
    return pltpu.CompilerParams(dimension_semantics=sem, vmem_limit_bytes=vmem)


def _dot(a, b):
    return jnp.dot(a, b, preferred_element_type=F32)


def _dot_nt(a, b):
    return lax.dot_general(a, b, (((1,), (1,)), ((), ())), preferred_element_type=F32)


def _split_dot(x, w):
    hi = x.astype(BF16)
    lo = (x - hi.astype(F32)).astype(BF16)
    return _dot(hi, w) + _dot(lo, w)


def _ada_kernel(c_ref, w_ref, b_ref, o_ref):
    c = c_ref[...]
    s = c * jax.nn.sigmoid(c)
    o_ref[0] = _dot(s.astype(BF16), w_ref[0].astype(BF16)) + b_ref[0]


def ada_all(cond16, ada_w, ada_b):
    depth = ada_w.shape[0]
    tn = 1024
    return pl.pallas_call(
        _ada_kernel,
        out_shape=jax.ShapeDtypeStruct((depth, 16, N_MOD * D), F32),
        grid=(depth, N_MOD * D // tn),
        in_specs=[pl.BlockSpec((16, D), lambda l, j: (0, 0)),
                  pl.BlockSpec((1, D, tn), lambda l, j: (l, 0, j)),
                  pl.BlockSpec((1, 1, tn), lambda l, j: (l, 0, j))],
        out_specs=pl.BlockSpec((1, 16, tn), lambda l, j: (l, 0, j)),
        compiler_params=_cp(("parallel", "parallel")),
        name="ada",
    )(cond16, ada_w, ada_b.reshape(depth, 1, N_MOD * D))


def _modulate(x, g, shift, scale):
    y = x * lax.rsqrt(jnp.mean(x * x, axis=-1, keepdims=True) + EPS)
    return (y * g) * (1 + scale) + shift


def _norm_mod_kernel(x_ref, g_ref, m_ref, o_ref, *, which):
    h = _modulate(x_ref[...], g_ref[...], m_ref[0, which:which + 1, :], m_ref[0, which + 1:which + 2, :])
    o_ref[...] = h.astype(o_ref.dtype)


def _norm_mod_t_kernel(x_ref, g_ref, m_ref, o_ref, ot_ref, *, which):
    h = _modulate(x_ref[...], g_ref[...], m_ref[0, which:which + 1, :], m_ref[0, which + 1:which + 2, :])
    o_ref[...] = h.astype(o_ref.dtype)
    ot_ref[...] = h.T.astype(ot_ref.dtype)


def norm_mod(x, g, mods, tpc, which, transposed=False, tm=512):
    m = x.shape[0]
    in_specs = [pl.BlockSpec((tm, D), lambda i: (i, 0)),
                pl.BlockSpec((1, D), lambda i: (0, 0)),
                pl.BlockSpec((1, N_MOD, D), lambda i: ((i * tm) // tpc, 0, 0))]
    if not transposed:
        return pl.pallas_call(
            functools.partial(_norm_mod_kernel, which=which),
            out_shape=jax.ShapeDtypeStruct((m, D), BF16),
            grid=(m // tm,), in_specs=in_specs,
            out_specs=pl.BlockSpec((tm, D), lambda i: (i, 0)),
            compiler_params=_cp(("parallel",)), name="norm_mod",
        )(x, g.reshape(1, D), mods)
    return pl.pallas_call(
        functools.partial(_norm_mod_t_kernel, which=which),
        out_shape=(jax.ShapeDtypeStruct((m, D), BF16), jax.ShapeDtypeStruct((D, m), BF16)),
        grid=(m // tm,), in_specs=in_specs,
        out_specs=(pl.BlockSpec((tm, D), lambda i: (i, 0)), pl.BlockSpec((D, tm), lambda i: (0, i))),
        compiler_params=_cp(("parallel",)), name="norm_mod_t",
    )(x, g.reshape(1, D), mods)


def _final_norm_kernel(x_ref, g_ref, o_ref):
    x = x_ref[...]
    o_ref[...] = x * lax.rsqrt(jnp.mean(x * x, axis=-1, keepdims=True) + EPS) * g_ref[...]


def rms_final(x, g, tm=512):
    m = x.shape[0]
    return pl.pallas_call(
        _final_norm_kernel,
        out_shape=jax.ShapeDtypeStruct((m, D), F32),
        grid=(m // tm,),
        in_specs=[pl.BlockSpec((tm, D), lambda i: (i, 0)), pl.BlockSpec((1, D), lambda i: (0, 0))],
        out_specs=pl.BlockSpec((tm, D), lambda i: (i, 0)),
        compiler_params=_cp(("parallel",)), name="final_norm",
    )(x, g.reshape(1, D))


def _mm_kernel(a_ref, w_ref, o_ref):
    o_ref[...] = _dot(a_ref[...], w_ref[...]).astype(o_ref.dtype)


def _mm_res_kernel(a_ref, w_ref, res_ref, m_ref, o_ref, *, gate):
    o_ref[...] = res_ref[...] + m_ref[0, gate:gate + 1, :] * _dot(a_ref[...], w_ref[...])


def matmul(a, w, out_dtype=F32, tm=512, tn=1024):
    m, n = a.shape[0], w.shape[1]
    tn = min(tn, n)
    return pl.pallas_call(
        _mm_kernel,
        out_shape=jax.ShapeDtypeStruct((m, n), out_dtype),
        grid=(m // tm, n // tn),
        in_specs=[pl.BlockSpec((tm, D), lambda i, j: (i, 0)), pl.BlockSpec((D, tn), lambda i, j: (0, j))],
        out_specs=pl.BlockSpec((tm, tn), lambda i, j: (i, j)),
        compiler_params=_cp(("parallel", "parallel")), name="matmul",
    )(a, w)


def matmul_res(a, w, res, mods, tpc, gate, tm=512):
    m = a.shape[0]
    return pl.pallas_call(
        functools.partial(_mm_res_kernel, gate=gate),
        out_shape=jax.ShapeDtypeStruct((m, D), F32),
        grid=(m // tm,),
        in_specs=[pl.BlockSpec((tm, D), lambda i: (i, 0)),
                  pl.BlockSpec((D, D), lambda i: (0, 0)),
                  pl.BlockSpec((tm, D), lambda i: (i, 0)),
                  pl.BlockSpec((1, N_MOD, D), lambda i: ((i * tm) // tpc, 0, 0))],
        out_specs=pl.BlockSpec((tm, D), lambda i: (i, 0)),
        compiler_params=_cp(("parallel",)), name="matmul_res",
    )(a, w, res, mods)


def _dft_mats(t):
    def cs(n):
        k = np.arange(n)
        ang = 2.0 * np.pi * ((k[:, None] * k[None, :]) % n) / n
        s = 1.0 / math.sqrt(n)
        return np.cos(ang) * s, np.sin(ang) * s
    cc, sc = cs(FNET_GD)
    ct, st = cs(t)
    return (jnp.asarray(np.concatenate([cc, sc], axis=1), BF16), jnp.asarray(ct, BF16), jnp.asarray(st, BF16))


def _dft_kernel(u_ref, cs_ref, ct_ref, st_ref, o_ref):
    p = _dot(u_ref[...].astype(BF16), cs_ref[...])
    pc = p[:, :FNET_GD].astype(BF16)
    ps = p[:, FNET_GD:].astype(BF16)
    o_ref[...] = (_dot(ct_ref[...], pc) - _dot(st_ref[...], ps)).astype(o_ref.dtype)


def fnet_dft(u, t):
    m = u.shape[0]
    cs, ct, st = _dft_mats(t)
    return pl.pallas_call(
        _dft_kernel,
        out_shape=jax.ShapeDtypeStruct((m, D), BF16),
        grid=(m // t, FNET_GROUPS),
        in_specs=[pl.BlockSpec((t, FNET_GD), lambda s, g: (s, g)),
                  pl.BlockSpec((FNET_GD, 2 * FNET_GD), lambda s, g: (0, 0)),
                  pl.BlockSpec((t, t), lambda s, g: (0, 0)),
                  pl.BlockSpec((t, t), lambda s, g: (0, 0))],
        out_specs=pl.BlockSpec((t, FNET_GD), lambda s, g: (s, g)),
        compiler_params=_cp(("parallel", "parallel")), name="fnet_dft",
    )(u, cs, ct, st)


def _head_ones():
    i = np.arange(D) // HEAD
    return jnp.asarray(i[:, None] == i[None, :], BF16)


def _rwkv_proj_kernel(x_ref, xp_ref, xn_ref, g_ref, m_ref, mu_ref, wr_ref, wk_ref, wv_ref, g1_ref, g2_ref,
                      w0_ref, w1_ref, w2_ref, a0_ref, a1_ref, a2_ref, kk_ref, ka_ref, rk_ref, ones_ref,
                      r_out, v_out, kkn_out, g_out, bonus_out, lw_out, kd_out, bd_out, *, tm, t):
    i = pl.program_id(0)
    shift = m_ref[0, 0:1, :]
    scale = m_ref[0, 1:2, :]
    g = g_ref[...]
    h = _modulate(x_ref[...], g, shift, scale)
    first = (i * tm) % t == 0
    last = ((i + 1) * tm) % t == 0
    hp = jnp.where(first, 0.0, _modulate(xp_ref[7:8, :], g, shift, scale))
    hn = jnp.where(last, 0.0, _modulate(xn_ref[0:1, :], g, shift, scale))
    row = lax.broadcasted_iota(jnp.int32, (tm, 1), 0)
    prev = jnp.where(row == 0, hp, pltpu.roll(h, 1, axis=0))
    nxt = jnp.where(row == tm - 1, hn, pltpu.roll(h, tm - 1, axis=0))
    xx = 0.5 * (prev + nxt) - h

    def mix(j):
        return (h + xx * mu_ref[j:j + 1, :]).astype(BF16)

    r = _dot(mix(0), wr_ref[...])
    k = _dot(mix(2), wk_ref[...])
    v = _dot(mix(3), wv_ref[...])
    gate = _dot(jax.nn.sigmoid(_dot(mix(5), g1_ref[...])).astype(BF16), g2_ref[...])
    xw = mix(1)
    xa = mix(4)
    ones = ones_ref[...]
    kk = k * kk_ref[...]
    kk = kk * lax.rsqrt(_split_dot(kk * kk, ones) + 1e-12)
    ksum = jnp.zeros_like(k)
    for j in range(2):
        w_raw = w0_ref[j:j + 1, :] + _dot(jnp.tanh(_dot(xw, w1_ref[j])).astype(BF16), w2_ref[j])
        lw_out[j] = -jnp.exp(-jax.nn.softplus(-w_raw) - 0.5)
        a = jax.nn.sigmoid(a0_ref[j:j + 1, :] + _dot(_dot(xa, a1_ref[j]).astype(BF16), a2_ref[j]))
        kd = k * (1 + (a - 1) * ka_ref[...])
        kd_out[j] = kd
        bd_out[j] = kk * a
        ksum = ksum + kd
    r_out[...] = r
    v_out[...] = v
    kkn_out[...] = kk
    g_out[...] = gate
    bonus_out[...] = _split_dot(r * ksum * rk_ref[...], ones) * v


def rwkv_proj(x, norm_g, mods, tpc, t, p, tm=256):
    m = x.shape[0]
    nb8 = m // 8
    full = lambda *shape: pl.BlockSpec(shape, lambda i: (0,) * len(shape))
    tok = pl.BlockSpec((tm, D), lambda i: (i, 0))
    tok2 = pl.BlockSpec((2, tm, D), lambda i: (0, i, 0))
    in_specs = [tok,
                pl.BlockSpec((8, D), lambda i: (jnp.maximum(i * (tm // 8) - 1, 0), 0)),
                pl.BlockSpec((8, D), lambda i: (jnp.minimum((i + 1) * (tm // 8), nb8 - 1), 0)),
                full(1, D),
                pl.BlockSpec((1, N_MOD, D), lambda i: ((i * tm) // tpc, 0, 0)),
                full(6, D), full(D, D), full(D, D), full(D, D), full(D, 128), full(128, D),
                full(2, D), full(2, D, 64), full(2, 64, D), full(2, D), full(2, D, 64), full(2, 64, D),
                full(1, D), full(1, D), full(1, D), full(D, D)]
    sd = jax.ShapeDtypeStruct
    return pl.pallas_call(
        functools.partial(_rwkv_proj_kernel, tm=tm, t=t),
        out_shape=(sd((m, D), F32),) * 5 + (sd((2, m, D), F32),) * 3,
        grid=(m // tm,), in_specs=in_specs,
        out_specs=(tok,) * 5 + (tok2,) * 3,
        compiler_params=_cp(("parallel",)), name="rwkv_proj",
    )(x, x, x, norm_g.reshape(1, D), mods, p["mu"], p["w_r"], p["w_k"], p["w_v"], p["g1"], p["g2"],
      p["w0"], p["w1"], p["w2"], p["a0"], p["a1"], p["a2"], p["k_k"], p["k_a"], p["r_k"], _head_ones())


def _rwkv_scan_kernel(r_ref, v_ref, kk_ref, lw_ref, kd_ref, b_ref, z0_ref, y_ref, zout_ref, z_scr, *, n_chunks):
    L = SCAN_L
    d = pl.program_id(1)
    c = pl.program_id(2)

    @pl.when(c == 0)
    def _():
        z_scr[...] = z0_ref[0, 0]

    row = lax.broadcasted_iota(jnp.int32, (L, L), 0)
    col = lax.broadcasted_iota(jnp.int32, (L, L), 1)
    fwd = d == 0
    order = (col - row) * (1 - 2 * d)
    before = order < 0
    upto = order <= 0
    cum_mat = upto.astype(BF16)
    same_head = (row // HEAD) == (col // HEAD)
    lane = lax.broadcasted_iota(jnp.int32, (1, 2 * HEAD), 1)
    head_mask = (lane < HEAD, lane >= HEAD)
    n_double = int(math.log2(L))

    def prepare(p):
        sl = slice(p * 2 * HEAD, (p + 1) * 2 * HEAD)
        lw = lw_ref[0, :, sl]
        cum = _split_dot_left(cum_mat, lw)
        tot = jnp.where(fwd, cum[L - 1:L, :], cum[0:1, :])
        inv = jnp.exp(-cum)
        ar = jnp.concatenate([-kk_ref[:, sl] * jnp.exp(cum - lw), r_ref[:, sl] * jnp.exp(cum)], axis=0)
        bk = jnp.concatenate([b_ref[0, :, sl] * inv, kd_ref[0, :, sl] * inv], axis=0).astype(BF16)
        z = z_scr[p]
        base = _dot_nt(ar.astype(BF16), z.astype(BF16))
        return dict(ar=ar, bk=bk, z=z, base=base, v=v_ref[:, sl], tot=tot)

    def start_chain(pp, hm):
        g4 = _dot_nt(jnp.where(hm, pp["ar"], 0.0).astype(BF16), pp["bk"])
        vm = jnp.where(hm, pp["v"], 0.0).astype(BF16)
        n = jnp.where(before, g4[:L, :L], 0.0)
        x = jnp.where(hm, pp["base"][:L], 0.0) + _dot(jnp.where(before, g4[:L, L:], 0.0).astype(BF16), vm)
        out = jnp.concatenate([jnp.where(upto, g4[L:, :L], 0.0), jnp.where(upto, g4[L:, L:], 0.0)], axis=1)
        return dict(n=n, x=x, out=out.astype(BF16), vm=vm)

    pairs = [prepare(p) for p in range(N_HEADS // 2)]
    chains = [[start_chain(pp, hm) for hm in head_mask] for pp in pairs]
    for it in range(n_double):
        for ch in (ch for pair in chains for ch in pair):
            nb = ch["n"].astype(BF16)
            xb = ch["x"].astype(BF16)
            if it + 1 < n_double:
                res = _dot(nb, jnp.concatenate([xb, nb], axis=1))
                ch["x"] = ch["x"] + res[:, :2 * HEAD]
                ch["n"] = res[:, 2 * HEAD:]
            else:
                ch["x"] = ch["x"] + _dot(nb, xb)
    for p, (pp, pair) in enumerate(zip(pairs, chains)):
        y = pp["base"][L:]
        for ch in pair:
            y = y + _dot(ch["out"], jnp.concatenate([ch["x"].astype(BF16), ch["vm"]], axis=0))
        uv = jnp.concatenate([pair[0]["x"] + pair[1]["x"], pp["v"]], axis=0)
        inc = _dot(uv.T.astype(BF16), pp["bk"])
        z_scr[p] = jnp.where(same_head, pp["z"] + inc, 0.0) * jnp.exp(pp["tot"])
        y_ref[0, :, p * 2 * HEAD:(p + 1) * 2 * HEAD] = y

    @pl.when(c == n_chunks - 1)
    def _():
        zout_ref[0, 0] = z_scr[...]


def _split_dot_left(w, x):
    hi = x.astype(BF16)
    lo = (x - hi.astype(F32)).astype(BF16)
    return _dot(w, hi) + _dot(w, lo)


def rwkv_scan(r, v, kk, lw, kd, bd, z0, t):
    m = r.shape[0]
    n_seq = m // t
    nc = t // SCAN_L

    def blk(s, d, c):
        return s * nc + c + d * (nc - 1 - 2 * c)

    tok = pl.BlockSpec((SCAN_L, D), lambda s, d, c: (blk(s, d, c), 0))
    tok2 = pl.BlockSpec((1, SCAN_L, D), lambda s, d, c: (d, blk(s, d, c), 0))
    zspec = pl.BlockSpec((1, 1, N_HEADS // 2, 2 * HEAD, 2 * HEAD), lambda s, d, c: (s, d, 0, 0, 0))
    return pl.pallas_call(
        functools.partial(_rwkv_scan_kernel, n_chunks=nc),
        out_shape=(jax.ShapeDtypeStruct((2, m, D), F32), jax.ShapeDtypeStruct(z0.shape, F32)),
        grid=(n_seq, 2, nc),
        in_specs=[tok, tok, tok, tok2, tok2, tok2, zspec],
        out_specs=(tok2, zspec),
        scratch_shapes=[pltpu.VMEM((N_HEADS // 2, 2 * HEAD, 2 * HEAD), F32)],
        compiler_params=_cp(("parallel", "parallel", "arbitrary")), name="rwkv_scan",
    )(r, v, kk, lw, kd, bd, z0)


def _rwkv_post_kernel(y_ref, bonus_ref, g_ref, lg_ref, lb_ref, ones_ref, o_ref):
    ones = ones_ref[...]
    o = y_ref[0] + y_ref[1]
    cen = o - _split_dot(o, ones) * (1.0 / HEAD)
    var = _split_dot(cen * cen, ones) * (1.0 / HEAD)
    o = cen * lax.rsqrt(var + LNX_EPS) * lg_ref[...] + lb_ref[...] + bonus_ref[...]
    o_ref[...] = (o * g_ref[...]).astype(o_ref.dtype)


def rwkv_post(y, bonus, g, lnx_g, lnx_b, tm=256):
    m = bonus.shape[0]
    tok = pl.BlockSpec((tm, D), lambda i: (i, 0))
    row = pl.BlockSpec((1, D), lambda i: (0, 0))
    return pl.pallas_call(
        _rwkv_post_kernel,
        out_shape=jax.ShapeDtypeStruct((m, D), BF16),
        grid=(m // tm,),
        in_specs=[pl.BlockSpec((2, tm, D), lambda i: (0, i, 0)), tok, tok, row, row,
                  pl.BlockSpec((D, D), lambda i: (0, 0))],
        out_specs=tok,
        compiler_params=_cp(("parallel",)), name="rwkv_post",
    )(y, bonus, g, lnx_g.reshape(1, D), lnx_b.reshape(1, D), _head_ones())


def _pair_states(s):
    n = s.shape[0]
    s = s.reshape(n, 2, N_HEADS // 2, 2, HEAD, HEAD)
    z = jnp.zeros((n, 2, N_HEADS // 2, 2, HEAD, 2, HEAD), F32)
    z = z.at[:, :, :, 0, :, 0, :].set(s[:, :, :, 0]).at[:, :, :, 1, :, 1, :].set(s[:, :, :, 1])
    return z.reshape(n, 2, N_HEADS // 2, 2 * HEAD, 2 * HEAD)


def _unpair_states(z):
    n = z.shape[0]
    z = z.reshape(n, 2, N_HEADS // 2, 2, HEAD, 2, HEAD)
    s = jnp.stack([z[:, :, :, 0, :, 0, :], z[:, :, :, 1, :, 1, :]], axis=3)
    return s.reshape(n, 2, N_HEADS, HEAD, HEAD)


def rwkv_layer(x, norm_g, mods, tpc, t, s0, p):
    r, v, kk, g, bonus, lw, kd, bd = rwkv_proj(x, norm_g, mods, tpc, t, p)
    y, zf = rwkv_scan(r, v, kk, lw, kd, bd, _pair_states(s0), t)
    o = rwkv_post(y, bonus, g, p["lnx_g"], p["lnx_b"])
    return matmul_res(o, p["w_o"], x, mods, tpc, 2), _unpair_states(zf)


def _softmax_rows(parts):
    m = parts[0].max(axis=-1, keepdims=True)
    for s in parts[1:]:
        m = jnp.maximum(m, s.max(axis=-1, keepdims=True))
    es = [jnp.exp(s - m) for s in parts]
    den = es[0].sum(axis=-1, keepdims=True)
    for e in es[1:]:
        den = den + e.sum(axis=-1, keepdims=True)
    inv = 1.0 / den
    return [(e * inv).astype(BF16) for e in es]


def _na_ctx_kernel(q_ref, k_ref, v_ref, o_ref):
    for h in range(N_HEADS):
        sl = slice(h * HEAD, (h + 1) * HEAD)
        q = q_ref[:, sl].astype(BF16)
        k = k_ref[:, sl].astype(BF16)
        v = v_ref[:, sl].astype(BF16)
        (p,) = _softmax_rows([_dot_nt(q, k) * NA_SCALE])
        o_ref[:, sl] = _dot(p, v).astype(o_ref.dtype)


def na_ctx_attn(qkv, t):
    m = qkv.shape[0]
    return pl.pallas_call(
        _na_ctx_kernel,
        out_shape=jax.ShapeDtypeStruct((m, D), BF16),
        grid=(m // t,),
        in_specs=[pl.BlockSpec((t, D), lambda b: (b, 0)),
                  pl.BlockSpec((t, D), lambda b: (b, 1)),
                  pl.BlockSpec((t, D), lambda b: (b, 2))],
        out_specs=pl.BlockSpec((t, D), lambda b: (b, 0)),
        compiler_params=_cp(("parallel",)), name="na_ctx_attn",
    )(qkv, qkv, qkv)


def _win_start(r, rows):
    return jnp.clip(r - WIN_ROWS // 2, 0, rows - WIN_ROWS)


def _na_lat_kernel(q_ref, k_ref, v_ref, ck_ref, cv_ref, bias_ref, o_ref, *, rows):
    r = pl.program_id(1)
    start = pl.multiple_of(_win_start(r, rows) * GRID_W, GRID_W)
    n_loc = WIN_ROWS * GRID_W
    qc = lax.broadcasted_iota(jnp.int32, (GRID_W, n_loc), 0)
    kc = lax.broadcasted_iota(jnp.int32, (GRID_W, n_loc), 1) % GRID_W
    cs = jnp.clip(qc - WIN_COLS // 2, 0, GRID_W - WIN_COLS)
    valid = (kc >= cs) & (kc < cs + WIN_COLS)
    for h in range(N_HEADS):
        sl = slice(h * HEAD, (h + 1) * HEAD)
        q = q_ref[:, sl].astype(BF16)
        kw = k_ref[pl.ds(start, n_loc), sl].astype(BF16)
        vw = v_ref[pl.ds(start, n_loc), sl].astype(BF16)
        s_loc = jnp.where(valid, _dot_nt(q, kw) * NA_SCALE + bias_ref[0, h], NEG_INF)
        s_ctx = _dot_nt(q, ck_ref[:, sl].astype(BF16)) * NA_SCALE
        p_loc, p_ctx = _softmax_rows([s_loc, s_ctx])
        o_ref[:, sl] = (_dot(p_loc, vw) + _dot(p_ctx, cv_ref[:, sl].astype(BF16))).astype(o_ref.dtype)


def _na_bias(rpb, rows):
    wr = min(WIN_ROWS, rows)
    qc = np.arange(GRID_W)
    col_off = np.clip(qc[None, :] - qc[:, None], -(WIN_COLS - 1), WIN_COLS - 1) + WIN_COLS - 1
    onehot = jnp.asarray(np.arange(2 * WIN_COLS - 1)[:, None, None] == col_off[None], F32)
    b = jnp.einsum('hrc,cqk->hrqk', rpb.astype(F32), onehot, precision=lax.Precision.HIGHEST)
    b = jnp.stack([b[:, ro0:ro0 + wr] for ro0 in range(WIN_ROWS)], axis=0)
    return jnp.transpose(b, (0, 1, 3, 2, 4)).reshape(WIN_ROWS, N_HEADS, GRID_W, wr * GRID_W)


def na_lat_attn(qkv, ck, cv, rpb, t, past):
    m = qkv.shape[0]
    rows = t // GRID_W
    assert rows >= WIN_ROWS
    n_seq = m // t
    n_loc = WIN_ROWS * GRID_W

    def bias_idx(b, r):
        return (_win_start(r, rows) - r + WIN_ROWS - 1, 0, 0, 0)

    return pl.pallas_call(
        functools.partial(_na_lat_kernel, rows=rows),
        out_shape=jax.ShapeDtypeStruct((m, D), BF16),
        grid=(n_seq, rows),
        in_specs=[pl.BlockSpec((GRID_W, D), lambda b, r: (b * rows + r, 0)),
                  pl.BlockSpec((t, D), lambda b, r: (b, 1)),
                  pl.BlockSpec((t, D), lambda b, r: (b, 2)),
                  pl.BlockSpec((past, D), lambda b, r: (b, 0)),
                  pl.BlockSpec((past, D), lambda b, r: (b, 0)),
                  pl.BlockSpec((1, N_HEADS, GRID_W, n_loc), bias_idx)],
        out_specs=pl.BlockSpec((GRID_W, D), lambda b, r: (b * rows + r, 0)),
        compiler_params=_cp(("parallel", "arbitrary")), name="na_lat_attn",
    )(qkv, qkv, qkv, ck, cv, _na_bias(rpb, rows))


def _sort16_net():
    def merge(lo, hi, r):
        step = r * 2
        if step < hi - lo:
            yield from merge(lo, hi, step)
            yield from merge(lo + r, hi, step)
            yield from ((i, i + r) for i in range(lo + r, hi - r, step))
        else:
            yield (lo, lo + r)

    def sort(lo, hi):
        if hi - lo >= 1:
            mid = lo + (hi - lo) // 2
            yield from sort(lo, mid)
            yield from sort(mid + 1, hi)
            yield from merge(lo, hi, 1)

    return tuple(sort(0, PEER_TOPK - 1))


_SORT16 = _sort16_net()
_BITONIC16 = tuple((i, i + d) for d in (8, 4, 2, 1) for i in range(PEER_TOPK) if not i & d)


def _exchange(x, net):
    x = list(x)
    for i, j in net:
        x[i], x[j] = jnp.maximum(x[i], x[j]), jnp.minimum(x[i], x[j])
    return x


def _merge_top16(a, b):
    return _exchange([jnp.maximum(a[i], b[PEER_TOPK - 1 - i]) for i in range(PEER_TOPK)], _BITONIC16)


def _merge_sublanes(x):
    for shift in (4, 2, 1):
        x = _merge_top16(x, [pltpu.roll(v, shift, axis=0) for v in x])
    return x


def _peer_stats_kernel(q_ref, keys_ref, cut_out, e1_out, tab_out):
    tm = q_ref.shape[0]
    sub = lax.broadcasted_iota(jnp.int32, (8, tm), 0)
    ninf = jnp.full((8, tm), -jnp.inf, F32)
    for h in range(PEER_HEADS):
        s, tops = [], []
        for c in range(2):
            hc = 2 * h + c
            q = q_ref[:, hc * PEER_KEYS:(hc + 1) * PEER_KEYS].astype(BF16)
            sc = _dot_nt(keys_ref[hc], q)
            s.append(sc)
            groups = [sc[8 * i:8 * i + 8, :] for i in range(PEER_KEYS // 8)]
            tops.append(_merge_sublanes(_exchange(groups, _SORT16)))
        a1, a2 = tops
        a1col = a1[7]
        for jj in range(6, -1, -1):
            a1col = jnp.where(sub == jj, a1[jj], a1col)
        lists = [jnp.where(sub < min(8, PEER_TOPK // (k + 1)), a1col + a2[k], ninf) for k in range(PEER_TOPK)]
        tail = [a1[8 + k] + a2[0] for k in range(8)] + [ninf] * 8
        best = _merge_top16(_merge_sublanes(lists), tail)
        z = jnp.zeros((8, tm), F32)
        for b in best:
            z = z + jnp.exp(b - best[0])
        thr = best[PEER_TOPK - 1][0:1, :]
        cut = jnp.full((PEER_KEYS, tm), jnp.inf, F32)
        for k in range(PEER_TOPK):
            a2k = a2[k][0:1, :]
            cut = jnp.where(s[0] + a2k >= thr, a2k, cut)
        cut_out[h] = cut
        e1_out[h] = jnp.exp(s[0] - a1[0][0:1, :]) / z[0:1, :]
        e2 = jnp.exp(s[1] - a2[0][0:1, :])
        for lb in range(tm // 128):
            lanes = slice(lb * 128, (lb + 1) * 128)
            tab_out[lb, h, :, 0] = s[1][:, lanes].reshape(PEER_KEYS // 8, 8, 128)
            tab_out[lb, h, :, 1] = e2[:, lanes].reshape(PEER_KEYS // 8, 8, 128)


def peer_stats(q, keys, tm=256):
    m = q.shape[0]
    nq = 2 * PEER_HEADS * PEER_KEYS
    sd = jax.ShapeDtypeStruct
    big = pl.BlockSpec((PEER_HEADS, PEER_KEYS, tm), lambda i: (0, 0, i))
    tab_shape = (PEER_HEADS, PEER_KEYS // 8, 2, 8, 128)
    return pl.pallas_call(
        _peer_stats_kernel,
        out_shape=(sd((PEER_HEADS, PEER_KEYS, m), F32),) * 2 + (sd((m // 128,) + tab_shape, F32),),
        grid=(m // tm,),
        in_specs=[pl.BlockSpec((tm, nq), lambda i: (i, 0)),
                  pl.BlockSpec((2 * PEER_HEADS, PEER_KEYS, PEER_KEYS), lambda i: (0, 0, 0))],
        out_specs=(big,) * 2 + (pl.BlockSpec((tm // 128,) + tab_shape, lambda i: (i, 0, 0, 0, 0, 0)),),
        compiler_params=_cp(("parallel",)), name="peer_stats",
    )(q, keys)


def _gelu(x):
    return 0.5 * x * (1.0 + lax.erf(x * (1.0 / math.sqrt(2.0))))


def _peer_dense_kernel(xt_ref, u_ref, vt_ref, cut_ref, e1_ref, tab_ref, res_ref, m_ref, o_ref,
                       acc_scr, *, tm, tn, sub, gate):
    j = pl.program_id(1)

    @pl.when(j == 0)
    def _():
        acc_scr[...] = jnp.zeros_like(acc_scr)

    acc = acc_scr[...]
    for sb in range(tn // sub):
        rows = slice(sb * sub, (sb + 1) * sub)
        a = _dot(u_ref[rows, :], xt_ref[...])
        blocks = []
        for il in range(sub // PEER_KEYS):
            i1 = sb * (sub // PEER_KEYS) + il
            cols = []
            for cb in range(tm // 128):
                lanes = slice(cb * 128, (cb + 1) * 128)
                parts = []
                for rg in range(PEER_KEYS // PEER_ROWS):
                    grp = slice(rg * PEER_ROWS // 8, (rg + 1) * PEER_ROWS // 8)
                    g = jnp.zeros((PEER_ROWS, 128), F32)
                    for h in range(PEER_HEADS):
                        s2 = tab_ref[cb, h, grp, 0].reshape(PEER_ROWS, 128)
                        e2 = tab_ref[cb, h, grp, 1].reshape(PEER_ROWS, 128)
                        hit = s2 >= cut_ref[h, i1:i1 + 1, lanes]
                        g = g + jnp.where(hit, e2 * e1_ref[h, i1:i1 + 1, lanes], 0.0)
                    r0 = il * PEER_KEYS + rg * PEER_ROWS
                    parts.append((g * _gelu(a[r0:r0 + PEER_ROWS, lanes])).astype(BF16))
                cols.append(jnp.concatenate(parts, axis=0))
            blocks.append(jnp.concatenate(cols, axis=1))
        acc = acc + _dot(vt_ref[:, rows], jnp.concatenate(blocks, axis=0))
    acc_scr[...] = acc

    @pl.when(j == pl.num_programs(1) - 1)
    def _():
        o_ref[...] = res_ref[...] + m_ref[0, gate:gate + 1, :] * acc_scr[...].T


def peer_dense(xt, u, vt, cut, e1, tab, res, mods, tpc, gate, tm=512, tn=1024, sub=256):
    m = xt.shape[1]
    n1 = tn // PEER_KEYS
    part = pl.BlockSpec((PEER_HEADS, n1, tm), lambda i, j: (0, j, i))
    full = pl.BlockSpec((tm // 128,) + tab.shape[1:], lambda i, j: (i, 0, 0, 0, 0, 0))
    return pl.pallas_call(
        functools.partial(_peer_dense_kernel, tm=tm, tn=tn, sub=sub, gate=gate),
        out_shape=jax.ShapeDtypeStruct((m, D), F32),
        grid=(m // tm, N_EXPERTS // tn),
        in_specs=[pl.BlockSpec((D, tm), lambda i, j: (0, i)),
                  pl.BlockSpec((tn, D), lambda i, j: (j, 0)),
                  pl.BlockSpec((D, tn), lambda i, j: (0, j)),
                  part, part, full,
                  pl.BlockSpec((tm, D), lambda i, j: (i, 0)),
                  pl.BlockSpec((1, N_MOD, D), lambda i, j: ((i * tm) // tpc, 0, 0))],
        out_specs=pl.BlockSpec((tm, D), lambda i, j: (i, 0)),
        scratch_shapes=[pltpu.VMEM((D, tm), F32)],
        compiler_params=_cp(("parallel", "arbitrary")), name="peer_dense",
    )(xt, u, vt, cut, e1, tab, res, mods)


def peer_layer(x, norm_g, mods, tpc, wq, keys, u, vt):
    h, ht = norm_mod(x, norm_g, mods, tpc, 3, transposed=True)
    q = matmul(h, wq)
    cut, e1, tab = peer_stats(q, keys)
    return peer_dense(ht, u, vt, cut, e1, tab, x, mods, tpc, 5)


def kernel(x_prompt, x_sample, c, state_wkv, cache_k, cache_v, c_ctx, ada_w, ada_b, norm_mix, norm_ffn, fnet_w_in, fnet_w_out, rwkv_mu, rwkv_w_r, rwkv_w_k, rwkv_w_v, rwkv_w_o, rwkv_w0, rwkv_w1, rwkv_w2, rwkv_a0, rwkv_a1, rwkv_a2, rwkv_g1, rwkv_g2, rwkv_k_k, rwkv_k_a, rwkv_r_k, rwkv_lnx_g, rwkv_lnx_b, na_w_qkv, na_w_o, na_rpb, peer_w_q, peer_sub_keys, peer_u, peer_v, final_norm):
    nb_c, t_c, _ = x_prompt.shape
    nb_s, t_s, _ = x_sample.shape
    depth = ada_w.shape[0]
    past = cache_k.shape[2]
    bf = lambda w: w.astype(BF16)

    cond = jnp.concatenate([c_ctx[None, :], c, jnp.zeros((16 - 1 - nb_s, D), F32)], axis=0)
    mods_all = ada_all(cond, ada_w, ada_b).reshape(depth, 16, N_MOD, D)

    xc = x_prompt.reshape(nb_c * t_c, D)
    xs = x_sample.reshape(nb_s * t_s, D)
    tpc_c, tpc_s = nb_c * t_c, t_s
    new_wkv, new_k, new_v = [], [], []

    for l in range(depth):
        kind, j = l % 3, l // 3
        mc = mods_all[l, 0:1]
        ms = mods_all[l, 1:1 + nb_s]
        if kind == 0:
            w_in, w_out = bf(fnet_w_in[j]), bf(fnet_w_out[j])
            outs = []
            for x, m, tpc, t in ((xc, mc, tpc_c, t_c), (xs, ms, tpc_s, t_s)):
                h = norm_mod(x, norm_mix[l], m, tpc, 0)
                f = fnet_dft(matmul(h, w_in), t)
                outs.append(matmul_res(f, w_out, x, m, tpc, 2))
            xc, xs = outs
        elif kind == 1:
            p = dict(mu=rwkv_mu[j], w_r=bf(rwkv_w_r[j]), w_k=bf(rwkv_w_k[j]), w_v=bf(rwkv_w_v[j]), w_o=bf(rwkv_w_o[j]),
                     w0=rwkv_w0[j], w1=bf(rwkv_w1[j]), w2=bf(rwkv_w2[j]), a0=rwkv_a0[j], a1=bf(rwkv_a1[j]),
                     a2=bf(rwkv_a2[j]), g1=bf(rwkv_g1[j]), g2=bf(rwkv_g2[j]), k_k=rwkv_k_k[j].reshape(1, D),
                     k_a=rwkv_k_a[j].reshape(1, D), r_k=rwkv_r_k[j].reshape(1, D), lnx_g=rwkv_lnx_g[j],
                     lnx_b=rwkv_lnx_b[j])
            s_zero = jnp.zeros((nb_c, 2, N_HEADS, HEAD, HEAD), F32)
            xc, sc = rwkv_layer(xc, norm_mix[l], mc, tpc_c, t_c, s_zero, p)
            xs, _ = rwkv_layer(xs, norm_mix[l], ms, tpc_s, t_s, state_wkv[:, j], p)
            new_wkv.append(sc)
        else:
            w_qkv, w_o = bf(na_w_qkv[j]), bf(na_w_o[j])
            qkv_c = matmul(norm_mod(xc, norm_mix[l], mc, tpc_c, 0), w_qkv)
            qkv_s = matmul(norm_mod(xs, norm_mix[l], ms, tpc_s, 0), w_qkv)
            new_k.append(qkv_c[:, D:2 * D].reshape(nb_c, t_c, N_HEADS, HEAD))
            new_v.append(qkv_c[:, 2 * D:].reshape(nb_c, t_c, N_HEADS, HEAD))
            oc = na_ctx_attn(qkv_c, t_c)
            os_ = na_lat_attn(qkv_s, cache_k[:, j].reshape(nb_s * past, D), cache_v[:, j].reshape(nb_s * past, D),
                              na_rpb[j], t_s, past)
            xc = matmul_res(oc, w_o, xc, mc, tpc_c, 2)
            xs = matmul_res(os_, w_o, xs, ms, tpc_s, 2)
        wq = bf(peer_w_q[l])
        keys = bf(peer_sub_keys[l]).reshape(2 * PEER_HEADS, PEER_KEYS, PEER_KEYS)
        u = bf(peer_u[l])
        vt = bf(peer_v[l]).T
        xc = peer_layer(xc, norm_ffn[l], mc, tpc_c, wq, keys, u, vt)
        xs = peer_layer(xs, norm_ffn[l], ms, tpc_s, wq, keys, u, vt)

    y_prompt = rms_final(xc, final_norm).reshape(nb_c, t_c, D)
    y_sample = rms_final(xs, final_norm).reshape(nb_s, t_s, D)
    return (y_prompt, y_sample, jnp.stack(new_wkv, axis=1), jnp.stack(new_k, axis=1), jnp.stack(new_v, axis=1))
```

```python
import functools
import math

import numpy as np
import jax
import jax.numpy as jnp
from jax import lax
from jax.experimental import pallas as pl
from jax.experimental.pallas import tpu as pltpu

F32 = jnp.float32
BF16 = jnp.bfloat16

D = 1024
N_MOD = 6
EPS = 1e-6
HEAD = 64
N_HEADS = D // HEAD
LNX_EPS = 64e-5
GRID_W = 64
WIN_ROWS = 8
WIN_COLS = 16
NA_SCALE = HEAD ** -0.5
NEG_INF = -1e30
FNET_GROUPS = 4
FNET_GD = D // FNET_GROUPS
PEER_KEYS = 128
PEER_HEADS = 8
PEER_TOPK = 16
N_EXPERTS = PEER_KEYS * PEER_KEYS
PEER_ROWS = 32
SCAN_L = 128

VMEM_LIMIT = 56 * 1024 * 1024


def _cp(sem, vmem=VMEM_LIMIT):
    return pltpu.CompilerParams(dimension_semantics=sem, vmem_limit_bytes=vmem)


def _dot(a, b):
    return jnp.dot(a, b, preferred_element_type=F32)


def _dot_nt(a, b):
    return lax.dot_general(a, b, (((1,), (1,)), ((), ())), preferred_element_type=F32)


def _split_dot(x, w):
    hi = x.astype(BF16)
    lo = (x - hi.astype(F32)).astype(BF16)
    return _dot(hi, w) + _dot(lo, w)


def _ada_kernel(c_ref, w_ref, b_ref, o_ref):
    c = c_ref[...]
    s = c * jax.nn.sigmoid(c)
    o_ref[0] = _dot(s.astype(BF16), w_ref[0].astype(BF16)) + b_ref[0]


def ada_all(cond16, ada_w, ada_b):
    depth = ada_w.shape[0]
    tn = 1024
    return pl.pallas_call(
        _ada_kernel,
        out_shape=jax.ShapeDtypeStruct((depth, 16, N_MOD * D), F32),
        grid=(depth, N_MOD * D // tn),
        in_specs=[pl.BlockSpec((16, D), lambda l, j: (0, 0)),
                  pl.BlockSpec((1, D, tn), lambda l, j: (l, 0, j)),
                  pl.BlockSpec((1, 1, tn), lambda l, j: (l, 0, j))],
        out_specs=pl.BlockSpec((1, 16, tn), lambda l, j: (l, 0, j)),
        compiler_params=_cp(("parallel", "parallel")),
        name="ada",
    )(cond16, ada_w, ada_b.reshape(depth, 1, N_MOD * D))


def _modulate(x, g, shift, scale):
    y = x * lax.rsqrt(jnp.mean(x * x, axis=-1, keepdims=True) + EPS)
    return (y * g) * (1 + scale) + shift


def _norm_mod_kernel(x_ref, g_ref, m_ref, o_ref, *, which):
    h = _modulate(x_ref[...], g_ref[...], m_ref[0, which:which + 1, :], m_ref[0, which + 1:which + 2, :])
    o_ref[...] = h.astype(o_ref.dtype)


def _norm_mod_t_kernel(x_ref, g_ref, m_ref, o_ref, ot_ref, *, which):
    h = _modulate(x_ref[...], g_ref[...], m_ref[0, which:which + 1, :], m_ref[0, which + 1:which + 2, :])
    o_ref[...] = h.astype(o_ref.dtype)
    ot_ref[...] = h.T.astype(ot_ref.dtype)


def norm_mod(x, g, mods, tpc, which, transposed=False, tm=512):
    m = x.shape[0]
    in_specs = [pl.BlockSpec((tm, D), lambda i: (i, 0)),
                pl.BlockSpec((1, D), lambda i: (0, 0)),
                pl.BlockSpec((1, N_MOD, D), lambda i: ((i * tm) // tpc, 0, 0))]
    if not transposed:
        return pl.pallas_call(
            functools.partial(_norm_mod_kernel, which=which),
            out_shape=jax.ShapeDtypeStruct((m, D), BF16),
            grid=(m // tm,), in_specs=in_specs,
            out_specs=pl.BlockSpec((tm, D), lambda i: (i, 0)),
            compiler_params=_cp(("parallel",)), name="norm_mod",
        )(x, g.reshape(1, D), mods)
    return pl.pallas_call(
        functools.partial(_norm_mod_t_kernel, which=which),
        out_shape=(jax.ShapeDtypeStruct((m, D), BF16), jax.ShapeDtypeStruct((D, m), BF16)),
        grid=(m // tm,), in_specs=in_specs,
        out_specs=(pl.BlockSpec((tm, D), lambda i: (i, 0)), pl.BlockSpec((D, tm), lambda i: (0, i))),
        compiler_params=_cp(("parallel",)), name="norm_mod_t",
    )(x, g.reshape(1, D), mods)


def _final_norm_kernel(x_ref, g_ref, o_ref):
    x = x_ref[...]
    o_ref[...] = x * lax.rsqrt(jnp.mean(x * x, axis=-1, keepdims=True) + EPS) * g_ref[...]


def rms_final(x, g, tm=512):
    m = x.shape[0]
    return pl.pallas_call(
        _final_norm_kernel,
        out_shape=jax.ShapeDtypeStruct((m, D), F32),
        grid=(m // tm,),
        in_specs=[pl.BlockSpec((tm, D), lambda i: (i, 0)), pl.BlockSpec((1, D), lambda i: (0, 0))],
        out_specs=pl.BlockSpec((tm, D), lambda i: (i, 0)),
        compiler_params=_cp(("parallel",)), name="final_norm",
    )(x, g.reshape(1, D))


def _mm_kernel(a_ref, w_ref, o_ref):
    o_ref[...] = _dot(a_ref[...], w_ref[...]).astype(o_ref.dtype)


def _mm_res_kernel(a_ref, w_ref, res_ref, m_ref, o_ref, *, gate):
    o_ref[...] = res_ref[...] + m_ref[0, gate:gate + 1, :] * _dot(a_ref[...], w_ref[...])


def matmul(a, w, out_dtype=F32, tm=512, tn=1024):
    m, n = a.shape[0], w.shape[1]
    tn = min(tn, n)
    return pl.pallas_call(
        _mm_kernel,
        out_shape=jax.ShapeDtypeStruct((m, n), out_dtype),
        grid=(m // tm, n // tn),
        in_specs=[pl.BlockSpec((tm, D), lambda i, j: (i, 0)), pl.BlockSpec((D, tn), lambda i, j: (0, j))],
        out_specs=pl.BlockSpec((tm, tn), lambda i, j: (i, j)),
        compiler_params=_cp(("parallel", "parallel")), name="matmul",
    )(a, w)


def matmul_res(a, w, res, mods, tpc, gate, tm=512):
    m = a.shape[0]
    return pl.pallas_call(
        functools.partial(_mm_res_kernel, gate=gate),
        out_shape=jax.ShapeDtypeStruct((m, D), F32),
        grid=(m // tm,),
        in_specs=[pl.BlockSpec((tm, D), lambda i: (i, 0)),
                  pl.BlockSpec((D, D), lambda i: (0, 0)),
                  pl.BlockSpec((tm, D), lambda i: (i, 0)),
                  pl.BlockSpec((1, N_MOD, D), lambda i: ((i * tm) // tpc, 0, 0))],
        out_specs=pl.BlockSpec((tm, D), lambda i: (i, 0)),
        compiler_params=_cp(("parallel",)), name="matmul_res",
    )(a, w, res, mods)


def _dft_mats(t):
    def cs(n):
        k = np.arange(n)
        ang = 2.0 * np.pi * ((k[:, None] * k[None, :]) % n) / n
        s = 1.0 / math.sqrt(n)
        return np.cos(ang) * s, np.sin(ang) * s
    cc, sc = cs(FNET_GD)
    ct, st = cs(t)
    return (jnp.asarray(np.concatenate([cc, sc], axis=1), BF16), jnp.asarray(ct, BF16), jnp.asarray(st, BF16))


def _dft_kernel(u_ref, cs_ref, ct_ref, st_ref, o_ref):
    p = _dot(u_ref[...].astype(BF16), cs_ref[...])
    pc = p[:, :FNET_GD].astype(BF16)
    ps = p[:, FNET_GD:].astype(BF16)
    o_ref[...] = (_dot(ct_ref[...], pc) - _dot(st_ref[...], ps)).astype(o_ref.dtype)


def fnet_dft(u, t):
    m = u.shape[0]
    cs, ct, st = _dft_mats(t)
    return pl.pallas_call(
        _dft_kernel,
        out_shape=jax.ShapeDtypeStruct((m, D), BF16),
        grid=(m // t, FNET_GROUPS),
        in_specs=[pl.BlockSpec((t, FNET_GD), lambda s, g: (s, g)),
                  pl.BlockSpec((FNET_GD, 2 * FNET_GD), lambda s, g: (0, 0)),
                  pl.BlockSpec((t, t), lambda s, g: (0, 0)),
                  pl.BlockSpec((t, t), lambda s, g: (0, 0))],
        out_specs=pl.BlockSpec((t, FNET_GD), lambda s, g: (s, g)),
        compiler_params=_cp(("parallel", "parallel")), name="fnet_dft",
    )(u, cs, ct, st)


def _head_ones():
    i = np.arange(D) // HEAD
    return jnp.asarray(i[:, None] == i[None, :], BF16)


def _rwkv_proj_kernel(x_ref, xp_ref, xn_ref, g_ref, m_ref, mu_ref, wr_ref, wk_ref, wv_ref, g1_ref, g2_ref,
                      w0_ref, w1_ref, w2_ref, a0_ref, a1_ref, a2_ref, kk_ref, ka_ref, rk_ref, ones_ref,
                      r_out, v_out, kkn_out, g_out, bonus_out, lw_out, kd_out, bd_out, *, tm, t):
    i = pl.program_id(0)
    shift = m_ref[0, 0:1, :]
    scale = m_ref[0, 1:2, :]
    g = g_ref[...]
    h = _modulate(x_ref[...], g, shift, scale)
    first = (i * tm) % t == 0
    last = ((i + 1) * tm) % t == 0
    hp = jnp.where(first, 0.0, _modulate(xp_ref[7:8, :], g, shift, scale))
    hn = jnp.where(last, 0.0, _modulate(xn_ref[0:1, :], g, shift, scale))
    row = lax.broadcasted_iota(jnp.int32, (tm, 1), 0)
    prev = jnp.where(row == 0, hp, pltpu.roll(h, 1, axis=0))
    nxt = jnp.where(row == tm - 1, hn, pltpu.roll(h, tm - 1, axis=0))
    xx = 0.5 * (prev + nxt) - h

    def mix(j):
        return (h + xx * mu_ref[j:j + 1, :]).astype(BF16)

    r = _dot(mix(0), wr_ref[...])
    k = _dot(mix(2), wk_ref[...])
    v = _dot(mix(3), wv_ref[...])
    gate = _dot(jax.nn.sigmoid(_dot(mix(5), g1_ref[...])).astype(BF16), g2_ref[...])
    xw = mix(1)
    xa = mix(4)
    ones = ones_ref[...]
    kk = k * kk_ref[...]
    kk = kk * lax.rsqrt(_split_dot(kk * kk, ones) + 1e-12)
    ksum = jnp.zeros_like(k)
    for j in range(2):
        w_raw = w0_ref[j:j + 1, :] + _dot(jnp.tanh(_dot(xw, w1_ref[j])).astype(BF16), w2_ref[j])
        lw_out[j] = -jnp.exp(-jax.nn.softplus(-w_raw) - 0.5)
        a = jax.nn.sigmoid(a0_ref[j:j + 1, :] + _dot(_dot(xa, a1_ref[j]).astype(BF16), a2_ref[j]))
        kd = k * (1 + (a - 1) * ka_ref[...])
        kd_out[j] = kd
        bd_out[j] = kk * a
        ksum = ksum + kd
    r_out[...] = r
    v_out[...] = v
    kkn_out[...] = kk
    g_out[...] = gate
    bonus_out[...] = _split_dot(r * ksum * rk_ref[...], ones) * v


def rwkv_proj(x, norm_g, mods, tpc, t, p, tm=256):
    m = x.shape[0]
    nb8 = m // 8
    full = lambda *shape: pl.BlockSpec(shape, lambda i: (0,) * len(shape))
    tok = pl.BlockSpec((tm, D), lambda i: (i, 0))
    tok2 = pl.BlockSpec((2, tm, D), lambda i: (0, i, 0))
    in_specs = [tok,
                pl.BlockSpec((8, D), lambda i: (jnp.maximum(i * (tm // 8) - 1, 0), 0)),
                pl.BlockSpec((8, D), lambda i: (jnp.minimum((i + 1) * (tm // 8), nb8 - 1), 0)),
                full(1, D),
                pl.BlockSpec((1, N_MOD, D), lambda i: ((i * tm) // tpc, 0, 0)),
                full(6, D), full(D, D), full(D, D), full(D, D), full(D, 128), full(128, D),
                full(2, D), full(2, D, 64), full(2, 64, D), full(2, D), full(2, D, 64), full(2, 64, D),
                full(1, D), full(1, D), full(1, D), full(D, D)]
    sd = jax.ShapeDtypeStruct
    return pl.pallas_call(
        functools.partial(_rwkv_proj_kernel, tm=tm, t=t),
        out_shape=(sd((m, D), F32),) * 5 + (sd((2, m, D), F32),) * 3,
        grid=(m // tm,), in_specs=in_specs,
        out_specs=(tok,) * 5 + (tok2,) * 3,
        compiler_params=_cp(("parallel",)), name="rwkv_proj",
    )(x, x, x, norm_g.reshape(1, D), mods, p["mu"], p["w_r"], p["w_k"], p["w_v"], p["g1"], p["g2"],
      p["w0"], p["w1"], p["w2"], p["a0"], p["a1"], p["a2"], p["k_k"], p["k_a"], p["r_k"], _head_ones())


def _rwkv_scan_kernel(r_ref, v_ref, kk_ref, lw_ref, kd_ref, b_ref, z0_ref, y_ref, zout_ref, z_scr, *, n_chunks):
    L = SCAN_L
    d = pl.program_id(1)
    c = pl.program_id(2)

    @pl.when(c == 0)
    def _():
        z_scr[...] = z0_ref[0, 0]

    row = lax.broadcasted_iota(jnp.int32, (L, L), 0)
    col = lax.broadcasted_iota(jnp.int32, (L, L), 1)
    fwd = d == 0
    order = (col - row) * (1 - 2 * d)
    before = order < 0
    upto = order <= 0
    cum_mat = upto.astype(BF16)
    same_head = (row // HEAD) == (col // HEAD)
    lane = lax.broadcasted_iota(jnp.int32, (1, 2 * HEAD), 1)
    head_mask = (lane < HEAD, lane >= HEAD)
    n_double = int(math.log2(L))

    def prepare(p):
        sl = slice(p * 2 * HEAD, (p + 1) * 2 * HEAD)
        lw = lw_ref[0, :, sl]
        cum = _split_dot_left(cum_mat, lw)
        tot = jnp.where(fwd, cum[L - 1:L, :], cum[0:1, :])
        inv = jnp.exp(-cum)
        ar = jnp.concatenate([-kk_ref[:, sl] * jnp.exp(cum - lw), r_ref[:, sl] * jnp.exp(cum)], axis=0)
        bk = jnp.concatenate([b_ref[0, :, sl] * inv, kd_ref[0, :, sl] * inv], axis=0).astype(BF16)
        z = z_scr[p]
        base = _dot_nt(ar.astype(BF16), z.astype(BF16))
        return dict(ar=ar, bk=bk, z=z, base=base, v=v_ref[:, sl], tot=tot)

    def start_chain(pp, hm):
        g4 = _dot_nt(jnp.where(hm, pp["ar"], 0.0).astype(BF16), pp["bk"])
        vm = jnp.where(hm, pp["v"], 0.0).astype(BF16)
        n = jnp.where(before, g4[:L, :L], 0.0)
        x = jnp.where(hm, pp["base"][:L], 0.0) + _dot(jnp.where(before, g4[:L, L:], 0.0).astype(BF16), vm)
        out = jnp.concatenate([jnp.where(upto, g4[L:, :L], 0.0), jnp.where(upto, g4[L:, L:], 0.0)], axis=1)
        return dict(n=n, x=x, out=out.astype(BF16), vm=vm)

    pairs = [prepare(p) for p in range(N_HEADS // 2)]
    chains = [[start_chain(pp, hm) for hm in head_mask] for pp in pairs]
    for it in range(n_double):
        for ch in (ch for pair in chains for ch in pair):
            nb = ch["n"].astype(BF16)
            xb = ch["x"].astype(BF16)
            if it + 1 < n_double:
                res = _dot(nb, jnp.concatenate([xb, nb], axis=1))
                ch["x"] = ch["x"] + res[:, :2 * HEAD]
                ch["n"] = res[:, 2 * HEAD:]
            else:
                ch["x"] = ch["x"] + _dot(nb, xb)
    for p, (pp, pair) in enumerate(zip(pairs, chains)):
        y = pp["base"][L:]
        for ch in pair:
            y = y + _dot(ch["out"], jnp.concatenate([ch["x"].astype(BF16), ch["vm"]], axis=0))
        uv = jnp.concatenate([pair[0]["x"] + pair[1]["x"], pp["v"]], axis=0)
        inc = _dot(uv.T.astype(BF16), pp["bk"])
        z_scr[p] = jnp.where(same_head, pp["z"] + inc, 0.0) * jnp.exp(pp["tot"])
        y_ref[0, :, p * 2 * HEAD:(p + 1) * 2 * HEAD] = y

    @pl.when(c == n_chunks - 1)
    def _():
        zout_ref[0, 0] = z_scr[...]


def _split_dot_left(w, x):
    hi = x.astype(BF16)
    lo = (x - hi.astype(F32)).astype(BF16)
    return _dot(w, hi) + _dot(w, lo)


def rwkv_scan(r, v, kk, lw, kd, bd, z0, t):
    m = r.shape[0]
    n_seq = m // t
    nc = t // SCAN_L

    def blk(s, d, c):
        return s * nc + c + d * (nc - 1 - 2 * c)

    tok = pl.BlockSpec((SCAN_L, D), lambda s, d, c: (blk(s, d, c), 0))
    tok2 = pl.BlockSpec((1, SCAN_L, D), lambda s, d, c: (d, blk(s, d, c), 0))
    zspec = pl.BlockSpec((1, 1, N_HEADS // 2, 2 * HEAD, 2 * HEAD), lambda s, d, c: (s, d, 0, 0, 0))
    return pl.pallas_call(
        functools.partial(_rwkv_scan_kernel, n_chunks=nc),
        out_shape=(jax.ShapeDtypeStruct((2, m, D), F32), jax.ShapeDtypeStruct(z0.shape, F32)),
        grid=(n_seq, 2, nc),
        in_specs=[tok, tok, tok, tok2, tok2, tok2, zspec],
        out_specs=(tok2, zspec),
        scratch_shapes=[pltpu.VMEM((N_HEADS // 2, 2 * HEAD, 2 * HEAD), F32)],
        compiler_params=_cp(("parallel", "parallel", "arbitrary")), name="rwkv_scan",
    )(r, v, kk, lw, kd, bd, z0)


def _rwkv_post_kernel(y_ref, bonus_ref, g_ref, lg_ref, lb_ref, ones_ref, o_ref):
    ones = ones_ref[...]
    o = y_ref[0] + y_ref[1]
    cen = o - _split_dot(o, ones) * (1.0 / HEAD)
    var = _split_dot(cen * cen, ones) * (1.0 / HEAD)
    o = cen * lax.rsqrt(var + LNX_EPS) * lg_ref[...] + lb_ref[...] + bonus_ref[...]
    o_ref[...] = (o * g_ref[...]).astype(o_ref.dtype)


def rwkv_post(y, bonus, g, lnx_g, lnx_b, tm=256):
    m = bonus.shape[0]
    tok = pl.BlockSpec((tm, D), lambda i: (i, 0))
    row = pl.BlockSpec((1, D), lambda i: (0, 0))
    return pl.pallas_call(
        _rwkv_post_kernel,
        out_shape=jax.ShapeDtypeStruct((m, D), BF16),
        grid=(m // tm,),
        in_specs=[pl.BlockSpec((2, tm, D), lambda i: (0, i, 0)), tok, tok, row, row,
                  pl.BlockSpec((D, D), lambda i: (0, 0))],
        out_specs=tok,
        compiler_params=_cp(("parallel",)), name="rwkv_post",
    )(y, bonus, g, lnx_g.reshape(1, D), lnx_b.reshape(1, D), _head_ones())


def _pair_states(s):
    n = s.shape[0]
    s = s.reshape(n, 2, N_HEADS // 2, 2, HEAD, HEAD)
    z = jnp.zeros((n, 2, N_HEADS // 2, 2, HEAD, 2, HEAD), F32)
    z = z.at[:, :, :, 0, :, 0, :].set(s[:, :, :, 0]).at[:, :, :, 1, :, 1, :].set(s[:, :, :, 1])
    return z.reshape(n, 2, N_HEADS // 2, 2 * HEAD, 2 * HEAD)


def _unpair_states(z):
    n = z.shape[0]
    z = z.reshape(n, 2, N_HEADS // 2, 2, HEAD, 2, HEAD)
    s = jnp.stack([z[:, :, :, 0, :, 0, :], z[:, :, :, 1, :, 1, :]], axis=3)
    return s.reshape(n, 2, N_HEADS, HEAD, HEAD)


def rwkv_layer(x, norm_g, mods, tpc, t, s0, p):
    r, v, kk, g, bonus, lw, kd, bd = rwkv_proj(x, norm_g, mods, tpc, t, p)
    y, zf = rwkv_scan(r, v, kk, lw, kd, bd, _pair_states(s0), t)
    o = rwkv_post(y, bonus, g, p["lnx_g"], p["lnx_b"])
    return matmul_res(o, p["w_o"], x, mods, tpc, 2), _unpair_states(zf)


def _softmax_rows(parts):
    m = parts[0].max(axis=-1, keepdims=True)
    for s in parts[1:]:
        m = jnp.maximum(m, s.max(axis=-1, keepdims=True))
    es = [jnp.exp(s - m) for s in parts]
    den = es[0].sum(axis=-1, keepdims=True)
    for e in es[1:]:
        den = den + e.sum(axis=-1, keepdims=True)
    inv = 1.0 / den
    return [(e * inv).astype(BF16) for e in es]


def _na_ctx_kernel(q_ref, k_ref, v_ref, o_ref):
    for h in range(N_HEADS):
        sl = slice(h * HEAD, (h + 1) * HEAD)
        q = q_ref[:, sl].astype(BF16)
        k = k_ref[:, sl].astype(BF16)
        v = v_ref[:, sl].astype(BF16)
        (p,) = _softmax_rows([_dot_nt(q, k) * NA_SCALE])
        o_ref[:, sl] = _dot(p, v).astype(o_ref.dtype)


def na_ctx_attn(qkv, t):
    m = qkv.shape[0]
    return pl.pallas_call(
        _na_ctx_kernel,
        out_shape=jax.ShapeDtypeStruct((m, D), BF16),
        grid=(m // t,),
        in_specs=[pl.BlockSpec((t, D), lambda b: (b, 0)),
                  pl.BlockSpec((t, D), lambda b: (b, 1)),
                  pl.BlockSpec((t, D), lambda b: (b, 2))],
        out_specs=pl.BlockSpec((t, D), lambda b: (b, 0)),
        compiler_params=_cp(("parallel",)), name="na_ctx_attn",
    )(qkv, qkv, qkv)


def _win_start(r, rows):
    return jnp.clip(r - WIN_ROWS // 2, 0, rows - WIN_ROWS)


def _na_lat_kernel(q_ref, k_ref, v_ref, ck_ref, cv_ref, bias_ref, o_ref, *, rows):
    r = pl.program_id(1)
    start = pl.multiple_of(_win_start(r, rows) * GRID_W, GRID_W)
    n_loc = WIN_ROWS * GRID_W
    qc = lax.broadcasted_iota(jnp.int32, (GRID_W, n_loc), 0)
    kc = lax.broadcasted_iota(jnp.int32, (GRID_W, n_loc), 1) % GRID_W
    cs = jnp.clip(qc - WIN_COLS // 2, 0, GRID_W - WIN_COLS)
    valid = (kc >= cs) & (kc < cs + WIN_COLS)
    for h in range(N_HEADS):
        sl = slice(h * HEAD, (h + 1) * HEAD)
        q = q_ref[:, sl].astype(BF16)
        kw = k_ref[pl.ds(start, n_loc), sl].astype(BF16)
        vw = v_ref[pl.ds(start, n_loc), sl].astype(BF16)
        s_loc = jnp.where(valid, _dot_nt(q, kw) * NA_SCALE + bias_ref[0, h], NEG_INF)
        s_ctx = _dot_nt(q, ck_ref[:, sl].astype(BF16)) * NA_SCALE
        p_loc, p_ctx = _softmax_rows([s_loc, s_ctx])
        o_ref[:, sl] = (_dot(p_loc, vw) + _dot(p_ctx, cv_ref[:, sl].astype(BF16))).astype(o_ref.dtype)


def _na_bias(rpb, rows):
    wr = min(WIN_ROWS, rows)
    qc = np.arange(GRID_W)
    col_off = np.clip(qc[None, :] - qc[:, None], -(WIN_COLS - 1), WIN_COLS - 1) + WIN_COLS - 1
    onehot = jnp.asarray(np.arange(2 * WIN_COLS - 1)[:, None, None] == col_off[None], F32)
    b = jnp.einsum('hrc,cqk->hrqk', rpb.astype(F32), onehot, precision=lax.Precision.HIGHEST)
    b = jnp.stack([b[:, ro0:ro0 + wr] for ro0 in range(WIN_ROWS)], axis=0)
    return jnp.transpose(b, (0, 1, 3, 2, 4)).reshape(WIN_ROWS, N_HEADS, GRID_W, wr * GRID_W)


def na_lat_attn(qkv, ck, cv, rpb, t, past):
    m = qkv.shape[0]
    rows = t // GRID_W
    assert rows >= WIN_ROWS
    n_seq = m // t
    n_loc = WIN_ROWS * GRID_W

    def bias_idx(b, r):
        return (_win_start(r, rows) - r + WIN_ROWS - 1, 0, 0, 0)

    return pl.pallas_call(
        functools.partial(_na_lat_kernel, rows=rows),
        out_shape=jax.ShapeDtypeStruct((m, D), BF16),
        grid=(n_seq, rows),
        in_specs=[pl.BlockSpec((GRID_W, D), lambda b, r: (b * rows + r, 0)),
                  pl.BlockSpec((t, D), lambda b, r: (b, 1)),
                  pl.BlockSpec((t, D), lambda b, r: (b, 2)),
                  pl.BlockSpec((past, D), lambda b, r: (b, 0)),
                  pl.BlockSpec((past, D), lambda b, r: (b, 0)),
                  pl.BlockSpec((1, N_HEADS, GRID_W, n_loc), bias_idx)],
        out_specs=pl.BlockSpec((GRID_W, D), lambda b, r: (b * rows + r, 0)),
        compiler_params=_cp(("parallel", "arbitrary")), name="na_lat_attn",
    )(qkv, qkv, qkv, ck, cv, _na_bias(rpb, rows))


def _sort16_net():
    def merge(lo, hi, r):
        step = r * 2
        if step < hi - lo:
            yield from merge(lo, hi, step)
            yield from merge(lo + r, hi, step)
            yield from ((i, i + r) for i in range(lo + r, hi - r, step))
        else:
            yield (lo, lo + r)

    def sort(lo, hi):
        if hi - lo >= 1:
            mid = lo + (hi - lo) // 2
            yield from sort(lo, mid)
            yield from sort(mid + 1, hi)
            yield from merge(lo, hi, 1)

    return tuple(sort(0, PEER_TOPK - 1))


_SORT16 = _sort16_net()
_BITONIC16 = tuple((i, i + d) for d in (8, 4, 2, 1) for i in range(PEER_TOPK) if not i & d)


def _exchange(x, net):
    x = list(x)
    for i, j in net:
        x[i], x[j] = jnp.maximum(x[i], x[j]), jnp.minimum(x[i], x[j])
    return x


def _merge_top16(a, b):
    return _exchange([jnp.maximum(a[i], b[PEER_TOPK - 1 - i]) for i in range(PEER_TOPK)], _BITONIC16)


def _merge_sublanes(x):
    for shift in (4, 2, 1):
        x = _merge_top16(x, [pltpu.roll(v, shift, axis=0) for v in x])
    return x


def _peer_stats_kernel(q_ref, keys_ref, cnt_out, e1_out, tab_out):
    tm = q_ref.shape[0]
    sub = lax.broadcasted_iota(jnp.int32, (8, tm), 0)
    ninf = jnp.full((8, tm), -jnp.inf, F32)
    for h in range(PEER_HEADS):
        s, tops = [], []
        for c in range(2):
            hc = 2 * h + c
            q = q_ref[:, hc * PEER_KEYS:(hc + 1) * PEER_KEYS].astype(BF16)
            sc = _dot_nt(keys_ref[hc], q)
            s.append(sc)
            groups = [sc[8 * i:8 * i + 8, :] for i in range(PEER_KEYS // 8)]
            tops.append(_merge_sublanes(_exchange(groups, _SORT16)))
        a1, a2 = tops
        a1col = a1[7]
        for jj in range(6, -1, -1):
            a1col = jnp.where(sub == jj, a1[jj], a1col)
        lists = [jnp.where(sub < min(8, PEER_TOPK // (k + 1)), a1col + a2[k], ninf) for k in range(PEER_TOPK)]
        tail = [a1[8 + k] + a2[0] for k in range(8)] + [ninf] * 8
        best = _merge_top16(_merge_sublanes(lists), tail)
        z = jnp.zeros((8, tm), F32)
        for b in best:
            z = z + jnp.exp(b - best[0])
        thr = best[PEER_TOPK - 1][0:1, :]
        cnt = jnp.zeros((PEER_KEYS, tm), F32)
        rank = jnp.zeros((PEER_KEYS, tm), F32)
        for k in range(PEER_TOPK):
            a2k = a2[k][0:1, :]
            cnt = cnt + jnp.where(s[0] + a2k >= thr, 1.0, 0.0)
            rank = rank + jnp.where(a2k > s[1], 1.0, 0.0)
        cnt_out[h] = cnt
        e1_out[h] = jnp.exp(s[0] - a1[0][0:1, :]) / z[0:1, :]
        e2 = jnp.exp(s[1] - a2[0][0:1, :])
        for lb in range(tm // 128):
            lanes = slice(lb * 128, (lb + 1) * 128)
            tab_out[lb, h, :, 0] = rank[:, lanes].reshape(PEER_KEYS // 8, 8, 128)
            tab_out[lb, h, :, 1] = e2[:, lanes].reshape(PEER_KEYS // 8, 8, 128)


def peer_stats(q, keys, tm=256):
    m = q.shape[0]
    nq = 2 * PEER_HEADS * PEER_KEYS
    sd = jax.ShapeDtypeStruct
    big = pl.BlockSpec((PEER_HEADS, PEER_KEYS, tm), lambda i: (0, 0, i))
    tab_shape = (PEER_HEADS, PEER_KEYS // 8, 2, 8, 128)
    return pl.pallas_call(
        _peer_stats_kernel,
        out_shape=(sd((PEER_HEADS, PEER_KEYS, m), F32),) * 2 + (sd((m // 128,) + tab_shape, F32),),
        grid=(m // tm,),
        in_specs=[pl.BlockSpec((tm, nq), lambda i: (i, 0)),
                  pl.BlockSpec((2 * PEER_HEADS, PEER_KEYS, PEER_KEYS), lambda i: (0, 0, 0))],
        out_specs=(big,) * 2 + (pl.BlockSpec((tm // 128,) + tab_shape, lambda i: (i, 0, 0, 0, 0, 0)),),
        compiler_params=_cp(("parallel",)), name="peer_stats",
    )(q, keys)


def _gelu(x):
    return 0.5 * x * (1.0 + lax.erf(x * (1.0 / math.sqrt(2.0))))


def _peer_dense_kernel(xt_ref, u_ref, un_ref, vt_ref, vp_ref, cnt_ref, e1_ref, tab_ref, res_ref, m_ref, o_ref,
                       acc_scr, a_scr, w_scr, *, tm, tn, sub, gate):
    j = pl.program_id(1)
    n_sub = tn // sub
    n_cb = tm // 128
    out_rows = D // n_cb

    def act_matmul(sb):
        return _dot(u_ref[sb * sub:(sb + 1) * sub, :], xt_ref[...])

    @pl.when(j == 0)
    def _():
        acc_scr[...] = jnp.zeros_like(acc_scr)
        w_scr[...] = jnp.zeros_like(w_scr)
        a_scr[...] = act_matmul(0)

    def weights(sb, cb, a):
        lanes = slice(cb * 128, (cb + 1) * 128)
        parts = []
        for il in range(sub // PEER_KEYS):
            i1 = sb * (sub // PEER_KEYS) + il
            for rg in range(PEER_KEYS // PEER_ROWS):
                grp = slice(rg * PEER_ROWS // 8, (rg + 1) * PEER_ROWS // 8)
                g = jnp.zeros((PEER_ROWS, 128), F32)
                for h in range(PEER_HEADS):
                    rank = tab_ref[cb, h, grp, 0].reshape(PEER_ROWS, 128)
                    e2 = tab_ref[cb, h, grp, 1].reshape(PEER_ROWS, 128)
                    hit = rank < cnt_ref[h, i1:i1 + 1, lanes]
                    g = g + jnp.where(hit, e2 * e1_ref[h, i1:i1 + 1, lanes], 0.0)
                r0 = il * PEER_KEYS + rg * PEER_ROWS
                parts.append((g * _gelu(a[r0:r0 + PEER_ROWS, lanes])).astype(BF16))
        return jnp.concatenate(parts, axis=0)

    def accumulate(sb, q, w):
        rows = slice(q * out_rows, (q + 1) * out_rows)
        if sb < 0:
            acc_scr[rows, :] += _dot(vp_ref[rows, :], w)
        else:
            acc_scr[rows, :] += _dot(vt_ref[rows, sb * sub:(sb + 1) * sub], w)

    a = a_scr[...]
    w_prev = w_scr[...]
    for sb in range(n_sub):
        cols = []
        for cb in range(n_cb):
            if cb == 1:
                a_next = act_matmul(sb + 1) if sb + 1 < n_sub else _dot(un_ref[...], xt_ref[...])
            cols.append(weights(sb, cb, a))
            accumulate(sb - 1, cb, w_prev)
        w_prev = jnp.concatenate(cols, axis=1)
        a = a_next
    a_scr[...] = a
    w_scr[...] = w_prev

    @pl.when(j == pl.num_programs(1) - 1)
    def _():
        for q in range(n_cb):
            accumulate(n_sub - 1, q, w_scr[...])
        o_ref[...] = res_ref[...] + m_ref[0, gate:gate + 1, :] * acc_scr[...].T


def peer_dense(xt, u, vt, cnt, e1, tab, res, mods, tpc, gate, tm=512, tn=1024, sub=256):
    m = xt.shape[1]
    n1 = tn // PEER_KEYS
    part = pl.BlockSpec((PEER_HEADS, n1, tm), lambda i, j: (0, j, i))
    full = pl.BlockSpec((tm // 128,) + tab.shape[1:], lambda i, j: (i, 0, 0, 0, 0, 0))
    n_sub = tn // sub
    last_sub = N_EXPERTS // sub - 1
    return pl.pallas_call(
        functools.partial(_peer_dense_kernel, tm=tm, tn=tn, sub=sub, gate=gate),
        out_shape=jax.ShapeDtypeStruct((m, D), F32),
        grid=(m // tm, N_EXPERTS // tn),
        in_specs=[pl.BlockSpec((D, tm), lambda i, j: (0, i)),
                  pl.BlockSpec((tn, D), lambda i, j: (j, 0)),
                  pl.BlockSpec((sub, D), lambda i, j: (jnp.minimum((j + 1) * n_sub, last_sub), 0)),
                  pl.BlockSpec((D, tn), lambda i, j: (0, j)),
                  pl.BlockSpec((D, sub), lambda i, j: (0, jnp.maximum(j * n_sub - 1, 0))),
                  part, part, full,
                  pl.BlockSpec((tm, D), lambda i, j: (i, 0)),
                  pl.BlockSpec((1, N_MOD, D), lambda i, j: ((i * tm) // tpc, 0, 0))],
        out_specs=pl.BlockSpec((tm, D), lambda i, j: (i, 0)),
        scratch_shapes=[pltpu.VMEM((D, tm), F32), pltpu.VMEM((sub, tm), F32), pltpu.VMEM((sub, tm), BF16)],
        compiler_params=_cp(("parallel", "arbitrary")), name="peer_dense",
    )(xt, u, u, vt, vt, cnt, e1, tab, res, mods)


def peer_layer(x, norm_g, mods, tpc, wq, keys, u, vt):
    h, ht = norm_mod(x, norm_g, mods, tpc, 3, transposed=True)
    q = matmul(h, wq)
    cnt, e1, tab = peer_stats(q, keys)
    return peer_dense(ht, u, vt, cnt, e1, tab, x, mods, tpc, 5)


def kernel(x_prompt, x_sample, c, state_wkv, cache_k, cache_v, c_ctx, ada_w, ada_b, norm_mix, norm_ffn, fnet_w_in, fnet_w_out, rwkv_mu, rwkv_w_r, rwkv_w_k, rwkv_w_v, rwkv_w_o, rwkv_w0, rwkv_w1, rwkv_w2, rwkv_a0, rwkv_a1, rwkv_a2, rwkv_g1, rwkv_g2, rwkv_k_k, rwkv_k_a, rwkv_r_k, rwkv_lnx_g, rwkv_lnx_b, na_w_qkv, na_w_o, na_rpb, peer_w_q, peer_sub_keys, peer_u, peer_v, final_norm):
    nb_c, t_c, _ = x_prompt.shape
    nb_s, t_s, _ = x_sample.shape
    depth = ada_w.shape[0]
    past = cache_k.shape[2]
    bf = lambda w: w.astype(BF16)

    cond = jnp.concatenate([c_ctx[None, :], c, jnp.zeros((16 - 1 - nb_s, D), F32)], axis=0)
    mods_all = ada_all(cond, ada_w, ada_b).reshape(depth, 16, N_MOD, D)

    xc = x_prompt.reshape(nb_c * t_c, D)
    xs = x_sample.reshape(nb_s * t_s, D)
    tpc_c, tpc_s = nb_c * t_c, t_s
    new_wkv, new_k, new_v = [], [], []

    for l in range(depth):
        kind, j = l % 3, l // 3
        mc = mods_all[l, 0:1]
        ms = mods_all[l, 1:1 + nb_s]
        if kind == 0:
            w_in, w_out = bf(fnet_w_in[j]), bf(fnet_w_out[j])
            outs = []
            for x, m, tpc, t in ((xc, mc, tpc_c, t_c), (xs, ms, tpc_s, t_s)):
                h = norm_mod(x, norm_mix[l], m, tpc, 0)
                f = fnet_dft(matmul(h, w_in), t)
                outs.append(matmul_res(f, w_out, x, m, tpc, 2))
            xc, xs = outs
        elif kind == 1:
            p = dict(mu=rwkv_mu[j], w_r=bf(rwkv_w_r[j]), w_k=bf(rwkv_w_k[j]), w_v=bf(rwkv_w_v[j]), w_o=bf(rwkv_w_o[j]),
                     w0=rwkv_w0[j], w1=bf(rwkv_w1[j]), w2=bf(rwkv_w2[j]), a0=rwkv_a0[j], a1=bf(rwkv_a1[j]),
                     a2=bf(rwkv_a2[j]), g1=bf(rwkv_g1[j]), g2=bf(rwkv_g2[j]), k_k=rwkv_k_k[j].reshape(1, D),
                     k_a=rwkv_k_a[j].reshape(1, D), r_k=rwkv_r_k[j].reshape(1, D), lnx_g=rwkv_lnx_g[j],
                     lnx_b=rwkv_lnx_b[j])
            s_zero = jnp.zeros((nb_c, 2, N_HEADS, HEAD, HEAD), F32)
            xc, sc = rwkv_layer(xc, norm_mix[l], mc, tpc_c, t_c, s_zero, p)
            xs, _ = rwkv_layer(xs, norm_mix[l], ms, tpc_s, t_s, state_wkv[:, j], p)
            new_wkv.append(sc)
        else:
            w_qkv, w_o = bf(na_w_qkv[j]), bf(na_w_o[j])
            qkv_c = matmul(norm_mod(xc, norm_mix[l], mc, tpc_c, 0), w_qkv)
            qkv_s = matmul(norm_mod(xs, norm_mix[l], ms, tpc_s, 0), w_qkv)
            new_k.append(qkv_c[:, D:2 * D].reshape(nb_c, t_c, N_HEADS, HEAD))
            new_v.append(qkv_c[:, 2 * D:].reshape(nb_c, t_c, N_HEADS, HEAD))
            oc = na_ctx_attn(qkv_c, t_c)
            os_ = na_lat_attn(qkv_s, cache_k[:, j].reshape(nb_s * past, D), cache_v[:, j].reshape(nb_s * past, D),
                              na_rpb[j], t_s, past)
            xc = matmul_res(oc, w_o, xc, mc, tpc_c, 2)
            xs = matmul_res(os_, w_o, xs, ms, tpc_s, 2)
        wq = bf(peer_w_q[l])
        keys = bf(peer_sub_keys[l]).reshape(2 * PEER_HEADS, PEER_KEYS, PEER_KEYS)
        u = bf(peer_u[l])
        vt = bf(peer_v[l]).T
        xc = peer_layer(xc, norm_ffn[l], mc, tpc_c, wq, keys, u, vt)
        xs = peer_layer(xs, norm_ffn[l], ms, tpc_s, wq, keys, u, vt)

    y_prompt = rms_final(xc, final_norm).reshape(nb_c, t_c, D)
    y_sample = rms_final(xs, final_norm).reshape(nb_s, t_s, D)
    return (y_prompt, y_sample, jnp.stack(new_wkv, axis=1), jnp.stack(new_k, axis=1), jnp.stack(new_v, axis=1))
```

```python
import functools
import math

import numpy as np
import jax
import jax.numpy as jnp
from jax import lax
from jax.experimental import pallas as pl
from jax.experimental.pallas import tpu as pltpu

F32 = jnp.float32
BF16 = jnp.bfloat16

D = 1024
N_MOD = 6
EPS = 1e-6
HEAD = 64
N_HEADS = D // HEAD
LNX_EPS = 64e-5
GRID_W = 64
WIN_ROWS = 8
WIN_COLS = 16
NA_SCALE = HEAD ** -0.5
NEG_INF = -1e30
FNET_GROUPS = 4
FNET_GD = D // FNET_GROUPS
PEER_KEYS = 128
PEER_HEADS = 8
PEER_TOPK = 16
N_EXPERTS = PEER_KEYS * PEER_KEYS
PEER_ROWS = 32
SCAN_L = 128

VMEM_LIMIT = 56 * 1024 * 1024


def _cp(sem, vmem=VMEM_LIMIT):
    return pltpu.CompilerParams(dimension_semantics=sem, vmem_limit_bytes=vmem)


def _dot(a, b):
    return jnp.dot(a, b, preferred_element_type=F32)


def _dot_nt(a, b):
    return lax.dot_general(a, b, (((1,), (1,)), ((), ())), preferred_element_type=F32)


def _split_dot(x, w):
    hi = x.astype(BF16)
    lo = (x - hi.astype(F32)).astype(BF16)
    return _dot(hi, w) + _dot(lo, w)


def _ada_kernel(c_ref, w_ref, b_ref, o_ref):
    c = c_ref[...]
    s = c * jax.nn.sigmoid(c)
    o_ref[0] = _dot(s.astype(BF16), w_ref[0].astype(BF16)) + b_ref[0]


def ada_all(cond16, ada_w, ada_b):
    depth = ada_w.shape[0]
    tn = 1024
    return pl.pallas_call(
        _ada_kernel,
        out_shape=jax.ShapeDtypeStruct((depth, 16, N_MOD * D), F32),
        grid=(depth, N_MOD * D // tn),
        in_specs=[pl.BlockSpec((16, D), lambda l, j: (0, 0)),
                  pl.BlockSpec((1, D, tn), lambda l, j: (l, 0, j)),
                  pl.BlockSpec((1, 1, tn), lambda l, j: (l, 0, j))],
        out_specs=pl.BlockSpec((1, 16, tn), lambda l, j: (l, 0, j)),
        compiler_params=_cp(("parallel", "parallel")),
        name="ada",
    )(cond16, ada_w, ada_b.reshape(depth, 1, N_MOD * D))


def _modulate(x, g, shift, scale):
    y = x * lax.rsqrt(jnp.mean(x * x, axis=-1, keepdims=True) + EPS)
    return (y * g) * (1 + scale) + shift


def _norm_mod_kernel(x_ref, g_ref, m_ref, o_ref, *, which):
    h = _modulate(x_ref[...], g_ref[...], m_ref[0, which:which + 1, :], m_ref[0, which + 1:which + 2, :])
    o_ref[...] = h.astype(o_ref.dtype)


def _norm_mod_t_kernel(x_ref, g_ref, m_ref, o_ref, ot_ref, *, which):
    h = _modulate(x_ref[...], g_ref[...], m_ref[0, which:which + 1, :], m_ref[0, which + 1:which + 2, :])
    o_ref[...] = h.astype(o_ref.dtype)
    ot_ref[...] = h.T.astype(ot_ref.dtype)


def norm_mod(x, g, mods, tpc, which, transposed=False, tm=512):
    m = x.shape[0]
    in_specs = [pl.BlockSpec((tm, D), lambda i: (i, 0)),
                pl.BlockSpec((1, D), lambda i: (0, 0)),
                pl.BlockSpec((1, N_MOD, D), lambda i: ((i * tm) // tpc, 0, 0))]
    if not transposed:
        return pl.pallas_call(
            functools.partial(_norm_mod_kernel, which=which),
            out_shape=jax.ShapeDtypeStruct((m, D), BF16),
            grid=(m // tm,), in_specs=in_specs,
            out_specs=pl.BlockSpec((tm, D), lambda i: (i, 0)),
            compiler_params=_cp(("parallel",)), name="norm_mod",
        )(x, g.reshape(1, D), mods)
    return pl.pallas_call(
        functools.partial(_norm_mod_t_kernel, which=which),
        out_shape=(jax.ShapeDtypeStruct((m, D), BF16), jax.ShapeDtypeStruct((D, m), BF16)),
        grid=(m // tm,), in_specs=in_specs,
        out_specs=(pl.BlockSpec((tm, D), lambda i: (i, 0)), pl.BlockSpec((D, tm), lambda i: (0, i))),
        compiler_params=_cp(("parallel",)), name="norm_mod_t",
    )(x, g.reshape(1, D), mods)


def _final_norm_kernel(x_ref, g_ref, o_ref):
    x = x_ref[...]
    o_ref[...] = x * lax.rsqrt(jnp.mean(x * x, axis=-1, keepdims=True) + EPS) * g_ref[...]


def rms_final(x, g, tm=512):
    m = x.shape[0]
    return pl.pallas_call(
        _final_norm_kernel,
        out_shape=jax.ShapeDtypeStruct((m, D), F32),
        grid=(m // tm,),
        in_specs=[pl.BlockSpec((tm, D), lambda i: (i, 0)), pl.BlockSpec((1, D), lambda i: (0, 0))],
        out_specs=pl.BlockSpec((tm, D), lambda i: (i, 0)),
        compiler_params=_cp(("parallel",)), name="final_norm",
    )(x, g.reshape(1, D))


def _mm_kernel(a_ref, w_ref, o_ref):
    o_ref[...] = _dot(a_ref[...], w_ref[...]).astype(o_ref.dtype)


def _mm_res_kernel(a_ref, w_ref, res_ref, m_ref, o_ref, *, gate):
    o_ref[...] = res_ref[...] + m_ref[0, gate:gate + 1, :] * _dot(a_ref[...], w_ref[...])


def matmul(a, w, out_dtype=F32, tm=512, tn=1024):
    m, n = a.shape[0], w.shape[1]
    tn = min(tn, n)
    return pl.pallas_call(
        _mm_kernel,
        out_shape=jax.ShapeDtypeStruct((m, n), out_dtype),
        grid=(m // tm, n // tn),
        in_specs=[pl.BlockSpec((tm, D), lambda i, j: (i, 0)), pl.BlockSpec((D, tn), lambda i, j: (0, j))],
        out_specs=pl.BlockSpec((tm, tn), lambda i, j: (i, j)),
        compiler_params=_cp(("parallel", "parallel")), name="matmul",
    )(a, w)


def matmul_res(a, w, res, mods, tpc, gate, tm=512):
    m = a.shape[0]
    return pl.pallas_call(
        functools.partial(_mm_res_kernel, gate=gate),
        out_shape=jax.ShapeDtypeStruct((m, D), F32),
        grid=(m // tm,),
        in_specs=[pl.BlockSpec((tm, D), lambda i: (i, 0)),
                  pl.BlockSpec((D, D), lambda i: (0, 0)),
                  pl.BlockSpec((tm, D), lambda i: (i, 0)),
                  pl.BlockSpec((1, N_MOD, D), lambda i: ((i * tm) // tpc, 0, 0))],
        out_specs=pl.BlockSpec((tm, D), lambda i: (i, 0)),
        compiler_params=_cp(("parallel",)), name="matmul_res",
    )(a, w, res, mods)


def _dft_mats(t):
    def cs(n):
        k = np.arange(n)
        ang = 2.0 * np.pi * ((k[:, None] * k[None, :]) % n) / n
        s = 1.0 / math.sqrt(n)
        return np.cos(ang) * s, np.sin(ang) * s
    cc, sc = cs(FNET_GD)
    ct, st = cs(t)
    return (jnp.asarray(np.concatenate([cc, sc], axis=1), BF16), jnp.asarray(ct, BF16), jnp.asarray(st, BF16))


def _dft_kernel(u_ref, cs_ref, ct_ref, st_ref, o_ref):
    p = _dot(u_ref[...].astype(BF16), cs_ref[...])
    pc = p[:, :FNET_GD].astype(BF16)
    ps = p[:, FNET_GD:].astype(BF16)
    o_ref[...] = (_dot(ct_ref[...], pc) - _dot(st_ref[...], ps)).astype(o_ref.dtype)


def fnet_dft(u, t):
    m = u.shape[0]
    cs, ct, st = _dft_mats(t)
    return pl.pallas_call(
        _dft_kernel,
        out_shape=jax.ShapeDtypeStruct((m, D), BF16),
        grid=(m // t, FNET_GROUPS),
        in_specs=[pl.BlockSpec((t, FNET_GD), lambda s, g: (s, g)),
                  pl.BlockSpec((FNET_GD, 2 * FNET_GD), lambda s, g: (0, 0)),
                  pl.BlockSpec((t, t), lambda s, g: (0, 0)),
                  pl.BlockSpec((t, t), lambda s, g: (0, 0))],
        out_specs=pl.BlockSpec((t, FNET_GD), lambda s, g: (s, g)),
        compiler_params=_cp(("parallel", "parallel")), name="fnet_dft",
    )(u, cs, ct, st)


def _head_ones():
    i = np.arange(D) // HEAD
    return jnp.asarray(i[:, None] == i[None, :], BF16)


def _rwkv_proj_kernel(x_ref, xp_ref, xn_ref, g_ref, m_ref, mu_ref, wr_ref, wk_ref, wv_ref, g1_ref, g2_ref,
                      w0_ref, w1_ref, w2_ref, a0_ref, a1_ref, a2_ref, kk_ref, ka_ref, rk_ref, ones_ref,
                      r_out, v_out, kkn_out, g_out, bonus_out, lw_out, kd_out, bd_out, *, tm, t):
    i = pl.program_id(0)
    shift = m_ref[0, 0:1, :]
    scale = m_ref[0, 1:2, :]
    g = g_ref[...]
    h = _modulate(x_ref[...], g, shift, scale)
    first = (i * tm) % t == 0
    last = ((i + 1) * tm) % t == 0
    hp = jnp.where(first, 0.0, _modulate(xp_ref[7:8, :], g, shift, scale))
    hn = jnp.where(last, 0.0, _modulate(xn_ref[0:1, :], g, shift, scale))
    row = lax.broadcasted_iota(jnp.int32, (tm, 1), 0)
    prev = jnp.where(row == 0, hp, pltpu.roll(h, 1, axis=0))
    nxt = jnp.where(row == tm - 1, hn, pltpu.roll(h, tm - 1, axis=0))
    xx = 0.5 * (prev + nxt) - h

    def mix(j):
        return (h + xx * mu_ref[j:j + 1, :]).astype(BF16)

    r = _dot(mix(0), wr_ref[...])
    k = _dot(mix(2), wk_ref[...])
    v = _dot(mix(3), wv_ref[...])
    gate = _dot(jax.nn.sigmoid(_dot(mix(5), g1_ref[...])).astype(BF16), g2_ref[...])
    xw = mix(1)
    xa = mix(4)
    ones = ones_ref[...]
    kk = k * kk_ref[...]
    kk = kk * lax.rsqrt(_split_dot(kk * kk, ones) + 1e-12)
    ksum = jnp.zeros_like(k)
    for j in range(2):
        w_raw = w0_ref[j:j + 1, :] + _dot(jnp.tanh(_dot(xw, w1_ref[j])).astype(BF16), w2_ref[j])
        lw_out[j] = -jnp.exp(-jax.nn.softplus(-w_raw) - 0.5)
        a = jax.nn.sigmoid(a0_ref[j:j + 1, :] + _dot(_dot(xa, a1_ref[j]).astype(BF16), a2_ref[j]))
        kd = k * (1 + (a - 1) * ka_ref[...])
        kd_out[j] = kd
        bd_out[j] = kk * a
        ksum = ksum + kd
    r_out[...] = r
    v_out[...] = v
    kkn_out[...] = kk
    g_out[...] = gate
    bonus_out[...] = _split_dot(r * ksum * rk_ref[...], ones) * v


def rwkv_proj(x, norm_g, mods, tpc, t, p, tm=256):
    m = x.shape[0]
    nb8 = m // 8
    full = lambda *shape: pl.BlockSpec(shape, lambda i: (0,) * len(shape))
    tok = pl.BlockSpec((tm, D), lambda i: (i, 0))
    tok2 = pl.BlockSpec((2, tm, D), lambda i: (0, i, 0))
    in_specs = [tok,
                pl.BlockSpec((8, D), lambda i: (jnp.maximum(i * (tm // 8) - 1, 0), 0)),
                pl.BlockSpec((8, D), lambda i: (jnp.minimum((i + 1) * (tm // 8), nb8 - 1), 0)),
                full(1, D),
                pl.BlockSpec((1, N_MOD, D), lambda i: ((i * tm) // tpc, 0, 0)),
                full(6, D), full(D, D), full(D, D), full(D, D), full(D, 128), full(128, D),
                full(2, D), full(2, D, 64), full(2, 64, D), full(2, D), full(2, D, 64), full(2, 64, D),
                full(1, D), full(1, D), full(1, D), full(D, D)]
    sd = jax.ShapeDtypeStruct
    return pl.pallas_call(
        functools.partial(_rwkv_proj_kernel, tm=tm, t=t),
        out_shape=(sd((m, D), F32),) * 5 + (sd((2, m, D), F32),) * 3,
        grid=(m // tm,), in_specs=in_specs,
        out_specs=(tok,) * 5 + (tok2,) * 3,
        compiler_params=_cp(("parallel",)), name="rwkv_proj",
    )(x, x, x, norm_g.reshape(1, D), mods, p["mu"], p["w_r"], p["w_k"], p["w_v"], p["g1"], p["g2"],
      p["w0"], p["w1"], p["w2"], p["a0"], p["a1"], p["a2"], p["k_k"], p["k_a"], p["r_k"], _head_ones())


def _rwkv_scan_kernel(r_ref, v_ref, kk_ref, lw_ref, kd_ref, b_ref, z0_ref, y_ref, zout_ref, z_scr, *, n_chunks):
    L = SCAN_L
    d = pl.program_id(1)
    c = pl.program_id(2)

    @pl.when(c == 0)
    def _():
        z_scr[...] = z0_ref[0, 0]

    row = lax.broadcasted_iota(jnp.int32, (L, L), 0)
    col = lax.broadcasted_iota(jnp.int32, (L, L), 1)
    fwd = d == 0
    order = (col - row) * (1 - 2 * d)
    before = order < 0
    upto = order <= 0
    cum_mat = upto.astype(BF16)
    same_head = (row // HEAD) == (col // HEAD)
    lane = lax.broadcasted_iota(jnp.int32, (1, 2 * HEAD), 1)
    head_mask = (lane < HEAD, lane >= HEAD)
    n_double = int(math.log2(L))

    def prepare(p):
        sl = slice(p * 2 * HEAD, (p + 1) * 2 * HEAD)
        lw = lw_ref[0, :, sl]
        cum = _split_dot_left(cum_mat, lw)
        tot = jnp.where(fwd, cum[L - 1:L, :], cum[0:1, :])
        inv = jnp.exp(-cum)
        ar = jnp.concatenate([-kk_ref[:, sl] * jnp.exp(cum - lw), r_ref[:, sl] * jnp.exp(cum)], axis=0)
        bk = jnp.concatenate([b_ref[0, :, sl] * inv, kd_ref[0, :, sl] * inv], axis=0).astype(BF16)
        z = z_scr[p]
        base = _dot_nt(ar.astype(BF16), z.astype(BF16))
        return dict(ar=ar, bk=bk, z=z, base=base, v=v_ref[:, sl], tot=tot)

    def start_chain(pp, hm):
        g4 = _dot_nt(jnp.where(hm, pp["ar"], 0.0).astype(BF16), pp["bk"])
        vm = jnp.where(hm, pp["v"], 0.0).astype(BF16)
        n = jnp.where(before, g4[:L, :L], 0.0)
        x = jnp.where(hm, pp["base"][:L], 0.0) + _dot(jnp.where(before, g4[:L, L:], 0.0).astype(BF16), vm)
        out = jnp.concatenate([jnp.where(upto, g4[L:, :L], 0.0), jnp.where(upto, g4[L:, L:], 0.0)], axis=1)
        return dict(n=n, x=x, out=out.astype(BF16), vm=vm)

    pairs = [prepare(p) for p in range(N_HEADS // 2)]
    chains = [[start_chain(pp, hm) for hm in head_mask] for pp in pairs]
    for it in range(n_double):
        for ch in (ch for pair in chains for ch in pair):
            nb = ch["n"].astype(BF16)
            xb = ch["x"].astype(BF16)
            if it + 1 < n_double:
                res = _dot(nb, jnp.concatenate([xb, nb], axis=1))
                ch["x"] = ch["x"] + res[:, :2 * HEAD]
                ch["n"] = res[:, 2 * HEAD:]
            else:
                ch["x"] = ch["x"] + _dot(nb, xb)
    for p, (pp, pair) in enumerate(zip(pairs, chains)):
        y = pp["base"][L:]
        for ch in pair:
            y = y + _dot(ch["out"], jnp.concatenate([ch["x"].astype(BF16), ch["vm"]], axis=0))
        uv = jnp.concatenate([pair[0]["x"] + pair[1]["x"], pp["v"]], axis=0)
        inc = _dot(uv.T.astype(BF16), pp["bk"])
        z_scr[p] = jnp.where(same_head, pp["z"] + inc, 0.0) * jnp.exp(pp["tot"])
        y_ref[0, :, p * 2 * HEAD:(p + 1) * 2 * HEAD] = y

    @pl.when(c == n_chunks - 1)
    def _():
        zout_ref[0, 0] = z_scr[...]


def _split_dot_left(w, x):
    hi = x.astype(BF16)
    lo = (x - hi.astype(F32)).astype(BF16)
    return _dot(w, hi) + _dot(w, lo)


def rwkv_scan(r, v, kk, lw, kd, bd, z0, t):
    m = r.shape[0]
    n_seq = m // t
    nc = t // SCAN_L

    def blk(s, d, c):
        return s * nc + c + d * (nc - 1 - 2 * c)

    tok = pl.BlockSpec((SCAN_L, D), lambda s, d, c: (blk(s, d, c), 0))
    tok2 = pl.BlockSpec((1, SCAN_L, D), lambda s, d, c: (d, blk(s, d, c), 0))
    zspec = pl.BlockSpec((1, 1, N_HEADS // 2, 2 * HEAD, 2 * HEAD), lambda s, d, c: (s, d, 0, 0, 0))
    return pl.pallas_call(
        functools.partial(_rwkv_scan_kernel, n_chunks=nc),
        out_shape=(jax.ShapeDtypeStruct((2, m, D), F32), jax.ShapeDtypeStruct(z0.shape, F32)),
        grid=(n_seq, 2, nc),
        in_specs=[tok, tok, tok, tok2, tok2, tok2, zspec],
        out_specs=(tok2, zspec),
        scratch_shapes=[pltpu.VMEM((N_HEADS // 2, 2 * HEAD, 2 * HEAD), F32)],
        compiler_params=_cp(("parallel", "parallel", "arbitrary")), name="rwkv_scan",
    )(r, v, kk, lw, kd, bd, z0)


def _rwkv_post_kernel(y_ref, bonus_ref, g_ref, lg_ref, lb_ref, ones_ref, o_ref):
    ones = ones_ref[...]
    o = y_ref[0] + y_ref[1]
    cen = o - _split_dot(o, ones) * (1.0 / HEAD)
    var = _split_dot(cen * cen, ones) * (1.0 / HEAD)
    o = cen * lax.rsqrt(var + LNX_EPS) * lg_ref[...] + lb_ref[...] + bonus_ref[...]
    o_ref[...] = (o * g_ref[...]).astype(o_ref.dtype)


def rwkv_post(y, bonus, g, lnx_g, lnx_b, tm=256):
    m = bonus.shape[0]
    tok = pl.BlockSpec((tm, D), lambda i: (i, 0))
    row = pl.BlockSpec((1, D), lambda i: (0, 0))
    return pl.pallas_call(
        _rwkv_post_kernel,
        out_shape=jax.ShapeDtypeStruct((m, D), BF16),
        grid=(m // tm,),
        in_specs=[pl.BlockSpec((2, tm, D), lambda i: (0, i, 0)), tok, tok, row, row,
                  pl.BlockSpec((D, D), lambda i: (0, 0))],
        out_specs=tok,
        compiler_params=_cp(("parallel",)), name="rwkv_post",
    )(y, bonus, g, lnx_g.reshape(1, D), lnx_b.reshape(1, D), _head_ones())


def _pair_states(s):
    n = s.shape[0]
    s = s.reshape(n, 2, N_HEADS // 2, 2, HEAD, HEAD)
    z = jnp.zeros((n, 2, N_HEADS // 2, 2, HEAD, 2, HEAD), F32)
    z = z.at[:, :, :, 0, :, 0, :].set(s[:, :, :, 0]).at[:, :, :, 1, :, 1, :].set(s[:, :, :, 1])
    return z.reshape(n, 2, N_HEADS // 2, 2 * HEAD, 2 * HEAD)


def _unpair_states(z):
    n = z.shape[0]
    z = z.reshape(n, 2, N_HEADS // 2, 2, HEAD, 2, HEAD)
    s = jnp.stack([z[:, :, :, 0, :, 0, :], z[:, :, :, 1, :, 1, :]], axis=3)
    return s.reshape(n, 2, N_HEADS, HEAD, HEAD)


def rwkv_layer(x, norm_g, mods, tpc, t, s0, p):
    r, v, kk, g, bonus, lw, kd, bd = rwkv_proj(x, norm_g, mods, tpc, t, p)
    y, zf = rwkv_scan(r, v, kk, lw, kd, bd, _pair_states(s0), t)
    o = rwkv_post(y, bonus, g, p["lnx_g"], p["lnx_b"])
    return matmul_res(o, p["w_o"], x, mods, tpc, 2), _unpair_states(zf)


def _softmax_rows(parts):
    m = parts[0].max(axis=-1, keepdims=True)
    for s in parts[1:]:
        m = jnp.maximum(m, s.max(axis=-1, keepdims=True))
    es = [jnp.exp(s - m) for s in parts]
    den = es[0].sum(axis=-1, keepdims=True)
    for e in es[1:]:
        den = den + e.sum(axis=-1, keepdims=True)
    inv = 1.0 / den
    return [(e * inv).astype(BF16) for e in es]


def _qkv_kernel(a_ref, w_ref, *out_refs, with_f32):
    j = pl.program_id(1)
    r = _dot(a_ref[...], w_ref[...])
    for idx in range(3):
        @pl.when(j == idx)
        def _(idx=idx):
            out_refs[idx][...] = r.astype(BF16)
            if with_f32 and idx > 0:
                out_refs[2 + idx][...] = r


def qkv_proj(a, w, with_f32, tm=512):
    m = a.shape[0]
    tok = pl.BlockSpec((tm, D), lambda i, j: (i, 0))
    sd = jax.ShapeDtypeStruct
    n_f32 = 2 if with_f32 else 0
    return pl.pallas_call(
        functools.partial(_qkv_kernel, with_f32=with_f32),
        out_shape=(sd((m, D), BF16),) * 3 + (sd((m, D), F32),) * n_f32,
        grid=(m // tm, 3),
        in_specs=[tok, pl.BlockSpec((D, D), lambda i, j: (0, j))],
        out_specs=(tok,) * (3 + n_f32),
        compiler_params=_cp(("parallel", "arbitrary")), name="qkv_proj",
    )(a, w)


def _pair_masks():
    lane = lax.broadcasted_iota(jnp.int32, (1, 2 * HEAD), 1)
    return lane < HEAD, lane >= HEAD


def _na_ctx_kernel(q_ref, k_ref, v_ref, o_ref):
    masks = _pair_masks()
    zero = jnp.zeros((), BF16)
    scores = []
    for p in range(N_HEADS // 2):
        sl = slice(p * 2 * HEAD, (p + 1) * 2 * HEAD)
        q = q_ref[:, sl]
        k = k_ref[:, sl]
        scores.append([_dot_nt(jnp.where(hm, q, zero), k) * NA_SCALE for hm in masks])
    probs = [[_softmax_rows([s])[0] for s in pair] for pair in scores]
    for p in range(N_HEADS // 2):
        sl = slice(p * 2 * HEAD, (p + 1) * 2 * HEAD)
        v = v_ref[:, sl]
        o_ref[:, sl] = jnp.where(masks[0], _dot(probs[p][0], v), _dot(probs[p][1], v)).astype(o_ref.dtype)


def na_ctx_attn(q, k, v, t):
    m = q.shape[0]
    seq = pl.BlockSpec((t, D), lambda b: (b, 0))
    return pl.pallas_call(
        _na_ctx_kernel,
        out_shape=jax.ShapeDtypeStruct((m, D), BF16),
        grid=(m // t,),
        in_specs=[seq, seq, seq],
        out_specs=seq,
        compiler_params=_cp(("parallel",)), name="na_ctx_attn",
    )(q, k, v)


def _win_start(r, rows):
    return jnp.clip(r - WIN_ROWS // 2, 0, rows - WIN_ROWS)


def _na_lat_kernel(q_ref, k_ref, v_ref, ck_ref, cv_ref, bias_ref, o_ref, *, rows):
    r = pl.program_id(1)
    start = pl.multiple_of(_win_start(r, rows) * GRID_W, GRID_W)
    n_loc = WIN_ROWS * GRID_W
    qc = lax.broadcasted_iota(jnp.int32, (GRID_W, n_loc), 0)
    kc = lax.broadcasted_iota(jnp.int32, (GRID_W, n_loc), 1) % GRID_W
    cs = jnp.clip(qc - WIN_COLS // 2, 0, GRID_W - WIN_COLS)
    valid = (kc >= cs) & (kc < cs + WIN_COLS)
    masks = _pair_masks()
    zero = jnp.zeros((), BF16)
    scores = []
    for p in range(N_HEADS // 2):
        sl = slice(p * 2 * HEAD, (p + 1) * 2 * HEAD)
        q = q_ref[:, sl]
        kw = k_ref[pl.ds(start, n_loc), sl]
        ck = ck_ref[:, sl]
        pair = []
        for i, hm in enumerate(masks):
            qm = jnp.where(hm, q, zero)
            s_loc = jnp.where(valid, _dot_nt(qm, kw) * NA_SCALE + bias_ref[0, 2 * p + i], NEG_INF)
            pair.append([s_loc, _dot_nt(qm, ck) * NA_SCALE])
        scores.append(pair)
    probs = [[_softmax_rows(parts) for parts in pair] for pair in scores]
    for p in range(N_HEADS // 2):
        sl = slice(p * 2 * HEAD, (p + 1) * 2 * HEAD)
        vw = v_ref[pl.ds(start, n_loc), sl]
        cv = cv_ref[:, sl]
        outs = [_dot(p_loc, vw) + _dot(p_ctx, cv) for p_loc, p_ctx in probs[p]]
        o_ref[:, sl] = jnp.where(masks[0], outs[0], outs[1]).astype(o_ref.dtype)


def _na_bias(rpb, rows):
    wr = min(WIN_ROWS, rows)
    qc = np.arange(GRID_W)
    col_off = np.clip(qc[None, :] - qc[:, None], -(WIN_COLS - 1), WIN_COLS - 1) + WIN_COLS - 1
    onehot = jnp.asarray(np.arange(2 * WIN_COLS - 1)[:, None, None] == col_off[None], F32)
    b = jnp.einsum('hrc,cqk->hrqk', rpb.astype(F32), onehot, precision=lax.Precision.HIGHEST)
    b = jnp.stack([b[:, ro0:ro0 + wr] for ro0 in range(WIN_ROWS)], axis=0)
    return jnp.transpose(b, (0, 1, 3, 2, 4)).reshape(WIN_ROWS, N_HEADS, GRID_W, wr * GRID_W)


def na_lat_attn(q, k, v, ck, cv, rpb, t, past):
    m = q.shape[0]
    rows = t // GRID_W
    assert rows >= WIN_ROWS
    n_seq = m // t
    n_loc = WIN_ROWS * GRID_W

    def bias_idx(b, r):
        return (_win_start(r, rows) - r + WIN_ROWS - 1, 0, 0, 0)

    return pl.pallas_call(
        functools.partial(_na_lat_kernel, rows=rows),
        out_shape=jax.ShapeDtypeStruct((m, D), BF16),
        grid=(n_seq, rows),
        in_specs=[pl.BlockSpec((GRID_W, D), lambda b, r: (b * rows + r, 0)),
                  pl.BlockSpec((t, D), lambda b, r: (b, 0)),
                  pl.BlockSpec((t, D), lambda b, r: (b, 0)),
                  pl.BlockSpec((past, D), lambda b, r: (b, 0)),
                  pl.BlockSpec((past, D), lambda b, r: (b, 0)),
                  pl.BlockSpec((1, N_HEADS, GRID_W, n_loc), bias_idx)],
        out_specs=pl.BlockSpec((GRID_W, D), lambda b, r: (b * rows + r, 0)),
        compiler_params=_cp(("parallel", "arbitrary")), name="na_lat_attn",
    )(q, k, v, ck, cv, _na_bias(rpb, rows))


def _sort16_net():
    def merge(lo, hi, r):
        step = r * 2
        if step < hi - lo:
            yield from merge(lo, hi, step)
            yield from merge(lo + r, hi, step)
            yield from ((i, i + r) for i in range(lo + r, hi - r, step))
        else:
            yield (lo, lo + r)

    def sort(lo, hi):
        if hi - lo >= 1:
            mid = lo + (hi - lo) // 2
            yield from sort(lo, mid)
            yield from sort(mid + 1, hi)
            yield from merge(lo, hi, 1)

    return tuple(sort(0, PEER_TOPK - 1))


_SORT16 = _sort16_net()
_BITONIC16 = tuple((i, i + d) for d in (8, 4, 2, 1) for i in range(PEER_TOPK) if not i & d)


def _exchange(x, net):
    x = list(x)
    for i, j in net:
        x[i], x[j] = jnp.maximum(x[i], x[j]), jnp.minimum(x[i], x[j])
    return x


def _merge_top16(a, b):
    return _exchange([jnp.maximum(a[i], b[PEER_TOPK - 1 - i]) for i in range(PEER_TOPK)], _BITONIC16)


def _merge_sublanes(x):
    for shift in (4, 2, 1):
        x = _merge_top16(x, [pltpu.roll(v, shift, axis=0) for v in x])
    return x


def _peer_stats_kernel(q_ref, keys_ref, cut_out, e1_out, tab_out):
    tm = q_ref.shape[0]
    sub = lax.broadcasted_iota(jnp.int32, (8, tm), 0)
    ninf = jnp.full((8, tm), -jnp.inf, F32)
    for h in range(PEER_HEADS):
        s, tops = [], []
        for c in range(2):
            hc = 2 * h + c
            q = q_ref[:, hc * PEER_KEYS:(hc + 1) * PEER_KEYS].astype(BF16)
            sc = _dot_nt(keys_ref[hc], q)
            s.append(sc)
            groups = [sc[8 * i:8 * i + 8, :] for i in range(PEER_KEYS // 8)]
            tops.append(_merge_sublanes(_exchange(groups, _SORT16)))
        a1, a2 = tops
        a1col = a1[7]
        for jj in range(6, -1, -1):
            a1col = jnp.where(sub == jj, a1[jj], a1col)
        lists = [jnp.where(sub < min(8, PEER_TOPK // (k + 1)), a1col + a2[k], ninf) for k in range(PEER_TOPK)]
        tail = [a1[8 + k] + a2[0] for k in range(8)] + [ninf] * 8
        best = _merge_top16(_merge_sublanes(lists), tail)
        z = jnp.zeros((8, tm), F32)
        for b in best:
            z = z + jnp.exp(b - best[0])
        thr = best[PEER_TOPK - 1][0:1, :]
        cut = jnp.full((PEER_KEYS, tm), jnp.inf, F32)
        for k in range(PEER_TOPK):
            a2k = a2[k][0:1, :]
            cut = jnp.where(s[0] + a2k >= thr, a2k, cut)
        cut_out[h] = cut
        e1_out[h] = jnp.exp(s[0] - a1[0][0:1, :]) / z[0:1, :]
        e2 = jnp.exp(s[1] - a2[0][0:1, :])
        for lb in range(tm // 128):
            lanes = slice(lb * 128, (lb + 1) * 128)
            tab_out[lb, h, :, 0] = s[1][:, lanes].reshape(PEER_KEYS // 8, 8, 128)
            tab_out[lb, h, :, 1] = e2[:, lanes].reshape(PEER_KEYS // 8, 8, 128)


def peer_stats(q, keys, tm=256):
    m = q.shape[0]
    nq = 2 * PEER_HEADS * PEER_KEYS
    sd = jax.ShapeDtypeStruct
    big = pl.BlockSpec((PEER_HEADS, PEER_KEYS, tm), lambda i: (0, 0, i))
    tab_shape = (PEER_HEADS, PEER_KEYS // 8, 2, 8, 128)
    return pl.pallas_call(
        _peer_stats_kernel,
        out_shape=(sd((PEER_HEADS, PEER_KEYS, m), F32),) * 2 + (sd((m // 128,) + tab_shape, F32),),
        grid=(m // tm,),
        in_specs=[pl.BlockSpec((tm, nq), lambda i: (i, 0)),
                  pl.BlockSpec((2 * PEER_HEADS, PEER_KEYS, PEER_KEYS), lambda i: (0, 0, 0))],
        out_specs=(big,) * 2 + (pl.BlockSpec((tm // 128,) + tab_shape, lambda i: (i, 0, 0, 0, 0, 0)),),
        compiler_params=_cp(("parallel",)), name="peer_stats",
    )(q, keys)


def _gelu(x):
    return 0.5 * x * (1.0 + lax.erf(x * (1.0 / math.sqrt(2.0))))


def _peer_dense_kernel(xt_ref, u_ref, un_ref, vt_ref, vp_ref, cut_ref, e1_ref, tab_ref, res_ref, m_ref, o_ref,
                       acc_scr, a_scr, w_scr, *, tm, tn, sub, gate):
    j = pl.program_id(1)
    n_sub = tn // sub
    n_cb = tm // 128
    out_rows = D // n_cb

    def act_matmul(sb):
        return _dot(u_ref[sb * sub:(sb + 1) * sub, :], xt_ref[...])

    @pl.when(j == 0)
    def _():
        acc_scr[...] = jnp.zeros_like(acc_scr)
        w_scr[...] = jnp.zeros_like(w_scr)
        a_scr[...] = act_matmul(0)

    def weights(sb, cb, a):
        lanes = slice(cb * 128, (cb + 1) * 128)
        parts = []
        for il in range(sub // PEER_KEYS):
            i1 = sb * (sub // PEER_KEYS) + il
            for rg in range(PEER_KEYS // PEER_ROWS):
                grp = slice(rg * PEER_ROWS // 8, (rg + 1) * PEER_ROWS // 8)
                g = jnp.zeros((PEER_ROWS, 128), F32)
                for h in range(PEER_HEADS):
                    s2 = tab_ref[cb, h, grp, 0].reshape(PEER_ROWS, 128)
                    e2 = tab_ref[cb, h, grp, 1].reshape(PEER_ROWS, 128)
                    hit = s2 >= cut_ref[h, i1:i1 + 1, lanes]
                    g = g + jnp.where(hit, e2 * e1_ref[h, i1:i1 + 1, lanes], 0.0)
                r0 = il * PEER_KEYS + rg * PEER_ROWS
                parts.append((g * _gelu(a[r0:r0 + PEER_ROWS, lanes])).astype(BF16))
        return jnp.concatenate(parts, axis=0)

    def out_piece(sb, q, w):
        rows = slice(q * out_rows, (q + 1) * out_rows)
        if sb < 0:
            return _dot(vp_ref[rows, :], w)
        return _dot(vt_ref[rows, sb * sub:(sb + 1) * sub], w)

    a = a_scr[...]
    w_prev = w_scr[...]
    contrib = [None] * n_cb
    for sb in range(n_sub):
        cols = []
        for cb in range(n_cb):
            if cb == 0:
                a_next = act_matmul(sb + 1) if sb + 1 < n_sub else _dot(un_ref[...], xt_ref[...])
            piece = out_piece(sb - 1, cb, w_prev)
            contrib[cb] = piece if contrib[cb] is None else contrib[cb] + piece
            cols.append(weights(sb, cb, a))
        w_prev = jnp.concatenate(cols, axis=1)
        a = a_next
    for q in range(n_cb):
        acc_scr[q * out_rows:(q + 1) * out_rows, :] += contrib[q]
    a_scr[...] = a
    w_scr[...] = w_prev

    @pl.when(j == pl.num_programs(1) - 1)
    def _():
        for q in range(n_cb):
            acc_scr[q * out_rows:(q + 1) * out_rows, :] += out_piece(n_sub - 1, q, w_scr[...])
        o_ref[...] = res_ref[...] + m_ref[0, gate:gate + 1, :] * acc_scr[...].T


def peer_dense(xt, u, vt, cut, e1, tab, res, mods, tpc, gate, tm=512, tn=1024, sub=256):
    m = xt.shape[1]
    n1 = tn // PEER_KEYS
    part = pl.BlockSpec((PEER_HEADS, n1, tm), lambda i, j: (0, j, i))
    full = pl.BlockSpec((tm // 128,) + tab.shape[1:], lambda i, j: (i, 0, 0, 0, 0, 0))
    n_sub = tn // sub
    last_sub = N_EXPERTS // sub - 1
    return pl.pallas_call(
        functools.partial(_peer_dense_kernel, tm=tm, tn=tn, sub=sub, gate=gate),
        out_shape=jax.ShapeDtypeStruct((m, D), F32),
        grid=(m // tm, N_EXPERTS // tn),
        in_specs=[pl.BlockSpec((D, tm), lambda i, j: (0, i)),
                  pl.BlockSpec((tn, D), lambda i, j: (j, 0)),
                  pl.BlockSpec((sub, D), lambda i, j: (jnp.minimum((j + 1) * n_sub, last_sub), 0)),
                  pl.BlockSpec((D, tn), lambda i, j: (0, j)),
                  pl.BlockSpec((D, sub), lambda i, j: (0, jnp.maximum(j * n_sub - 1, 0))),
                  part, part, full,
                  pl.BlockSpec((tm, D), lambda i, j: (i, 0)),
                  pl.BlockSpec((1, N_MOD, D), lambda i, j: ((i * tm) // tpc, 0, 0))],
        out_specs=pl.BlockSpec((tm, D), lambda i, j: (i, 0)),
        scratch_shapes=[pltpu.VMEM((D, tm), F32), pltpu.VMEM((sub, tm), F32), pltpu.VMEM((sub, tm), BF16)],
        compiler_params=_cp(("parallel", "arbitrary")), name="peer_dense",
    )(xt, u, u, vt, vt, cut, e1, tab, res, mods)


def peer_layer(x, norm_g, mods, tpc, wq, keys, u, vt):
    h, ht = norm_mod(x, norm_g, mods, tpc, 3, transposed=True)
    q = matmul(h, wq)
    cut, e1, tab = peer_stats(q, keys)
    return peer_dense(ht, u, vt, cut, e1, tab, x, mods, tpc, 5)


def kernel(x_prompt, x_sample, c, state_wkv, cache_k, cache_v, c_ctx, ada_w, ada_b, norm_mix, norm_ffn, fnet_w_in, fnet_w_out, rwkv_mu, rwkv_w_r, rwkv_w_k, rwkv_w_v, rwkv_w_o, rwkv_w0, rwkv_w1, rwkv_w2, rwkv_a0, rwkv_a1, rwkv_a2, rwkv_g1, rwkv_g2, rwkv_k_k, rwkv_k_a, rwkv_r_k, rwkv_lnx_g, rwkv_lnx_b, na_w_qkv, na_w_o, na_rpb, peer_w_q, peer_sub_keys, peer_u, peer_v, final_norm):
    nb_c, t_c, _ = x_prompt.shape
    nb_s, t_s, _ = x_sample.shape
    depth = ada_w.shape[0]
    past = cache_k.shape[2]
    bf = lambda w: w.astype(BF16)

    cond = jnp.concatenate([c_ctx[None, :], c, jnp.zeros((16 - 1 - nb_s, D), F32)], axis=0)
    mods_all = ada_all(cond, ada_w, ada_b).reshape(depth, 16, N_MOD, D)

    xc = x_prompt.reshape(nb_c * t_c, D)
    xs = x_sample.reshape(nb_s * t_s, D)
    tpc_c, tpc_s = nb_c * t_c, t_s
    new_wkv, new_k, new_v = [], [], []

    for l in range(depth):
        kind, j = l % 3, l // 3
        mc = mods_all[l, 0:1]
        ms = mods_all[l, 1:1 + nb_s]
        if kind == 0:
            w_in, w_out = bf(fnet_w_in[j]), bf(fnet_w_out[j])
            outs = []
            for x, m, tpc, t in ((xc, mc, tpc_c, t_c), (xs, ms, tpc_s, t_s)):
                h = norm_mod(x, norm_mix[l], m, tpc, 0)
                f = fnet_dft(matmul(h, w_in), t)
                outs.append(matmul_res(f, w_out, x, m, tpc, 2))
            xc, xs = outs
        elif kind == 1:
            p = dict(mu=rwkv_mu[j], w_r=bf(rwkv_w_r[j]), w_k=bf(rwkv_w_k[j]), w_v=bf(rwkv_w_v[j]), w_o=bf(rwkv_w_o[j]),
                     w0=rwkv_w0[j], w1=bf(rwkv_w1[j]), w2=bf(rwkv_w2[j]), a0=rwkv_a0[j], a1=bf(rwkv_a1[j]),
                     a2=bf(rwkv_a2[j]), g1=bf(rwkv_g1[j]), g2=bf(rwkv_g2[j]), k_k=rwkv_k_k[j].reshape(1, D),
                     k_a=rwkv_k_a[j].reshape(1, D), r_k=rwkv_r_k[j].reshape(1, D), lnx_g=rwkv_lnx_g[j],
                     lnx_b=rwkv_lnx_b[j])
            s_zero = jnp.zeros((nb_c, 2, N_HEADS, HEAD, HEAD), F32)
            xc, sc = rwkv_layer(xc, norm_mix[l], mc, tpc_c, t_c, s_zero, p)
            xs, _ = rwkv_layer(xs, norm_mix[l], ms, tpc_s, t_s, state_wkv[:, j], p)
            new_wkv.append(sc)
        else:
            w_qkv, w_o = bf(na_w_qkv[j]), bf(na_w_o[j])
            qc, kc, vc, kc32, vc32 = qkv_proj(norm_mod(xc, norm_mix[l], mc, tpc_c, 0), w_qkv, True)
            qs, ks, vs = qkv_proj(norm_mod(xs, norm_mix[l], ms, tpc_s, 0), w_qkv, False)
            new_k.append(kc32.reshape(nb_c, t_c, N_HEADS, HEAD))
            new_v.append(vc32.reshape(nb_c, t_c, N_HEADS, HEAD))
            oc = na_ctx_attn(qc, kc, vc, t_c)
            os_ = na_lat_attn(qs, ks, vs, bf(cache_k[:, j]).reshape(nb_s * past, D),
                              bf(cache_v[:, j]).reshape(nb_s * past, D), na_rpb[j], t_s, past)
            xc = matmul_res(oc, w_o, xc, mc, tpc_c, 2)
            xs = matmul_res(os_, w_o, xs, ms, tpc_s, 2)
        wq = bf(peer_w_q[l])
        keys = bf(peer_sub_keys[l]).reshape(2 * PEER_HEADS, PEER_KEYS, PEER_KEYS)
        u = bf(peer_u[l])
        vt = bf(peer_v[l]).T
        xc = peer_layer(xc, norm_ffn[l], mc, tpc_c, wq, keys, u, vt)
        xs = peer_layer(xs, norm_ffn[l], ms, tpc_s, wq, keys, u, vt)

    y_prompt = rms_final(xc, final_norm).reshape(nb_c, t_c, D)
    y_sample = rms_final(xs, final_norm).reshape(nb_s, t_s, D)
    return (y_prompt, y_sample, jnp.stack(new_wkv, axis=1), jnp.stack(new_k, axis=1), jnp.stack(new_v, axis=1))
```

```python
import functools
import math

import numpy as np
import jax
import jax.numpy as jnp
from jax import lax
from jax.experimental import pallas as pl
from jax.experimental.pallas import tpu as pltpu

F32 = jnp.float32
BF16 = jnp.bfloat16

D = 1024
N_MOD = 6
EPS = 1e-6
HEAD = 64
N_HEADS = D // HEAD
LNX_EPS = 64e-5
GRID_W = 64
WIN_ROWS = 8
WIN_COLS = 16
NA_SCALE = HEAD ** -0.5
NEG_INF = -1e30
FNET_GROUPS = 4
FNET_GD = D // FNET_GROUPS
PEER_KEYS = 128
PEER_HEADS = 8
PEER_TOPK = 16
N_EXPERTS = PEER_KEYS * PEER_KEYS
PEER_ROWS = 32
SCAN_L = 128

VMEM_LIMIT = 56 * 1024 * 1024


def _cp(sem, vmem=VMEM_LIMIT):
    return pltpu.CompilerParams(dimension_semantics=sem, vmem_limit_bytes=vmem)


def _dot(a, b):
    return jnp.dot(a, b, preferred_element_type=F32)


def _dot_nt(a, b):
    return lax.dot_general(a, b, (((1,), (1,)), ((), ())), preferred_element_type=F32)


def _split_dot(x, w):
    hi = x.astype(BF16)
    lo = (x - hi.astype(F32)).astype(BF16)
    return _dot(hi, w) + _dot(lo, w)


def _ada_kernel(c_ref, w_ref, b_ref, o_ref):
    c = c_ref[...]
    s = c * jax.nn.sigmoid(c)
    o_ref[0] = _dot(s.astype(BF16), w_ref[0].astype(BF16)) + b_ref[0]


def ada_all(cond16, ada_w, ada_b):
    depth = ada_w.shape[0]
    tn = 1024
    return pl.pallas_call(
        _ada_kernel,
        out_shape=jax.ShapeDtypeStruct((depth, 16, N_MOD * D), F32),
        grid=(depth, N_MOD * D // tn),
        in_specs=[pl.BlockSpec((16, D), lambda l, j: (0, 0)),
                  pl.BlockSpec((1, D, tn), lambda l, j: (l, 0, j)),
                  pl.BlockSpec((1, 1, tn), lambda l, j: (l, 0, j))],
        out_specs=pl.BlockSpec((1, 16, tn), lambda l, j: (l, 0, j)),
        compiler_params=_cp(("parallel", "parallel")),
        name="ada",
    )(cond16, ada_w, ada_b.reshape(depth, 1, N_MOD * D))


def _modulate(x, g, shift, scale):
    y = x * lax.rsqrt(jnp.mean(x * x, axis=-1, keepdims=True) + EPS)
    return (y * g) * (1 + scale) + shift


def _norm_mod_kernel(x_ref, g_ref, m_ref, o_ref, *, which):
    h = _modulate(x_ref[...], g_ref[...], m_ref[0, which:which + 1, :], m_ref[0, which + 1:which + 2, :])
    o_ref[...] = h.astype(o_ref.dtype)


def _norm_mod_t_kernel(x_ref, g_ref, m_ref, o_ref, ot_ref, *, which):
    h = _modulate(x_ref[...], g_ref[...], m_ref[0, which:which + 1, :], m_ref[0, which + 1:which + 2, :])
    o_ref[...] = h.astype(o_ref.dtype)
    ot_ref[...] = h.T.astype(ot_ref.dtype)


def norm_mod(x, g, mods, tpc, which, transposed=False, tm=512):
    m = x.shape[0]
    in_specs = [pl.BlockSpec((tm, D), lambda i: (i, 0)),
                pl.BlockSpec((1, D), lambda i: (0, 0)),
                pl.BlockSpec((1, N_MOD, D), lambda i: ((i * tm) // tpc, 0, 0))]
    if not transposed:
        return pl.pallas_call(
            functools.partial(_norm_mod_kernel, which=which),
            out_shape=jax.ShapeDtypeStruct((m, D), BF16),
            grid=(m // tm,), in_specs=in_specs,
            out_specs=pl.BlockSpec((tm, D), lambda i: (i, 0)),
            compiler_params=_cp(("parallel",)), name="norm_mod",
        )(x, g.reshape(1, D), mods)
    return pl.pallas_call(
        functools.partial(_norm_mod_t_kernel, which=which),
        out_shape=(jax.ShapeDtypeStruct((m, D), BF16), jax.ShapeDtypeStruct((D, m), BF16)),
        grid=(m // tm,), in_specs=in_specs,
        out_specs=(pl.BlockSpec((tm, D), lambda i: (i, 0)), pl.BlockSpec((D, tm), lambda i: (0, i))),
        compiler_params=_cp(("parallel",)), name="norm_mod_t",
    )(x, g.reshape(1, D), mods)


def _final_norm_kernel(x_ref, g_ref, o_ref):
    x = x_ref[...]
    o_ref[...] = x * lax.rsqrt(jnp.mean(x * x, axis=-1, keepdims=True) + EPS) * g_ref[...]


def rms_final(x, g, tm=512):
    m = x.shape[0]
    return pl.pallas_call(
        _final_norm_kernel,
        out_shape=jax.ShapeDtypeStruct((m, D), F32),
        grid=(m // tm,),
        in_specs=[pl.BlockSpec((tm, D), lambda i: (i, 0)), pl.BlockSpec((1, D), lambda i: (0, 0))],
        out_specs=pl.BlockSpec((tm, D), lambda i: (i, 0)),
        compiler_params=_cp(("parallel",)), name="final_norm",
    )(x, g.reshape(1, D))


def _mm_kernel(a_ref, w_ref, o_ref):
    o_ref[...] = _dot(a_ref[...], w_ref[...]).astype(o_ref.dtype)


def _mm_res_kernel(a_ref, w_ref, res_ref, m_ref, o_ref, *, gate):
    o_ref[...] = res_ref[...] + m_ref[0, gate:gate + 1, :] * _dot(a_ref[...], w_ref[...])


def matmul(a, w, out_dtype=F32, tm=512, tn=1024):
    m, n = a.shape[0], w.shape[1]
    tn = min(tn, n)
    return pl.pallas_call(
        _mm_kernel,
        out_shape=jax.ShapeDtypeStruct((m, n), out_dtype),
        grid=(m // tm, n // tn),
        in_specs=[pl.BlockSpec((tm, D), lambda i, j: (i, 0)), pl.BlockSpec((D, tn), lambda i, j: (0, j))],
        out_specs=pl.BlockSpec((tm, tn), lambda i, j: (i, j)),
        compiler_params=_cp(("parallel", "parallel")), name="matmul",
    )(a, w)


def matmul_res(a, w, res, mods, tpc, gate, tm=512):
    m = a.shape[0]
    return pl.pallas_call(
        functools.partial(_mm_res_kernel, gate=gate),
        out_shape=jax.ShapeDtypeStruct((m, D), F32),
        grid=(m // tm,),
        in_specs=[pl.BlockSpec((tm, D), lambda i: (i, 0)),
                  pl.BlockSpec((D, D), lambda i: (0, 0)),
                  pl.BlockSpec((tm, D), lambda i: (i, 0)),
                  pl.BlockSpec((1, N_MOD, D), lambda i: ((i * tm) // tpc, 0, 0))],
        out_specs=pl.BlockSpec((tm, D), lambda i: (i, 0)),
        compiler_params=_cp(("parallel",)), name="matmul_res",
    )(a, w, res, mods)


def _dft_mats(t):
    def cs(n):
        k = np.arange(n)
        ang = 2.0 * np.pi * ((k[:, None] * k[None, :]) % n) / n
        s = 1.0 / math.sqrt(n)
        return np.cos(ang) * s, np.sin(ang) * s
    cc, sc = cs(FNET_GD)
    ct, st = cs(t)
    return (jnp.asarray(np.concatenate([cc, sc], axis=1), BF16), jnp.asarray(ct, BF16), jnp.asarray(st, BF16))


def _dft_kernel(u_ref, cs_ref, ct_ref, st_ref, o_ref):
    p = _dot(u_ref[...].astype(BF16), cs_ref[...])
    pc = p[:, :FNET_GD].astype(BF16)
    ps = p[:, FNET_GD:].astype(BF16)
    o_ref[...] = (_dot(ct_ref[...], pc) - _dot(st_ref[...], ps)).astype(o_ref.dtype)


def fnet_dft(u, t):
    m = u.shape[0]
    cs, ct, st = _dft_mats(t)
    return pl.pallas_call(
        _dft_kernel,
        out_shape=jax.ShapeDtypeStruct((m, D), BF16),
        grid=(m // t, FNET_GROUPS),
        in_specs=[pl.BlockSpec((t, FNET_GD), lambda s, g: (s, g)),
                  pl.BlockSpec((FNET_GD, 2 * FNET_GD), lambda s, g: (0, 0)),
                  pl.BlockSpec((t, t), lambda s, g: (0, 0)),
                  pl.BlockSpec((t, t), lambda s, g: (0, 0))],
        out_specs=pl.BlockSpec((t, FNET_GD), lambda s, g: (s, g)),
        compiler_params=_cp(("parallel", "parallel")), name="fnet_dft",
    )(u, cs, ct, st)


def _head_ones():
    i = np.arange(D) // HEAD
    return jnp.asarray(i[:, None] == i[None, :], BF16)


def _rwkv_proj_kernel(x_ref, xp_ref, xn_ref, g_ref, m_ref, mu_ref, wr_ref, wk_ref, wv_ref, g1_ref, g2_ref,
                      w0_ref, w1_ref, w2_ref, a0_ref, a1_ref, a2_ref, kk_ref, ka_ref, rk_ref, ones_ref,
                      r_out, v_out, kkn_out, g_out, bonus_out, lw_out, kd_out, bd_out, *, tm, t):
    i = pl.program_id(0)
    shift = m_ref[0, 0:1, :]
    scale = m_ref[0, 1:2, :]
    g = g_ref[...]
    h = _modulate(x_ref[...], g, shift, scale)
    first = (i * tm) % t == 0
    last = ((i + 1) * tm) % t == 0
    hp = jnp.where(first, 0.0, _modulate(xp_ref[7:8, :], g, shift, scale))
    hn = jnp.where(last, 0.0, _modulate(xn_ref[0:1, :], g, shift, scale))
    row = lax.broadcasted_iota(jnp.int32, (tm, 1), 0)
    prev = jnp.where(row == 0, hp, pltpu.roll(h, 1, axis=0))
    nxt = jnp.where(row == tm - 1, hn, pltpu.roll(h, tm - 1, axis=0))
    xx = 0.5 * (prev + nxt) - h

    def mix(j):
        return (h + xx * mu_ref[j:j + 1, :]).astype(BF16)

    r = _dot(mix(0), wr_ref[...])
    k = _dot(mix(2), wk_ref[...])
    v = _dot(mix(3), wv_ref[...])
    gate = _dot(jax.nn.sigmoid(_dot(mix(5), g1_ref[...])).astype(BF16), g2_ref[...])
    xw = mix(1)
    xa = mix(4)
    ones = ones_ref[...]
    kk = k * kk_ref[...]
    kk = kk * lax.rsqrt(_split_dot(kk * kk, ones) + 1e-12)
    ksum = jnp.zeros_like(k)
    for j in range(2):
        w_raw = w0_ref[j:j + 1, :] + _dot(jnp.tanh(_dot(xw, w1_ref[j])).astype(BF16), w2_ref[j])
        lw_out[j] = -jnp.exp(-jax.nn.softplus(-w_raw) - 0.5)
        a = jax.nn.sigmoid(a0_ref[j:j + 1, :] + _dot(_dot(xa, a1_ref[j]).astype(BF16), a2_ref[j]))
        kd = k * (1 + (a - 1) * ka_ref[...])
        kd_out[j] = kd
        bd_out[j] = kk * a
        ksum = ksum + kd
    r_out[...] = r
    v_out[...] = v
    kkn_out[...] = kk
    g_out[...] = gate
    bonus_out[...] = _split_dot(r * ksum * rk_ref[...], ones) * v


def rwkv_proj(x, norm_g, mods, tpc, t, p, tm=256):
    m = x.shape[0]
    nb8 = m // 8
    full = lambda *shape: pl.BlockSpec(shape, lambda i: (0,) * len(shape))
    tok = pl.BlockSpec((tm, D), lambda i: (i, 0))
    tok2 = pl.BlockSpec((2, tm, D), lambda i: (0, i, 0))
    in_specs = [tok,
                pl.BlockSpec((8, D), lambda i: (jnp.maximum(i * (tm // 8) - 1, 0), 0)),
                pl.BlockSpec((8, D), lambda i: (jnp.minimum((i + 1) * (tm // 8), nb8 - 1), 0)),
                full(1, D),
                pl.BlockSpec((1, N_MOD, D), lambda i: ((i * tm) // tpc, 0, 0)),
                full(6, D), full(D, D), full(D, D), full(D, D), full(D, 128), full(128, D),
                full(2, D), full(2, D, 64), full(2, 64, D), full(2, D), full(2, D, 64), full(2, 64, D),
                full(1, D), full(1, D), full(1, D), full(D, D)]
    sd = jax.ShapeDtypeStruct
    return pl.pallas_call(
        functools.partial(_rwkv_proj_kernel, tm=tm, t=t),
        out_shape=(sd((m, D), F32),) * 5 + (sd((2, m, D), F32),) * 3,
        grid=(m // tm,), in_specs=in_specs,
        out_specs=(tok,) * 5 + (tok2,) * 3,
        compiler_params=_cp(("parallel",)), name="rwkv_proj",
    )(x, x, x, norm_g.reshape(1, D), mods, p["mu"], p["w_r"], p["w_k"], p["w_v"], p["g1"], p["g2"],
      p["w0"], p["w1"], p["w2"], p["a0"], p["a1"], p["a2"], p["k_k"], p["k_a"], p["r_k"], _head_ones())


def _rwkv_scan_kernel(r_ref, v_ref, kk_ref, lw_ref, kd_ref, b_ref, z0_ref, y_ref, zout_ref, z_scr, *, n_chunks):
    L = SCAN_L
    d = pl.program_id(1)
    c = pl.program_id(2)

    row = lax.broadcasted_iota(jnp.int32, (L, L), 0)
    col = lax.broadcasted_iota(jnp.int32, (L, L), 1)

    @pl.when(c == 0)
    def _():
        dup = (lax.broadcasted_iota(jnp.int32, (HEAD, L), 1) % HEAD
               == lax.broadcasted_iota(jnp.int32, (HEAD, L), 0)).astype(BF16)
        for p in range(N_HEADS // 2):
            both = jnp.concatenate([_split3_dot(z0_ref[0, 0, 2 * p], dup), _split3_dot(z0_ref[0, 0, 2 * p + 1], dup)],
                                   axis=0)
            z_scr[p] = jnp.where((row // HEAD) == (col // HEAD), both, 0.0)

    fwd = d == 0
    order = (col - row) * (1 - 2 * d)
    before = order < 0
    upto = order <= 0
    cum_mat = upto.astype(BF16)
    same_head = (row // HEAD) == (col // HEAD)
    lane = lax.broadcasted_iota(jnp.int32, (1, 2 * HEAD), 1)
    head_mask = (lane < HEAD, lane >= HEAD)
    n_double = int(math.log2(L))

    def prepare(p):
        sl = slice(p * 2 * HEAD, (p + 1) * 2 * HEAD)
        lw = lw_ref[0, :, sl]
        cum = _split_dot_left(cum_mat, lw)
        tot = jnp.where(fwd, cum[L - 1:L, :], cum[0:1, :])
        inv = jnp.exp(-cum)
        ar = jnp.concatenate([-kk_ref[:, sl] * jnp.exp(cum - lw), r_ref[:, sl] * jnp.exp(cum)], axis=0)
        bk = jnp.concatenate([b_ref[0, :, sl] * inv, kd_ref[0, :, sl] * inv], axis=0).astype(BF16)
        z = z_scr[p]
        base = _dot_nt(ar.astype(BF16), z.astype(BF16))
        return dict(ar=ar, bk=bk, z=z, base=base, v=v_ref[:, sl], tot=tot)

    def start_chain(pp, hm):
        g4 = _dot_nt(jnp.where(hm, pp["ar"], 0.0).astype(BF16), pp["bk"])
        vm = jnp.where(hm, pp["v"], 0.0).astype(BF16)
        n = jnp.where(before, g4[:L, :L], 0.0)
        x = jnp.where(hm, pp["base"][:L], 0.0) + _dot(jnp.where(before, g4[:L, L:], 0.0).astype(BF16), vm)
        out = jnp.concatenate([jnp.where(upto, g4[L:, :L], 0.0), jnp.where(upto, g4[L:, L:], 0.0)], axis=1)
        return dict(n=n, x=x, out=out.astype(BF16), vm=vm)

    pairs = [prepare(p) for p in range(N_HEADS // 2)]
    chains = [[start_chain(pp, hm) for hm in head_mask] for pp in pairs]
    for it in range(n_double):
        for ch in (ch for pair in chains for ch in pair):
            nb = ch["n"].astype(BF16)
            xb = ch["x"].astype(BF16)
            if it + 1 < n_double:
                res = _dot(nb, jnp.concatenate([xb, nb], axis=1))
                ch["x"] = ch["x"] + res[:, :2 * HEAD]
                ch["n"] = res[:, 2 * HEAD:]
            else:
                ch["x"] = ch["x"] + _dot(nb, xb)
    for p, (pp, pair) in enumerate(zip(pairs, chains)):
        y = pp["base"][L:]
        for ch in pair:
            y = y + _dot(ch["out"], jnp.concatenate([ch["x"].astype(BF16), ch["vm"]], axis=0))
        uv = jnp.concatenate([pair[0]["x"] + pair[1]["x"], pp["v"]], axis=0)
        inc = _dot(uv.T.astype(BF16), pp["bk"])
        z_scr[p] = jnp.where(same_head, pp["z"] + inc, 0.0) * jnp.exp(pp["tot"])
        y_ref[0, :, p * 2 * HEAD:(p + 1) * 2 * HEAD] = y

    @pl.when(c == n_chunks - 1)
    def _():
        prow = lax.broadcasted_iota(jnp.int32, (L, HEAD), 0)
        pcol = lax.broadcasted_iota(jnp.int32, (L, HEAD), 1)
        pick_a = (prow == pcol).astype(BF16)
        pick_b = (prow == pcol + HEAD).astype(BF16)
        for p in range(N_HEADS // 2):
            z = z_scr[p]
            zout_ref[0, 0, 2 * p] = _split3_dot(z[:HEAD, :], pick_a)
            zout_ref[0, 0, 2 * p + 1] = _split3_dot(z[HEAD:, :], pick_b)


def _split3_dot(x, w):
    hi = x.astype(BF16)
    r1 = x - hi.astype(F32)
    mid = r1.astype(BF16)
    lo = (r1 - mid.astype(F32)).astype(BF16)
    return _dot(hi, w) + _dot(mid, w) + _dot(lo, w)


def _split_dot_left(w, x):
    hi = x.astype(BF16)
    lo = (x - hi.astype(F32)).astype(BF16)
    return _dot(w, hi) + _dot(w, lo)


def rwkv_scan(r, v, kk, lw, kd, bd, z0, t):
    m = r.shape[0]
    n_seq = m // t
    nc = t // SCAN_L

    def blk(s, d, c):
        return s * nc + c + d * (nc - 1 - 2 * c)

    tok = pl.BlockSpec((SCAN_L, D), lambda s, d, c: (blk(s, d, c), 0))
    tok2 = pl.BlockSpec((1, SCAN_L, D), lambda s, d, c: (d, blk(s, d, c), 0))
    zspec = pl.BlockSpec((1, 1, N_HEADS, HEAD, HEAD), lambda s, d, c: (s, d, 0, 0, 0))
    return pl.pallas_call(
        functools.partial(_rwkv_scan_kernel, n_chunks=nc),
        out_shape=(jax.ShapeDtypeStruct((2, m, D), F32), jax.ShapeDtypeStruct(z0.shape, F32)),
        grid=(n_seq, 2, nc),
        in_specs=[tok, tok, tok, tok2, tok2, tok2, zspec],
        out_specs=(tok2, zspec),
        scratch_shapes=[pltpu.VMEM((N_HEADS // 2, 2 * HEAD, 2 * HEAD), F32)],
        compiler_params=_cp(("parallel", "parallel", "arbitrary")), name="rwkv_scan",
    )(r, v, kk, lw, kd, bd, z0)


def _rwkv_post_kernel(y_ref, bonus_ref, g_ref, lg_ref, lb_ref, ones_ref, o_ref):
    ones = ones_ref[...]
    o = y_ref[0] + y_ref[1]
    cen = o - _split_dot(o, ones) * (1.0 / HEAD)
    var = _split_dot(cen * cen, ones) * (1.0 / HEAD)
    o = cen * lax.rsqrt(var + LNX_EPS) * lg_ref[...] + lb_ref[...] + bonus_ref[...]
    o_ref[...] = (o * g_ref[...]).astype(o_ref.dtype)


def rwkv_post(y, bonus, g, lnx_g, lnx_b, tm=256):
    m = bonus.shape[0]
    tok = pl.BlockSpec((tm, D), lambda i: (i, 0))
    row = pl.BlockSpec((1, D), lambda i: (0, 0))
    return pl.pallas_call(
        _rwkv_post_kernel,
        out_shape=jax.ShapeDtypeStruct((m, D), BF16),
        grid=(m // tm,),
        in_specs=[pl.BlockSpec((2, tm, D), lambda i: (0, i, 0)), tok, tok, row, row,
                  pl.BlockSpec((D, D), lambda i: (0, 0))],
        out_specs=tok,
        compiler_params=_cp(("parallel",)), name="rwkv_post",
    )(y, bonus, g, lnx_g.reshape(1, D), lnx_b.reshape(1, D), _head_ones())


def rwkv_layer(x, norm_g, mods, tpc, t, s0, p):
    r, v, kk, g, bonus, lw, kd, bd = rwkv_proj(x, norm_g, mods, tpc, t, p)
    y, zf = rwkv_scan(r, v, kk, lw, kd, bd, s0, t)
    o = rwkv_post(y, bonus, g, p["lnx_g"], p["lnx_b"])
    return matmul_res(o, p["w_o"], x, mods, tpc, 2), zf


def _softmax_rows(parts):
    m = parts[0].max(axis=-1, keepdims=True)
    for s in parts[1:]:
        m = jnp.maximum(m, s.max(axis=-1, keepdims=True))
    es = [jnp.exp(s - m) for s in parts]
    den = es[0].sum(axis=-1, keepdims=True)
    for e in es[1:]:
        den = den + e.sum(axis=-1, keepdims=True)
    inv = 1.0 / den
    return [(e * inv).astype(BF16) for e in es]


def _qkv_kernel(a_ref, w_ref, *out_refs, with_f32):
    j = pl.program_id(1)
    r = _dot(a_ref[...], w_ref[...])
    for idx in range(3):
        @pl.when(j == idx)
        def _(idx=idx):
            out_refs[idx][...] = r.astype(BF16)
            if with_f32 and idx > 0:
                out_refs[2 + idx][...] = r


def qkv_proj(a, w, with_f32, tm=512):
    m = a.shape[0]
    tok = pl.BlockSpec((tm, D), lambda i, j: (i, 0))
    sd = jax.ShapeDtypeStruct
    n_f32 = 2 if with_f32 else 0
    return pl.pallas_call(
        functools.partial(_qkv_kernel, with_f32=with_f32),
        out_shape=(sd((m, D), BF16),) * 3 + (sd((m, D), F32),) * n_f32,
        grid=(m // tm, 3),
        in_specs=[tok, pl.BlockSpec((D, D), lambda i, j: (0, j))],
        out_specs=(tok,) * (3 + n_f32),
        compiler_params=_cp(("parallel", "arbitrary")), name="qkv_proj",
    )(a, w)


def _pair_masks():
    lane = lax.broadcasted_iota(jnp.int32, (1, 2 * HEAD), 1)
    return lane < HEAD, lane >= HEAD


def _na_ctx_kernel(q_ref, k_ref, v_ref, o_ref):
    masks = _pair_masks()
    zero = jnp.zeros((), BF16)
    scores = []
    for p in range(N_HEADS // 2):
        sl = slice(p * 2 * HEAD, (p + 1) * 2 * HEAD)
        q = q_ref[:, sl]
        k = k_ref[:, sl]
        scores.append([_dot_nt(jnp.where(hm, q, zero), k) * NA_SCALE for hm in masks])
    probs = [[_softmax_rows([s])[0] for s in pair] for pair in scores]
    for p in range(N_HEADS // 2):
        sl = slice(p * 2 * HEAD, (p + 1) * 2 * HEAD)
        v = v_ref[:, sl]
        o_ref[:, sl] = jnp.where(masks[0], _dot(probs[p][0], v), _dot(probs[p][1], v)).astype(o_ref.dtype)


def na_ctx_attn(q, k, v, t):
    m = q.shape[0]
    seq = pl.BlockSpec((t, D), lambda b: (b, 0))
    return pl.pallas_call(
        _na_ctx_kernel,
        out_shape=jax.ShapeDtypeStruct((m, D), BF16),
        grid=(m // t,),
        in_specs=[seq, seq, seq],
        out_specs=seq,
        compiler_params=_cp(("parallel",)), name="na_ctx_attn",
    )(q, k, v)


def _win_start(r, rows):
    return jnp.clip(r - WIN_ROWS // 2, 0, rows - WIN_ROWS)


def _na_lat_kernel(q_ref, k_ref, v_ref, ck_ref, cv_ref, bias_ref, o_ref, *, rows):
    r = pl.program_id(1)
    start = pl.multiple_of(_win_start(r, rows) * GRID_W, GRID_W)
    n_loc = WIN_ROWS * GRID_W
    qc = lax.broadcasted_iota(jnp.int32, (GRID_W, n_loc), 0)
    kc = lax.broadcasted_iota(jnp.int32, (GRID_W, n_loc), 1) % GRID_W
    cs = jnp.clip(qc - WIN_COLS // 2, 0, GRID_W - WIN_COLS)
    valid = (kc >= cs) & (kc < cs + WIN_COLS)
    masks = _pair_masks()
    zero = jnp.zeros((), BF16)
    scores = []
    for p in range(N_HEADS // 2):
        sl = slice(p * 2 * HEAD, (p + 1) * 2 * HEAD)
        q = q_ref[:, sl]
        kw = k_ref[pl.ds(start, n_loc), sl]
        ck = ck_ref[:, sl]
        pair = []
        for i, hm in enumerate(masks):
            qm = jnp.where(hm, q, zero)
            s_loc = jnp.where(valid, _dot_nt(qm, kw) * NA_SCALE + bias_ref[0, 2 * p + i], NEG_INF)
            pair.append([s_loc, _dot_nt(qm, ck) * NA_SCALE])
        scores.append(pair)
    probs = [[_softmax_rows(parts) for parts in pair] for pair in scores]
    for p in range(N_HEADS // 2):
        sl = slice(p * 2 * HEAD, (p + 1) * 2 * HEAD)
        vw = v_ref[pl.ds(start, n_loc), sl]
        cv = cv_ref[:, sl]
        outs = [_dot(p_loc, vw) + _dot(p_ctx, cv) for p_loc, p_ctx in probs[p]]
        o_ref[:, sl] = jnp.where(masks[0], outs[0], outs[1]).astype(o_ref.dtype)


def _na_bias(rpb, rows):
    wr = min(WIN_ROWS, rows)
    qc = np.arange(GRID_W)
    col_off = np.clip(qc[None, :] - qc[:, None], -(WIN_COLS - 1), WIN_COLS - 1) + WIN_COLS - 1
    onehot = jnp.asarray(np.arange(2 * WIN_COLS - 1)[:, None, None] == col_off[None], F32)
    b = jnp.einsum('hrc,cqk->hrqk', rpb.astype(F32), onehot, precision=lax.Precision.HIGHEST)
    b = jnp.stack([b[:, ro0:ro0 + wr] for ro0 in range(WIN_ROWS)], axis=0)
    return jnp.transpose(b, (0, 1, 3, 2, 4)).reshape(WIN_ROWS, N_HEADS, GRID_W, wr * GRID_W)


def na_lat_attn(q, k, v, ck, cv, rpb, t, past):
    m = q.shape[0]
    rows = t // GRID_W
    assert rows >= WIN_ROWS
    n_seq = m // t
    n_loc = WIN_ROWS * GRID_W

    def bias_idx(b, r):
        return (_win_start(r, rows) - r + WIN_ROWS - 1, 0, 0, 0)

    return pl.pallas_call(
        functools.partial(_na_lat_kernel, rows=rows),
        out_shape=jax.ShapeDtypeStruct((m, D), BF16),
        grid=(n_seq, rows),
        in_specs=[pl.BlockSpec((GRID_W, D), lambda b, r: (b * rows + r, 0)),
                  pl.BlockSpec((t, D), lambda b, r: (b, 0)),
                  pl.BlockSpec((t, D), lambda b, r: (b, 0)),
                  pl.BlockSpec((past, D), lambda b, r: (b, 0)),
                  pl.BlockSpec((past, D), lambda b, r: (b, 0)),
                  pl.BlockSpec((1, N_HEADS, GRID_W, n_loc), bias_idx)],
        out_specs=pl.BlockSpec((GRID_W, D), lambda b, r: (b * rows + r, 0)),
        compiler_params=_cp(("parallel", "arbitrary")), name="na_lat_attn",
    )(q, k, v, ck, cv, _na_bias(rpb, rows))


def _sort16_net():
    def merge(lo, hi, r):
        step = r * 2
        if step < hi - lo:
            yield from merge(lo, hi, step)
            yield from merge(lo + r, hi, step)
            yield from ((i, i + r) for i in range(lo + r, hi - r, step))
        else:
            yield (lo, lo + r)

    def sort(lo, hi):
        if hi - lo >= 1:
            mid = lo + (hi - lo) // 2
            yield from sort(lo, mid)
            yield from sort(mid + 1, hi)
            yield from merge(lo, hi, 1)

    return tuple(sort(0, PEER_TOPK - 1))


_SORT16 = _sort16_net()
_BITONIC16 = tuple((i, i + d) for d in (8, 4, 2, 1) for i in range(PEER_TOPK) if not i & d)


def _exchange(x, net):
    x = list(x)
    for i, j in net:
        x[i], x[j] = jnp.maximum(x[i], x[j]), jnp.minimum(x[i], x[j])
    return x


def _merge_top16(a, b):
    return _exchange([jnp.maximum(a[i], b[PEER_TOPK - 1 - i]) for i in range(PEER_TOPK)], _BITONIC16)


def _merge_sublanes(x):
    for shift in (4, 2, 1):
        x = _merge_top16(x, [pltpu.roll(v, shift, axis=0) for v in x])
    return x


def _peer_stats_kernel(q_ref, keys_ref, cut_out, e1_out, tab_out):
    tm = q_ref.shape[0]
    sub = lax.broadcasted_iota(jnp.int32, (8, tm), 0)
    ninf = jnp.full((8, tm), -jnp.inf, F32)
    for h in range(PEER_HEADS):
        s, tops = [], []
        for c in range(2):
            hc = 2 * h + c
            q = q_ref[:, hc * PEER_KEYS:(hc + 1) * PEER_KEYS].astype(BF16)
            sc = _dot_nt(keys_ref[hc], q)
            s.append(sc)
            groups = [sc[8 * i:8 * i + 8, :] for i in range(PEER_KEYS // 8)]
            tops.append(_merge_sublanes(_exchange(groups, _SORT16)))
        a1, a2 = tops
        a1col = a1[7]
        for jj in range(6, -1, -1):
            a1col = jnp.where(sub == jj, a1[jj], a1col)
        lists = [jnp.where(sub < min(8, PEER_TOPK // (k + 1)), a1col + a2[k], ninf) for k in range(PEER_TOPK)]
        tail = [a1[8 + k] + a2[0] for k in range(8)] + [ninf] * 8
        best = _merge_top16(_merge_sublanes(lists), tail)
        z = jnp.zeros((8, tm), F32)
        for b in best:
            z = z + jnp.exp(b - best[0])
        thr = best[PEER_TOPK - 1][0:1, :]
        cut = jnp.full((PEER_KEYS, tm), jnp.inf, F32)
        for k in range(PEER_TOPK):
            a2k = a2[k][0:1, :]
            cut = jnp.where(s[0] + a2k >= thr, a2k, cut)
        cut_out[h] = cut
        e1_out[h] = jnp.exp(s[0] - a1[0][0:1, :]) * (0.5 / z[0:1, :])
        e2 = jnp.exp(s[1] - a2[0][0:1, :])
        for lb in range(tm // 128):
            lanes = slice(lb * 128, (lb + 1) * 128)
            tab_out[lb, h, :, 0] = s[1][:, lanes].reshape(PEER_KEYS // 8, 8, 128)
            tab_out[lb, h, :, 1] = e2[:, lanes].reshape(PEER_KEYS // 8, 8, 128)


def peer_stats(q, keys, tm=256):
    m = q.shape[0]
    nq = 2 * PEER_HEADS * PEER_KEYS
    sd = jax.ShapeDtypeStruct
    big = pl.BlockSpec((PEER_HEADS, PEER_KEYS, tm), lambda i: (0, 0, i))
    tab_shape = (PEER_HEADS, PEER_KEYS // 8, 2, 8, 128)
    return pl.pallas_call(
        _peer_stats_kernel,
        out_shape=(sd((PEER_HEADS, PEER_KEYS, m), F32),) * 2 + (sd((m // 128,) + tab_shape, F32),),
        grid=(m // tm,),
        in_specs=[pl.BlockSpec((tm, nq), lambda i: (i, 0)),
                  pl.BlockSpec((2 * PEER_HEADS, PEER_KEYS, PEER_KEYS), lambda i: (0, 0, 0))],
        out_specs=(big,) * 2 + (pl.BlockSpec((tm // 128,) + tab_shape, lambda i: (i, 0, 0, 0, 0, 0)),),
        compiler_params=_cp(("parallel",)), name="peer_stats",
    )(q, keys)


def _gelu_x2(x):
    return x * (1.0 + lax.erf(x * (1.0 / math.sqrt(2.0))))


def _peer_dense_kernel(xt_ref, u_ref, un_ref, vt_ref, vp_ref, cut_ref, e1_ref, tab_ref, res_ref, m_ref, o_ref,
                       acc_scr, a_scr, w_scr, *, tm, tn, sub, gate):
    j = pl.program_id(1)
    n_sub = tn // sub
    n_cb = tm // 128
    out_rows = D // n_cb

    def act_matmul(sb):
        return _dot(u_ref[sb * sub:(sb + 1) * sub, :], xt_ref[...])

    @pl.when(j == 0)
    def _():
        acc_scr[...] = jnp.zeros_like(acc_scr)
        w_scr[...] = jnp.zeros_like(w_scr)
        a_scr[...] = act_matmul(0)

    def weights(sb, cb, a):
        lanes = slice(cb * 128, (cb + 1) * 128)
        n_il = sub // PEER_KEYS
        n_rg = PEER_KEYS // PEER_ROWS
        parts = [[None] * n_rg for _ in range(n_il)]
        for rg in range(n_rg):
            grp = slice(rg * PEER_ROWS // 8, (rg + 1) * PEER_ROWS // 8)
            g = [jnp.zeros((PEER_ROWS, 128), F32)] * n_il
            for h in range(PEER_HEADS):
                s2 = tab_ref[cb, h, grp, 0].reshape(PEER_ROWS, 128)
                e2 = tab_ref[cb, h, grp, 1].reshape(PEER_ROWS, 128)
                for il in range(n_il):
                    i1 = sb * n_il + il
                    hit = s2 >= cut_ref[h, i1:i1 + 1, lanes]
                    g[il] = g[il] + jnp.where(hit, e2 * e1_ref[h, i1:i1 + 1, lanes], 0.0)
            for il in range(n_il):
                r0 = il * PEER_KEYS + rg * PEER_ROWS
                parts[il][rg] = (g[il] * _gelu_x2(a[r0:r0 + PEER_ROWS, lanes])).astype(BF16)
        return jnp.concatenate([p for row in parts for p in row], axis=0)

    def out_piece(sb, q, w):
        rows = slice(q * out_rows, (q + 1) * out_rows)
        if sb < 0:
            return _dot(vp_ref[rows, :], w)
        return _dot(vt_ref[rows, sb * sub:(sb + 1) * sub], w)

    a = a_scr[...]
    w_prev = w_scr[...]
    contrib = [None] * n_cb
    for sb in range(n_sub):
        cols = []
        for cb in range(n_cb):
            if cb == 0:
                a_next = act_matmul(sb + 1) if sb + 1 < n_sub else _dot(un_ref[...], xt_ref[...])
            piece = out_piece(sb - 1, cb, w_prev)
            contrib[cb] = piece if contrib[cb] is None else contrib[cb] + piece
            cols.append(weights(sb, cb, a))
        w_prev = jnp.concatenate(cols, axis=1)
        a = a_next
    for q in range(n_cb):
        acc_scr[q * out_rows:(q + 1) * out_rows, :] += contrib[q]
    a_scr[...] = a
    w_scr[...] = w_prev

    @pl.when(j == pl.num_programs(1) - 1)
    def _():
        for q in range(n_cb):
            acc_scr[q * out_rows:(q + 1) * out_rows, :] += out_piece(n_sub - 1, q, w_scr[...])
        o_ref[...] = res_ref[...] + m_ref[0, gate:gate + 1, :] * acc_scr[...].T


def peer_dense(xt, u, vt, cut, e1, tab, res, mods, tpc, gate, tm=512, tn=1024, sub=256):
    m = xt.shape[1]
    n1 = tn // PEER_KEYS
    part = pl.BlockSpec((PEER_HEADS, n1, tm), lambda i, j: (0, j, i))
    full = pl.BlockSpec((tm // 128,) + tab.shape[1:], lambda i, j: (i, 0, 0, 0, 0, 0))
    n_sub = tn // sub
    last_sub = N_EXPERTS // sub - 1
    return pl.pallas_call(
        functools.partial(_peer_dense_kernel, tm=tm, tn=tn, sub=sub, gate=gate),
        out_shape=jax.ShapeDtypeStruct((m, D), F32),
        grid=(m // tm, N_EXPERTS // tn),
        in_specs=[pl.BlockSpec((D, tm), lambda i, j: (0, i)),
                  pl.BlockSpec((tn, D), lambda i, j: (j, 0)),
                  pl.BlockSpec((sub, D), lambda i, j: (jnp.minimum((j + 1) * n_sub, last_sub), 0)),
                  pl.BlockSpec((D, tn), lambda i, j: (0, j)),
                  pl.BlockSpec((D, sub), lambda i, j: (0, jnp.maximum(j * n_sub - 1, 0))),
                  part, part, full,
                  pl.BlockSpec((tm, D), lambda i, j: (i, 0)),
                  pl.BlockSpec((1, N_MOD, D), lambda i, j: ((i * tm) // tpc, 0, 0))],
        out_specs=pl.BlockSpec((tm, D), lambda i, j: (i, 0)),
        scratch_shapes=[pltpu.VMEM((D, tm), F32), pltpu.VMEM((sub, tm), F32), pltpu.VMEM((sub, tm), BF16)],
        compiler_params=_cp(("parallel", "arbitrary")), name="peer_dense",
    )(xt, u, u, vt, vt, cut, e1, tab, res, mods)


def peer_layer(x, norm_g, mods, tpc, wq, keys, u, vt):
    h, ht = norm_mod(x, norm_g, mods, tpc, 3, transposed=True)
    q = matmul(h, wq)
    cut, e1, tab = peer_stats(q, keys)
    return peer_dense(ht, u, vt, cut, e1, tab, x, mods, tpc, 5)


def kernel(x_prompt, x_sample, c, state_wkv, cache_k, cache_v, c_ctx, ada_w, ada_b, norm_mix, norm_ffn, fnet_w_in, fnet_w_out, rwkv_mu, rwkv_w_r, rwkv_w_k, rwkv_w_v, rwkv_w_o, rwkv_w0, rwkv_w1, rwkv_w2, rwkv_a0, rwkv_a1, rwkv_a2, rwkv_g1, rwkv_g2, rwkv_k_k, rwkv_k_a, rwkv_r_k, rwkv_lnx_g, rwkv_lnx_b, na_w_qkv, na_w_o, na_rpb, peer_w_q, peer_sub_keys, peer_u, peer_v, final_norm):
    nb_c, t_c, _ = x_prompt.shape
    nb_s, t_s, _ = x_sample.shape
    depth = ada_w.shape[0]
    past = cache_k.shape[2]
    bf = lambda w: w.astype(BF16)

    cond = jnp.concatenate([c_ctx[None, :], c, jnp.zeros((16 - 1 - nb_s, D), F32)], axis=0)
    mods_all = ada_all(cond, ada_w, ada_b).reshape(depth, 16, N_MOD, D)

    xc = x_prompt.reshape(nb_c * t_c, D)
    xs = x_sample.reshape(nb_s * t_s, D)
    tpc_c, tpc_s = nb_c * t_c, t_s
    new_wkv, new_k, new_v = [], [], []

    for l in range(depth):
        kind, j = l % 3, l // 3
        mc = mods_all[l, 0:1]
        ms = mods_all[l, 1:1 + nb_s]
        if kind == 0:
            w_in, w_out = bf(fnet_w_in[j]), bf(fnet_w_out[j])
            outs = []
            for x, m, tpc, t in ((xc, mc, tpc_c, t_c), (xs, ms, tpc_s, t_s)):
                h = norm_mod(x, norm_mix[l], m, tpc, 0)
                f = fnet_dft(matmul(h, w_in), t)
                outs.append(matmul_res(f, w_out, x, m, tpc, 2))
            xc, xs = outs
        elif kind == 1:
            p = dict(mu=rwkv_mu[j], w_r=bf(rwkv_w_r[j]), w_k=bf(rwkv_w_k[j]), w_v=bf(rwkv_w_v[j]), w_o=bf(rwkv_w_o[j]),
                     w0=rwkv_w0[j], w1=bf(rwkv_w1[j]), w2=bf(rwkv_w2[j]), a0=rwkv_a0[j], a1=bf(rwkv_a1[j]),
                     a2=bf(rwkv_a2[j]), g1=bf(rwkv_g1[j]), g2=bf(rwkv_g2[j]), k_k=rwkv_k_k[j].reshape(1, D),
                     k_a=rwkv_k_a[j].reshape(1, D), r_k=rwkv_r_k[j].reshape(1, D), lnx_g=rwkv_lnx_g[j],
                     lnx_b=rwkv_lnx_b[j])
            s_zero = jnp.zeros((nb_c, 2, N_HEADS, HEAD, HEAD), F32)
            xc, sc = rwkv_layer(xc, norm_mix[l], mc, tpc_c, t_c, s_zero, p)
            xs, _ = rwkv_layer(xs, norm_mix[l], ms, tpc_s, t_s, state_wkv[:, j], p)
            new_wkv.append(sc)
        else:
            w_qkv, w_o = bf(na_w_qkv[j]), bf(na_w_o[j])
            qc, kc, vc, kc32, vc32 = qkv_proj(norm_mod(xc, norm_mix[l], mc, tpc_c, 0), w_qkv, True)
            qs, ks, vs = qkv_proj(norm_mod(xs, norm_mix[l], ms, tpc_s, 0), w_qkv, False)
            new_k.append(kc32.reshape(nb_c, t_c, N_HEADS, HEAD))
            new_v.append(vc32.reshape(nb_c, t_c, N_HEADS, HEAD))
            oc = na_ctx_attn(qc, kc, vc, t_c)
            os_ = na_lat_attn(qs, ks, vs, bf(cache_k[:, j]).reshape(nb_s * past, D),
                              bf(cache_v[:, j]).reshape(nb_s * past, D), na_rpb[j], t_s, past)
            xc = matmul_res(oc, w_o, xc, mc, tpc_c, 2)
            xs = matmul_res(os_, w_o, xs, ms, tpc_s, 2)
        wq = bf(peer_w_q[l])
        keys = bf(peer_sub_keys[l]).reshape(2 * PEER_HEADS, PEER_KEYS, PEER_KEYS)
        u = bf(peer_u[l])
        vt = bf(peer_v[l]).T
        xc = peer_layer(xc, norm_ffn[l], mc, tpc_c, wq, keys, u, vt)
        xs = peer_layer(xs, norm_ffn[l], ms, tpc_s, wq, keys, u, vt)

    y_prompt = rms_final(xc, final_norm).reshape(nb_c, t_c, D)
    y_sample = rms_final(xs, final_norm).reshape(nb_s, t_s, D)
    return (y_prompt, y_sample, jnp.stack(new_wkv, axis=1), jnp.stack(new_k, axis=1), jnp.stack(new_v, axis=1))
```

```python
import functools
import math

import numpy as np
import jax
import jax.numpy as jnp
from jax import lax
from jax.experimental import pallas as pl
from jax.experimental.pallas import tpu as pltpu

F32 = jnp.float32
BF16 = jnp.bfloat16

D = 1024
N_MOD = 6
EPS = 1e-6
HEAD = 64
N_HEADS = D // HEAD
LNX_EPS = 64e-5
GRID_W = 64
WIN_ROWS = 8
WIN_COLS = 16
NA_SCALE = HEAD ** -0.5
NEG_INF = -1e30
FNET_GROUPS = 4
FNET_GD = D // FNET_GROUPS
PEER_KEYS = 128
PEER_HEADS = 8
PEER_TOPK = 16
N_EXPERTS = PEER_KEYS * PEER_KEYS
PEER_ROWS = 32
SCAN_L = 128

VMEM_LIMIT = 56 * 1024 * 1024


def _cp(sem, vmem=VMEM_LIMIT):
    return pltpu.CompilerParams(dimension_semantics=sem, vmem_limit_bytes=vmem)


def _dot(a, b):
    return jnp.dot(a, b, preferred_element_type=F32)


def _dot_nt(a, b):
    return lax.dot_general(a, b, (((1,), (1,)), ((), ())), preferred_element_type=F32)


def _split_dot(x, w):
    hi = x.astype(BF16)
    lo = (x - hi.astype(F32)).astype(BF16)
    return _dot(hi, w) + _dot(lo, w)


def _ada_kernel(c_ref, w_ref, b_ref, o_ref):
    c = c_ref[...]
    s = c * jax.nn.sigmoid(c)
    o_ref[0] = _dot(s.astype(BF16), w_ref[0].astype(BF16)) + b_ref[0]


def ada_all(cond16, ada_w, ada_b):
    depth = ada_w.shape[0]
    tn = 1024
    return pl.pallas_call(
        _ada_kernel,
        out_shape=jax.ShapeDtypeStruct((depth, 16, N_MOD * D), F32),
        grid=(depth, N_MOD * D // tn),
        in_specs=[pl.BlockSpec((16, D), lambda l, j: (0, 0)),
                  pl.BlockSpec((1, D, tn), lambda l, j: (l, 0, j)),
                  pl.BlockSpec((1, 1, tn), lambda l, j: (l, 0, j))],
        out_specs=pl.BlockSpec((1, 16, tn), lambda l, j: (l, 0, j)),
        compiler_params=_cp(("parallel", "parallel")),
        name="ada",
    )(cond16, ada_w, ada_b.reshape(depth, 1, N_MOD * D))


def _modulate(x, g, shift, scale):
    y = x * lax.rsqrt(jnp.mean(x * x, axis=-1, keepdims=True) + EPS)
    return (y * g) * (1 + scale) + shift


def _norm_mod_kernel(x_ref, g_ref, m_ref, o_ref, *, which):
    h = _modulate(x_ref[...], g_ref[...], m_ref[0, which:which + 1, :], m_ref[0, which + 1:which + 2, :])
    o_ref[...] = h.astype(o_ref.dtype)


def _norm_mod_t_kernel(x_ref, g_ref, m_ref, o_ref, ot_ref, *, which):
    h = _modulate(x_ref[...], g_ref[...], m_ref[0, which:which + 1, :], m_ref[0, which + 1:which + 2, :])
    o_ref[...] = h.astype(o_ref.dtype)
    ot_ref[...] = h.T.astype(ot_ref.dtype)


def norm_mod(x, g, mods, tpc, which, transposed=False, tm=512):
    m = x.shape[0]
    in_specs = [pl.BlockSpec((tm, D), lambda i: (i, 0)),
                pl.BlockSpec((1, D), lambda i: (0, 0)),
                pl.BlockSpec((1, N_MOD, D), lambda i: ((i * tm) // tpc, 0, 0))]
    if not transposed:
        return pl.pallas_call(
            functools.partial(_norm_mod_kernel, which=which),
            out_shape=jax.ShapeDtypeStruct((m, D), BF16),
            grid=(m // tm,), in_specs=in_specs,
            out_specs=pl.BlockSpec((tm, D), lambda i: (i, 0)),
            compiler_params=_cp(("parallel",)), name="norm_mod",
        )(x, g.reshape(1, D), mods)
    return pl.pallas_call(
        functools.partial(_norm_mod_t_kernel, which=which),
        out_shape=(jax.ShapeDtypeStruct((m, D), BF16), jax.ShapeDtypeStruct((D, m), BF16)),
        grid=(m // tm,), in_specs=in_specs,
        out_specs=(pl.BlockSpec((tm, D), lambda i: (i, 0)), pl.BlockSpec((D, tm), lambda i: (0, i))),
        compiler_params=_cp(("parallel",)), name="norm_mod_t",
    )(x, g.reshape(1, D), mods)


def _final_norm_kernel(x_ref, g_ref, o_ref):
    x = x_ref[...]
    o_ref[...] = x * lax.rsqrt(jnp.mean(x * x, axis=-1, keepdims=True) + EPS) * g_ref[...]


def rms_final(x, g, tm=512):
    m = x.shape[0]
    return pl.pallas_call(
        _final_norm_kernel,
        out_shape=jax.ShapeDtypeStruct((m, D), F32),
        grid=(m // tm,),
        in_specs=[pl.BlockSpec((tm, D), lambda i: (i, 0)), pl.BlockSpec((1, D), lambda i: (0, 0))],
        out_specs=pl.BlockSpec((tm, D), lambda i: (i, 0)),
        compiler_params=_cp(("parallel",)), name="final_norm",
    )(x, g.reshape(1, D))


def _mm_kernel(a_ref, w_ref, o_ref):
    o_ref[...] = _dot(a_ref[...], w_ref[...]).astype(o_ref.dtype)


def _mm_res_kernel(a_ref, w_ref, res_ref, m_ref, o_ref, *, gate):
    o_ref[...] = res_ref[...] + m_ref[0, gate:gate + 1, :] * _dot(a_ref[...], w_ref[...])


def matmul(a, w, out_dtype=F32, tm=512):
    m, n = a.shape[0], w.shape[1]
    tn = n
    return pl.pallas_call(
        _mm_kernel,
        out_shape=jax.ShapeDtypeStruct((m, n), out_dtype),
        grid=(m // tm, n // tn),
        in_specs=[pl.BlockSpec((tm, D), lambda i, j: (i, 0)), pl.BlockSpec((D, tn), lambda i, j: (0, j))],
        out_specs=pl.BlockSpec((tm, tn), lambda i, j: (i, j)),
        compiler_params=_cp(("parallel", "parallel")), name="matmul",
    )(a, w)


def matmul_res(a, w, res, mods, tpc, gate, tm=512):
    m = a.shape[0]
    return pl.pallas_call(
        functools.partial(_mm_res_kernel, gate=gate),
        out_shape=jax.ShapeDtypeStruct((m, D), F32),
        grid=(m // tm,),
        in_specs=[pl.BlockSpec((tm, D), lambda i: (i, 0)),
                  pl.BlockSpec((D, D), lambda i: (0, 0)),
                  pl.BlockSpec((tm, D), lambda i: (i, 0)),
                  pl.BlockSpec((1, N_MOD, D), lambda i: ((i * tm) // tpc, 0, 0))],
        out_specs=pl.BlockSpec((tm, D), lambda i: (i, 0)),
        compiler_params=_cp(("parallel",)), name="matmul_res",
    )(a, w, res, mods)


def _dft_mats(t):
    def cs(n):
        k = np.arange(n)
        ang = 2.0 * np.pi * ((k[:, None] * k[None, :]) % n) / n
        s = 1.0 / math.sqrt(n)
        return np.cos(ang) * s, np.sin(ang) * s
    cc, sc = cs(FNET_GD)
    ct, st = cs(t)
    return (jnp.asarray(np.concatenate([cc, sc], axis=1), BF16), jnp.asarray(ct, BF16), jnp.asarray(st, BF16))


def _dft_kernel(u_ref, cs_ref, ct_ref, st_ref, o_ref):
    p = _dot(u_ref[...], cs_ref[...])
    pc = p[:, :FNET_GD].astype(BF16)
    ps = p[:, FNET_GD:].astype(BF16)
    o_ref[...] = (_dot(ct_ref[...], pc) - _dot(st_ref[...], ps)).astype(o_ref.dtype)


def fnet_dft(u, t):
    m = u.shape[0]
    cs, ct, st = _dft_mats(t)
    return pl.pallas_call(
        _dft_kernel,
        out_shape=jax.ShapeDtypeStruct((m, D), BF16),
        grid=(m // t, FNET_GROUPS),
        in_specs=[pl.BlockSpec((t, FNET_GD), lambda s, g: (s, g)),
                  pl.BlockSpec((FNET_GD, 2 * FNET_GD), lambda s, g: (0, 0)),
                  pl.BlockSpec((t, t), lambda s, g: (0, 0)),
                  pl.BlockSpec((t, t), lambda s, g: (0, 0))],
        out_specs=pl.BlockSpec((t, FNET_GD), lambda s, g: (s, g)),
        compiler_params=_cp(("parallel", "parallel")), name="fnet_dft",
    )(u, cs, ct, st)


def _head_ones():
    i = np.arange(D) // HEAD
    return jnp.asarray(i[:, None] == i[None, :], BF16)


def _rwkv_proj_kernel(x_ref, xp_ref, xn_ref, g_ref, m_ref, mu_ref, wr_ref, wk_ref, wv_ref, g1_ref, g2_ref,
                      w0_ref, w1_ref, w2_ref, a0_ref, a1_ref, a2_ref, kk_ref, ka_ref, rk_ref, ones_ref,
                      r_out, v_out, kkn_out, g_out, bonus_out, lw_out, kd_out, bd_out, *, tm, t):
    i = pl.program_id(0)
    shift = m_ref[0, 0:1, :]
    scale = m_ref[0, 1:2, :]
    g = g_ref[...]
    h = _modulate(x_ref[...], g, shift, scale)
    first = (i * tm) % t == 0
    last = ((i + 1) * tm) % t == 0
    hp = jnp.where(first, 0.0, _modulate(xp_ref[7:8, :], g, shift, scale))
    hn = jnp.where(last, 0.0, _modulate(xn_ref[0:1, :], g, shift, scale))
    row = lax.broadcasted_iota(jnp.int32, (tm, 1), 0)
    prev = jnp.where(row == 0, hp, pltpu.roll(h, 1, axis=0))
    nxt = jnp.where(row == tm - 1, hn, pltpu.roll(h, tm - 1, axis=0))
    xx = 0.5 * (prev + nxt) - h

    def mix(j):
        return (h + xx * mu_ref[j:j + 1, :]).astype(BF16)

    r = _dot(mix(0), wr_ref[...])
    k = _dot(mix(2), wk_ref[...])
    v = _dot(mix(3), wv_ref[...])
    gate = _dot(jax.nn.sigmoid(_dot(mix(5), g1_ref[...])).astype(BF16), g2_ref[...])
    xw = mix(1)
    xa = mix(4)
    ones = ones_ref[...]
    kk = k * kk_ref[...]
    kk = kk * lax.rsqrt(_split_dot(kk * kk, ones) + 1e-12)
    ksum = jnp.zeros_like(k)
    for j in range(2):
        w_raw = w0_ref[j:j + 1, :] + _dot(jnp.tanh(_dot(xw, w1_ref[j])).astype(BF16), w2_ref[j])
        lw_out[j] = -jnp.exp(-jax.nn.softplus(-w_raw) - 0.5)
        a = jax.nn.sigmoid(a0_ref[j:j + 1, :] + _dot(_dot(xa, a1_ref[j]).astype(BF16), a2_ref[j]))
        kd = k * (1 + (a - 1) * ka_ref[...])
        kd_out[j] = kd
        bd_out[j] = kk * a
        ksum = ksum + kd
    r_out[...] = r
    v_out[...] = v
    kkn_out[...] = kk
    g_out[...] = gate
    bonus_out[...] = _split_dot(r * ksum * rk_ref[...], ones) * v


def rwkv_proj(x, norm_g, mods, tpc, t, p, tm=256):
    m = x.shape[0]
    nb8 = m // 8
    full = lambda *shape: pl.BlockSpec(shape, lambda i: (0,) * len(shape))
    tok = pl.BlockSpec((tm, D), lambda i: (i, 0))
    tok2 = pl.BlockSpec((2, tm, D), lambda i: (0, i, 0))
    in_specs = [tok,
                pl.BlockSpec((8, D), lambda i: (jnp.maximum(i * (tm // 8) - 1, 0), 0)),
                pl.BlockSpec((8, D), lambda i: (jnp.minimum((i + 1) * (tm // 8), nb8 - 1), 0)),
                full(1, D),
                pl.BlockSpec((1, N_MOD, D), lambda i: ((i * tm) // tpc, 0, 0)),
                full(6, D), full(D, D), full(D, D), full(D, D), full(D, 128), full(128, D),
                full(2, D), full(2, D, 64), full(2, 64, D), full(2, D), full(2, D, 64), full(2, 64, D),
                full(1, D), full(1, D), full(1, D), full(D, D)]
    sd = jax.ShapeDtypeStruct
    return pl.pallas_call(
        functools.partial(_rwkv_proj_kernel, tm=tm, t=t),
        out_shape=(sd((m, D), F32),) * 5 + (sd((2, m, D), F32),) * 3,
        grid=(m // tm,), in_specs=in_specs,
        out_specs=(tok,) * 5 + (tok2,) * 3,
        compiler_params=_cp(("parallel",)), name="rwkv_proj",
    )(x, x, x, norm_g.reshape(1, D), mods, p["mu"], p["w_r"], p["w_k"], p["w_v"], p["g1"], p["g2"],
      p["w0"], p["w1"], p["w2"], p["a0"], p["a1"], p["a2"], p["k_k"], p["k_a"], p["r_k"], _head_ones())


def _rwkv_scan_kernel(r_ref, v_ref, kk_ref, lw_ref, kd_ref, b_ref, z0_ref, y_ref, zout_ref, z_scr, *, n_chunks):
    L = SCAN_L
    d = pl.program_id(1)
    c = pl.program_id(2)

    row = lax.broadcasted_iota(jnp.int32, (L, L), 0)
    col = lax.broadcasted_iota(jnp.int32, (L, L), 1)

    @pl.when(c == 0)
    def _():
        dup = (lax.broadcasted_iota(jnp.int32, (HEAD, L), 1) % HEAD
               == lax.broadcasted_iota(jnp.int32, (HEAD, L), 0)).astype(BF16)
        for p in range(N_HEADS // 2):
            both = jnp.concatenate([_split3_dot(z0_ref[0, 0, 2 * p], dup), _split3_dot(z0_ref[0, 0, 2 * p + 1], dup)],
                                   axis=0)
            z_scr[p] = jnp.where((row // HEAD) == (col // HEAD), both, 0.0)

    fwd = d == 0
    order = (col - row) * (1 - 2 * d)
    before = order < 0
    upto = order <= 0
    cum_mat = upto.astype(BF16)
    same_head = (row // HEAD) == (col // HEAD)
    lane = lax.broadcasted_iota(jnp.int32, (1, 2 * HEAD), 1)
    head_mask = (lane < HEAD, lane >= HEAD)
    n_double = int(math.log2(L))

    def prepare(p):
        sl = slice(p * 2 * HEAD, (p + 1) * 2 * HEAD)
        lw = lw_ref[0, :, sl]
        cum = _split_dot_left(cum_mat, lw)
        tot = jnp.where(fwd, cum[L - 1:L, :], cum[0:1, :])
        inv = jnp.exp(-cum)
        ar = jnp.concatenate([-kk_ref[:, sl] * jnp.exp(cum - lw), r_ref[:, sl] * jnp.exp(cum)], axis=0)
        bk = jnp.concatenate([b_ref[0, :, sl] * inv, kd_ref[0, :, sl] * inv], axis=0).astype(BF16)
        z = z_scr[p]
        base = _dot_nt(ar.astype(BF16), z.astype(BF16))
        return dict(ar=ar, bk=bk, z=z, base=base, v=v_ref[:, sl], tot=tot)

    def start_chain(pp, hm):
        g4 = _dot_nt(jnp.where(hm, pp["ar"], 0.0).astype(BF16), pp["bk"])
        vm = jnp.where(hm, pp["v"], 0.0).astype(BF16)
        n = jnp.where(before, g4[:L, :L], 0.0)
        x = jnp.where(hm, pp["base"][:L], 0.0) + _dot(jnp.where(before, g4[:L, L:], 0.0).astype(BF16), vm)
        out = jnp.concatenate([jnp.where(upto, g4[L:, :L], 0.0), jnp.where(upto, g4[L:, L:], 0.0)], axis=1)
        return dict(n=n, x=x, out=out.astype(BF16), vm=vm)

    pairs = [prepare(p) for p in range(N_HEADS // 2)]
    chains = [[start_chain(pp, hm) for hm in head_mask] for pp in pairs]
    for it in range(n_double):
        for ch in (ch for pair in chains for ch in pair):
            nb = ch["n"].astype(BF16)
            xb = ch["x"].astype(BF16)
            if it + 1 < n_double:
                res = _dot(nb, jnp.concatenate([xb, nb], axis=1))
                ch["x"] = ch["x"] + res[:, :2 * HEAD]
                ch["n"] = res[:, 2 * HEAD:]
            else:
                ch["x"] = ch["x"] + _dot(nb, xb)
    for p, (pp, pair) in enumerate(zip(pairs, chains)):
        y = pp["base"][L:]
        for ch in pair:
            y = y + _dot(ch["out"], jnp.concatenate([ch["x"].astype(BF16), ch["vm"]], axis=0))
        uv = jnp.concatenate([pair[0]["x"] + pair[1]["x"], pp["v"]], axis=0)
        inc = _dot(uv.T.astype(BF16), pp["bk"])
        z_scr[p] = jnp.where(same_head, pp["z"] + inc, 0.0) * jnp.exp(pp["tot"])
        y_ref[0, :, p * 2 * HEAD:(p + 1) * 2 * HEAD] = y

    @pl.when(c == n_chunks - 1)
    def _():
        prow = lax.broadcasted_iota(jnp.int32, (L, HEAD), 0)
        pcol = lax.broadcasted_iota(jnp.int32, (L, HEAD), 1)
        pick_a = (prow == pcol).astype(BF16)
        pick_b = (prow == pcol + HEAD).astype(BF16)
        for p in range(N_HEADS // 2):
            z = z_scr[p]
            zout_ref[0, 0, 2 * p] = _split3_dot(z[:HEAD, :], pick_a)
            zout_ref[0, 0, 2 * p + 1] = _split3_dot(z[HEAD:, :], pick_b)


def _split3_dot(x, w):
    hi = x.astype(BF16)
    r1 = x - hi.astype(F32)
    mid = r1.astype(BF16)
    lo = (r1 - mid.astype(F32)).astype(BF16)
    return _dot(hi, w) + _dot(mid, w) + _dot(lo, w)


def _split_dot_left(w, x):
    hi = x.astype(BF16)
    lo = (x - hi.astype(F32)).astype(BF16)
    return _dot(w, hi) + _dot(w, lo)


def rwkv_scan(r, v, kk, lw, kd, bd, z0, t):
    m = r.shape[0]
    n_seq = m // t
    nc = t // SCAN_L

    def blk(s, d, c):
        return s * nc + c + d * (nc - 1 - 2 * c)

    tok = pl.BlockSpec((SCAN_L, D), lambda s, d, c: (blk(s, d, c), 0))
    tok2 = pl.BlockSpec((1, SCAN_L, D), lambda s, d, c: (d, blk(s, d, c), 0))
    zspec = pl.BlockSpec((1, 1, N_HEADS, HEAD, HEAD), lambda s, d, c: (s, d, 0, 0, 0))
    return pl.pallas_call(
        functools.partial(_rwkv_scan_kernel, n_chunks=nc),
        out_shape=(jax.ShapeDtypeStruct((2, m, D), F32), jax.ShapeDtypeStruct(z0.shape, F32)),
        grid=(n_seq, 2, nc),
        in_specs=[tok, tok, tok, tok2, tok2, tok2, zspec],
        out_specs=(tok2, zspec),
        scratch_shapes=[pltpu.VMEM((N_HEADS // 2, 2 * HEAD, 2 * HEAD), F32)],
        compiler_params=_cp(("parallel", "parallel", "arbitrary")), name="rwkv_scan",
    )(r, v, kk, lw, kd, bd, z0)


def _rwkv_post_kernel(y_ref, bonus_ref, g_ref, lg_ref, lb_ref, ones_ref, o_ref):
    ones = ones_ref[...]
    o = y_ref[0] + y_ref[1]
    cen = o - _split_dot(o, ones) * (1.0 / HEAD)
    var = _split_dot(cen * cen, ones) * (1.0 / HEAD)
    o = cen * lax.rsqrt(var + LNX_EPS) * lg_ref[...] + lb_ref[...] + bonus_ref[...]
    o_ref[...] = (o * g_ref[...]).astype(o_ref.dtype)


def rwkv_post(y, bonus, g, lnx_g, lnx_b, tm=256):
    m = bonus.shape[0]
    tok = pl.BlockSpec((tm, D), lambda i: (i, 0))
    row = pl.BlockSpec((1, D), lambda i: (0, 0))
    return pl.pallas_call(
        _rwkv_post_kernel,
        out_shape=jax.ShapeDtypeStruct((m, D), BF16),
        grid=(m // tm,),
        in_specs=[pl.BlockSpec((2, tm, D), lambda i: (0, i, 0)), tok, tok, row, row,
                  pl.BlockSpec((D, D), lambda i: (0, 0))],
        out_specs=tok,
        compiler_params=_cp(("parallel",)), name="rwkv_post",
    )(y, bonus, g, lnx_g.reshape(1, D), lnx_b.reshape(1, D), _head_ones())


def rwkv_layer(x, norm_g, mods, tpc, t, s0, p):
    r, v, kk, g, bonus, lw, kd, bd = rwkv_proj(x, norm_g, mods, tpc, t, p)
    y, zf = rwkv_scan(r, v, kk, lw, kd, bd, s0, t)
    o = rwkv_post(y, bonus, g, p["lnx_g"], p["lnx_b"])
    return matmul_res(o, p["w_o"], x, mods, tpc, 2), zf


def _softmax_rows(parts):
    m = parts[0].max(axis=-1, keepdims=True)
    for s in parts[1:]:
        m = jnp.maximum(m, s.max(axis=-1, keepdims=True))
    es = [jnp.exp(s - m) for s in parts]
    den = es[0].sum(axis=-1, keepdims=True)
    for e in es[1:]:
        den = den + e.sum(axis=-1, keepdims=True)
    inv = 1.0 / den
    return [(e * inv).astype(BF16) for e in es]


def _qkv_kernel(a_ref, w_ref, *out_refs, with_f32):
    for idx in range(3):
        r = _dot(a_ref[...], w_ref[:, idx * D:(idx + 1) * D])
        out_refs[idx][...] = r.astype(BF16)
        if with_f32 and idx > 0:
            out_refs[2 + idx][...] = r


def qkv_proj(a, w, with_f32, tm=512):
    m = a.shape[0]
    tok = pl.BlockSpec((tm, D), lambda i: (i, 0))
    sd = jax.ShapeDtypeStruct
    n_f32 = 2 if with_f32 else 0
    return pl.pallas_call(
        functools.partial(_qkv_kernel, with_f32=with_f32),
        out_shape=(sd((m, D), BF16),) * 3 + (sd((m, D), F32),) * n_f32,
        grid=(m // tm,),
        in_specs=[tok, pl.BlockSpec((D, 3 * D), lambda i: (0, 0))],
        out_specs=(tok,) * (3 + n_f32),
        compiler_params=_cp(("parallel",)), name="qkv_proj",
    )(a, w)


def _pair_masks():
    lane = lax.broadcasted_iota(jnp.int32, (1, 2 * HEAD), 1)
    return lane < HEAD, lane >= HEAD


def _na_ctx_kernel(q_ref, k_ref, v_ref, o_ref):
    masks = _pair_masks()
    zero = jnp.zeros((), BF16)
    scores = []
    for p in range(N_HEADS // 2):
        sl = slice(p * 2 * HEAD, (p + 1) * 2 * HEAD)
        q = q_ref[:, sl]
        k = k_ref[:, sl]
        scores.append([_dot_nt(jnp.where(hm, q, zero), k) * NA_SCALE for hm in masks])
    probs = [[_softmax_rows([s])[0] for s in pair] for pair in scores]
    for p in range(N_HEADS // 2):
        sl = slice(p * 2 * HEAD, (p + 1) * 2 * HEAD)
        v = v_ref[:, sl]
        o_ref[:, sl] = jnp.where(masks[0], _dot(probs[p][0], v), _dot(probs[p][1], v)).astype(o_ref.dtype)


def na_ctx_attn(q, k, v, t):
    m = q.shape[0]
    seq = pl.BlockSpec((t, D), lambda b: (b, 0))
    return pl.pallas_call(
        _na_ctx_kernel,
        out_shape=jax.ShapeDtypeStruct((m, D), BF16),
        grid=(m // t,),
        in_specs=[seq, seq, seq],
        out_specs=seq,
        compiler_params=_cp(("parallel",)), name="na_ctx_attn",
    )(q, k, v)


def _win_start(r, rows):
    return jnp.clip(r - WIN_ROWS // 2, 0, rows - WIN_ROWS)


def _na_lat_kernel(q_ref, k_ref, v_ref, ck_ref, cv_ref, bias_ref, o_ref, *, rows):
    r = pl.program_id(1)
    start = pl.multiple_of(_win_start(r, rows) * GRID_W, GRID_W)
    n_loc = WIN_ROWS * GRID_W
    qc = lax.broadcasted_iota(jnp.int32, (GRID_W, n_loc), 0)
    kc = lax.broadcasted_iota(jnp.int32, (GRID_W, n_loc), 1) % GRID_W
    cs = jnp.clip(qc - WIN_COLS // 2, 0, GRID_W - WIN_COLS)
    valid = (kc >= cs) & (kc < cs + WIN_COLS)
    masks = _pair_masks()
    zero = jnp.zeros((), BF16)
    scores = []
    for p in range(N_HEADS // 2):
        sl = slice(p * 2 * HEAD, (p + 1) * 2 * HEAD)
        q = q_ref[:, sl]
        kw = k_ref[pl.ds(start, n_loc), sl]
        ck = ck_ref[:, sl]
        pair = []
        for i, hm in enumerate(masks):
            qm = jnp.where(hm, q, zero)
            s_loc = jnp.where(valid, _dot_nt(qm, kw) * NA_SCALE + bias_ref[0, 2 * p + i], NEG_INF)
            pair.append([s_loc, _dot_nt(qm, ck) * NA_SCALE])
        scores.append(pair)
    probs = [[_softmax_rows(parts) for parts in pair] for pair in scores]
    for p in range(N_HEADS // 2):
        sl = slice(p * 2 * HEAD, (p + 1) * 2 * HEAD)
        vw = v_ref[pl.ds(start, n_loc), sl]
        cv = cv_ref[:, sl]
        outs = [_dot(p_loc, vw) + _dot(p_ctx, cv) for p_loc, p_ctx in probs[p]]
        o_ref[:, sl] = jnp.where(masks[0], outs[0], outs[1]).astype(o_ref.dtype)


def _na_bias(rpb, rows):
    wr = min(WIN_ROWS, rows)
    qc = np.arange(GRID_W)
    col_off = np.clip(qc[None, :] - qc[:, None], -(WIN_COLS - 1), WIN_COLS - 1) + WIN_COLS - 1
    onehot = jnp.asarray(np.arange(2 * WIN_COLS - 1)[:, None, None] == col_off[None], F32)
    b = jnp.einsum('hrc,cqk->hrqk', rpb.astype(F32), onehot, precision=lax.Precision.HIGHEST)
    b = jnp.stack([b[:, ro0:ro0 + wr] for ro0 in range(WIN_ROWS)], axis=0)
    return jnp.transpose(b, (0, 1, 3, 2, 4)).reshape(WIN_ROWS, N_HEADS, GRID_W, wr * GRID_W)


def na_lat_attn(q, k, v, ck, cv, rpb, t, past):
    m = q.shape[0]
    rows = t // GRID_W
    assert rows >= WIN_ROWS
    n_seq = m // t
    n_loc = WIN_ROWS * GRID_W

    def bias_idx(b, r):
        return (_win_start(r, rows) - r + WIN_ROWS - 1, 0, 0, 0)

    return pl.pallas_call(
        functools.partial(_na_lat_kernel, rows=rows),
        out_shape=jax.ShapeDtypeStruct((m, D), BF16),
        grid=(n_seq, rows),
        in_specs=[pl.BlockSpec((GRID_W, D), lambda b, r: (b * rows + r, 0)),
                  pl.BlockSpec((t, D), lambda b, r: (b, 0)),
                  pl.BlockSpec((t, D), lambda b, r: (b, 0)),
                  pl.BlockSpec((past, D), lambda b, r: (b, 0)),
                  pl.BlockSpec((past, D), lambda b, r: (b, 0)),
                  pl.BlockSpec((1, N_HEADS, GRID_W, n_loc), bias_idx)],
        out_specs=pl.BlockSpec((GRID_W, D), lambda b, r: (b * rows + r, 0)),
        compiler_params=_cp(("parallel", "arbitrary")), name="na_lat_attn",
    )(q, k, v, ck, cv, _na_bias(rpb, rows))


def _sort16_net():
    def merge(lo, hi, r):
        step = r * 2
        if step < hi - lo:
            yield from merge(lo, hi, step)
            yield from merge(lo + r, hi, step)
            yield from ((i, i + r) for i in range(lo + r, hi - r, step))
        else:
            yield (lo, lo + r)

    def sort(lo, hi):
        if hi - lo >= 1:
            mid = lo + (hi - lo) // 2
            yield from sort(lo, mid)
            yield from sort(mid + 1, hi)
            yield from merge(lo, hi, 1)

    return tuple(sort(0, PEER_TOPK - 1))


_SORT16 = _sort16_net()
_BITONIC16 = tuple((i, i + d) for d in (8, 4, 2, 1) for i in range(PEER_TOPK) if not i & d)


def _exchange(x, net):
    x = list(x)
    for i, j in net:
        x[i], x[j] = jnp.maximum(x[i], x[j]), jnp.minimum(x[i], x[j])
    return x


def _merge_top16(a, b):
    return _exchange([jnp.maximum(a[i], b[PEER_TOPK - 1 - i]) for i in range(PEER_TOPK)], _BITONIC16)


def _merge_sublanes(x):
    for shift in (4, 2, 1):
        x = _merge_top16(x, [pltpu.roll(v, shift, axis=0) for v in x])
    return x


def _peer_stats_kernel(q_ref, keys_ref, cut_out, e1_out, tab_out):
    tm = q_ref.shape[0]
    sub = lax.broadcasted_iota(jnp.int32, (8, tm), 0)
    ninf = jnp.full((8, tm), -jnp.inf, F32)
    for h in range(PEER_HEADS):
        s, tops = [], []
        for c in range(2):
            hc = 2 * h + c
            q = q_ref[:, hc * PEER_KEYS:(hc + 1) * PEER_KEYS]
            sc = _dot_nt(keys_ref[hc], q)
            s.append(sc)
            groups = [sc[8 * i:8 * i + 8, :] for i in range(PEER_KEYS // 8)]
            tops.append(_merge_sublanes(_exchange(groups, _SORT16)))
        a1, a2 = tops
        a1col = a1[7]
        for jj in range(6, -1, -1):
            a1col = jnp.where(sub == jj, a1[jj], a1col)
        lists = [jnp.where(sub < min(8, PEER_TOPK // (k + 1)), a1col + a2[k], ninf) for k in range(PEER_TOPK)]
        tail = [a1[8 + k] + a2[0] for k in range(8)] + [ninf] * 8
        best = _merge_top16(_merge_sublanes(lists), tail)
        z = jnp.zeros((8, tm), F32)
        for b in best:
            z = z + jnp.exp(b - best[0])
        thr = best[PEER_TOPK - 1][0:1, :]
        m2 = a2[0][0:1, :]
        cut = jnp.full((PEER_KEYS, tm), jnp.inf, F32)
        for k in range(PEER_TOPK):
            a2k = a2[k][0:1, :]
            cut = jnp.where(s[0] + a2k >= thr, jnp.exp(a2k - m2), cut)
        cut_out[h] = cut
        e1_out[h] = jnp.exp(s[0] - a1[0][0:1, :]) * (0.5 / z[0:1, :])
        e2 = jnp.exp(s[1] - m2)
        for lb in range(tm // 128):
            lanes = slice(lb * 128, (lb + 1) * 128)
            tab_out[lb, h] = e2[:, lanes].reshape(PEER_KEYS // 8, 8, 128)


def peer_stats(q, keys, tm=256):
    m = q.shape[0]
    nq = 2 * PEER_HEADS * PEER_KEYS
    sd = jax.ShapeDtypeStruct
    big = pl.BlockSpec((PEER_HEADS, PEER_KEYS, tm), lambda i: (0, 0, i))
    tab_shape = (PEER_HEADS, PEER_KEYS // 8, 8, 128)
    return pl.pallas_call(
        _peer_stats_kernel,
        out_shape=(sd((PEER_HEADS, PEER_KEYS, m), F32),) * 2 + (sd((m // 128,) + tab_shape, F32),),
        grid=(m // tm,),
        in_specs=[pl.BlockSpec((tm, nq), lambda i: (i, 0)),
                  pl.BlockSpec((2 * PEER_HEADS, PEER_KEYS, PEER_KEYS), lambda i: (0, 0, 0))],
        out_specs=(big,) * 2 + (pl.BlockSpec((tm // 128,) + tab_shape, lambda i: (i, 0, 0, 0, 0)),),
        compiler_params=_cp(("parallel",)), name="peer_stats",
    )(q, keys)


def _gelu_x2(x):
    return x * (1.0 + lax.erf(x * (1.0 / math.sqrt(2.0))))


def _peer_dense_kernel(xt_ref, u_ref, un_ref, vt_ref, vp_ref, cut_ref, e1_ref, tab_ref, res_ref, m_ref, o_ref,
                       acc_scr, a_scr, w_scr, *, tm, tn, sub, gate):
    j = pl.program_id(1)
    n_sub = tn // sub
    n_cb = tm // 128
    out_rows = D // n_cb

    def act_matmul(sb):
        return _dot(u_ref[sb * sub:(sb + 1) * sub, :], xt_ref[...])

    @pl.when(j == 0)
    def _():
        acc_scr[...] = jnp.zeros_like(acc_scr)
        w_scr[...] = jnp.zeros_like(w_scr)
        a_scr[...] = act_matmul(0)

    def weights(sb, cb, a):
        lanes = slice(cb * 128, (cb + 1) * 128)
        n_il = sub // PEER_KEYS
        n_rg = PEER_KEYS // PEER_ROWS
        parts = [[None] * n_rg for _ in range(n_il)]
        for rg in range(n_rg):
            grp = slice(rg * PEER_ROWS // 8, (rg + 1) * PEER_ROWS // 8)
            g = [jnp.zeros((PEER_ROWS, 128), F32)] * n_il
            for h in range(PEER_HEADS):
                e2 = tab_ref[cb, h, grp].reshape(PEER_ROWS, 128)
                for il in range(n_il):
                    i1 = sb * n_il + il
                    hit = e2 >= cut_ref[h, i1:i1 + 1, lanes]
                    g[il] = g[il] + jnp.where(hit, e2 * e1_ref[h, i1:i1 + 1, lanes], 0.0)
            for il in range(n_il):
                r0 = il * PEER_KEYS + rg * PEER_ROWS
                parts[il][rg] = (g[il] * _gelu_x2(a[r0:r0 + PEER_ROWS, lanes])).astype(BF16)
        return jnp.concatenate([p for row in parts for p in row], axis=0)

    def out_piece(sb, q, w):
        rows = slice(q * out_rows, (q + 1) * out_rows)
        if sb < 0:
            return _dot(vp_ref[rows, :], w)
        return _dot(vt_ref[rows, sb * sub:(sb + 1) * sub], w)

    a = a_scr[...]
    w_prev = w_scr[...]
    contrib = [None] * n_cb
    for sb in range(n_sub):
        cols = []
        for cb in range(n_cb):
            if cb == 0:
                a_next = act_matmul(sb + 1) if sb + 1 < n_sub else _dot(un_ref[...], xt_ref[...])
            piece = out_piece(sb - 1, cb, w_prev)
            contrib[cb] = piece if contrib[cb] is None else contrib[cb] + piece
            cols.append(weights(sb, cb, a))
        w_prev = jnp.concatenate(cols, axis=1)
        a = a_next
    for q in range(n_cb):
        acc_scr[q * out_rows:(q + 1) * out_rows, :] += contrib[q]
    a_scr[...] = a
    w_scr[...] = w_prev

    @pl.when(j == pl.num_programs(1) - 1)
    def _():
        for q in range(n_cb):
            acc_scr[q * out_rows:(q + 1) * out_rows, :] += out_piece(n_sub - 1, q, w_scr[...])
        o_ref[...] = res_ref[...] + m_ref[0, gate:gate + 1, :] * acc_scr[...].T


def peer_dense(xt, u, vt, cut, e1, tab, res, mods, tpc, gate, tm=512, tn=1024, sub=256):
    m = xt.shape[1]
    n1 = tn // PEER_KEYS
    part = pl.BlockSpec((PEER_HEADS, n1, tm), lambda i, j: (0, j, i))
    full = pl.BlockSpec((tm // 128,) + tab.shape[1:], lambda i, j: (i, 0, 0, 0, 0))
    n_sub = tn // sub
    last_sub = N_EXPERTS // sub - 1
    return pl.pallas_call(
        functools.partial(_peer_dense_kernel, tm=tm, tn=tn, sub=sub, gate=gate),
        out_shape=jax.ShapeDtypeStruct((m, D), F32),
        grid=(m // tm, N_EXPERTS // tn),
        in_specs=[pl.BlockSpec((D, tm), lambda i, j: (0, i)),
                  pl.BlockSpec((tn, D), lambda i, j: (j, 0)),
                  pl.BlockSpec((sub, D), lambda i, j: (jnp.minimum((j + 1) * n_sub, last_sub), 0)),
                  pl.BlockSpec((D, tn), lambda i, j: (0, j)),
                  pl.BlockSpec((D, sub), lambda i, j: (0, jnp.maximum(j * n_sub - 1, 0))),
                  part, part, full,
                  pl.BlockSpec((tm, D), lambda i, j: (i, 0)),
                  pl.BlockSpec((1, N_MOD, D), lambda i, j: ((i * tm) // tpc, 0, 0))],
        out_specs=pl.BlockSpec((tm, D), lambda i, j: (i, 0)),
        scratch_shapes=[pltpu.VMEM((D, tm), F32), pltpu.VMEM((sub, tm), F32), pltpu.VMEM((sub, tm), BF16)],
        compiler_params=_cp(("parallel", "arbitrary")), name="peer_dense",
    )(xt, u, u, vt, vt, cut, e1, tab, res, mods)


def peer_layer(x, norm_g, mods, tpc, wq, keys, u, vt):
    h, ht = norm_mod(x, norm_g, mods, tpc, 3, transposed=True)
    q = matmul(h, wq, out_dtype=BF16)
    cut, e1, tab = peer_stats(q, keys)
    return peer_dense(ht, u, vt, cut, e1, tab, x, mods, tpc, 5)


def kernel(x_prompt, x_sample, c, state_wkv, cache_k, cache_v, c_ctx, ada_w, ada_b, norm_mix, norm_ffn, fnet_w_in, fnet_w_out, rwkv_mu, rwkv_w_r, rwkv_w_k, rwkv_w_v, rwkv_w_o, rwkv_w0, rwkv_w1, rwkv_w2, rwkv_a0, rwkv_a1, rwkv_a2, rwkv_g1, rwkv_g2, rwkv_k_k, rwkv_k_a, rwkv_r_k, rwkv_lnx_g, rwkv_lnx_b, na_w_qkv, na_w_o, na_rpb, peer_w_q, peer_sub_keys, peer_u, peer_v, final_norm):
    nb_c, t_c, _ = x_prompt.shape
    nb_s, t_s, _ = x_sample.shape
    depth = ada_w.shape[0]
    past = cache_k.shape[2]
    bf = lambda w: w.astype(BF16)

    cond = jnp.concatenate([c_ctx[None, :], c, jnp.zeros((16 - 1 - nb_s, D), F32)], axis=0)
    mods_all = ada_all(cond, ada_w, ada_b).reshape(depth, 16, N_MOD, D)

    xc = x_prompt.reshape(nb_c * t_c, D)
    xs = x_sample.reshape(nb_s * t_s, D)
    tpc_c, tpc_s = nb_c * t_c, t_s
    new_wkv, new_k, new_v = [], [], []

    for l in range(depth):
        kind, j = l % 3, l // 3
        mc = mods_all[l, 0:1]
        ms = mods_all[l, 1:1 + nb_s]
        if kind == 0:
            w_in, w_out = bf(fnet_w_in[j]), bf(fnet_w_out[j])
            outs = []
            for x, m, tpc, t in ((xc, mc, tpc_c, t_c), (xs, ms, tpc_s, t_s)):
                h = norm_mod(x, norm_mix[l], m, tpc, 0)
                f = fnet_dft(matmul(h, w_in, out_dtype=BF16), t)
                outs.append(matmul_res(f, w_out, x, m, tpc, 2))
            xc, xs = outs
        elif kind == 1:
            p = dict(mu=rwkv_mu[j], w_r=bf(rwkv_w_r[j]), w_k=bf(rwkv_w_k[j]), w_v=bf(rwkv_w_v[j]), w_o=bf(rwkv_w_o[j]),
                     w0=rwkv_w0[j], w1=bf(rwkv_w1[j]), w2=bf(rwkv_w2[j]), a0=rwkv_a0[j], a1=bf(rwkv_a1[j]),
                     a2=bf(rwkv_a2[j]), g1=bf(rwkv_g1[j]), g2=bf(rwkv_g2[j]), k_k=rwkv_k_k[j].reshape(1, D),
                     k_a=rwkv_k_a[j].reshape(1, D), r_k=rwkv_r_k[j].reshape(1, D), lnx_g=rwkv_lnx_g[j],
                     lnx_b=rwkv_lnx_b[j])
            s_zero = jnp.zeros((nb_c, 2, N_HEADS, HEAD, HEAD), F32)
            xc, sc = rwkv_layer(xc, norm_mix[l], mc, tpc_c, t_c, s_zero, p)
            xs, _ = rwkv_layer(xs, norm_mix[l], ms, tpc_s, t_s, state_wkv[:, j], p)
            new_wkv.append(sc)
        else:
            w_qkv, w_o = bf(na_w_qkv[j]), bf(na_w_o[j])
            qc, kc, vc, kc32, vc32 = qkv_proj(norm_mod(xc, norm_mix[l], mc, tpc_c, 0), w_qkv, True)
            qs, ks, vs = qkv_proj(norm_mod(xs, norm_mix[l], ms, tpc_s, 0), w_qkv, False)
            new_k.append(kc32.reshape(nb_c, t_c, N_HEADS, HEAD))
            new_v.append(vc32.reshape(nb_c, t_c, N_HEADS, HEAD))
            oc = na_ctx_attn(qc, kc, vc, t_c)
            os_ = na_lat_attn(qs, ks, vs, bf(cache_k[:, j]).reshape(nb_s * past, D),
                              bf(cache_v[:, j]).reshape(nb_s * past, D), na_rpb[j], t_s, past)
            xc = matmul_res(oc, w_o, xc, mc, tpc_c, 2)
            xs = matmul_res(os_, w_o, xs, ms, tpc_s, 2)
        wq = bf(peer_w_q[l])
        keys = bf(peer_sub_keys[l]).reshape(2 * PEER_HEADS, PEER_KEYS, PEER_KEYS)
        u = bf(peer_u[l])
        vt = bf(peer_v[l]).T
        xc = peer_layer(xc, norm_ffn[l], mc, tpc_c, wq, keys, u, vt)
        xs = peer_layer(xs, norm_ffn[l], ms, tpc_s, wq, keys, u, vt)

    y_prompt = rms_final(xc, final_norm).reshape(nb_c, t_c, D)
    y_sample = rms_final(xs, final_norm).reshape(nb_s, t_s, D)
    return (y_prompt, y_sample, jnp.stack(new_wkv, axis=1), jnp.stack(new_k, axis=1), jnp.stack(new_v, axis=1))
```

```python
import functools
import math

import numpy as np
import jax
import jax.numpy as jnp
from jax import lax
from jax.experimental import pallas as pl
from jax.experimental.pallas import tpu as pltpu

F32 = jnp.float32
BF16 = jnp.bfloat16

D = 1024
N_MOD = 6
EPS = 1e-6
HEAD = 64
N_HEADS = D // HEAD
LNX_EPS = 64e-5
GRID_W = 64
WIN_ROWS = 8
WIN_COLS = 16
NA_SCALE = HEAD ** -0.5
NEG_INF = -1e30
FNET_GROUPS = 4
FNET_GD = D // FNET_GROUPS
PEER_KEYS = 128
PEER_HEADS = 8
PEER_TOPK = 16
N_EXPERTS = PEER_KEYS * PEER_KEYS
PEER_ROWS = 32
SCAN_L = 128

VMEM_LIMIT = 56 * 1024 * 1024


def _cp(sem, vmem=VMEM_LIMIT):
    return pltpu.CompilerParams(dimension_semantics=sem, vmem_limit_bytes=vmem)


def _dot(a, b):
    return jnp.dot(a, b, preferred_element_type=F32)


def _dot_nt(a, b):
    return lax.dot_general(a, b, (((1,), (1,)), ((), ())), preferred_element_type=F32)


def _split_dot(x, w):
    hi = x.astype(BF16)
    lo = (x - hi.astype(F32)).astype(BF16)
    return _dot(hi, w) + _dot(lo, w)


def _ada_kernel(c_ref, w_ref, b_ref, o_ref):
    c = c_ref[...]
    s = c * jax.nn.sigmoid(c)
    o_ref[0] = _dot(s.astype(BF16), w_ref[0].astype(BF16)) + b_ref[0]


def ada_all(cond16, ada_w, ada_b):
    depth = ada_w.shape[0]
    tn = 1024
    return pl.pallas_call(
        _ada_kernel,
        out_shape=jax.ShapeDtypeStruct((depth, 16, N_MOD * D), F32),
        grid=(depth, N_MOD * D // tn),
        in_specs=[pl.BlockSpec((16, D), lambda l, j: (0, 0)),
                  pl.BlockSpec((1, D, tn), lambda l, j: (l, 0, j)),
                  pl.BlockSpec((1, 1, tn), lambda l, j: (l, 0, j))],
        out_specs=pl.BlockSpec((1, 16, tn), lambda l, j: (l, 0, j)),
        compiler_params=_cp(("parallel", "parallel")),
        name="ada",
    )(cond16, ada_w, ada_b.reshape(depth, 1, N_MOD * D))


def _modulate(x, g, shift, scale):
    y = x * lax.rsqrt(jnp.mean(x * x, axis=-1, keepdims=True) + EPS)
    return (y * g) * (1 + scale) + shift


def _norm_mod_kernel(x_ref, g_ref, m_ref, o_ref, *, which):
    h = _modulate(x_ref[...], g_ref[...], m_ref[0, which:which + 1, :], m_ref[0, which + 1:which + 2, :])
    o_ref[...] = h.astype(o_ref.dtype)


def _norm_mod_t_kernel(x_ref, g_ref, m_ref, o_ref, ot_ref, *, which):
    h = _modulate(x_ref[...], g_ref[...], m_ref[0, which:which + 1, :], m_ref[0, which + 1:which + 2, :])
    o_ref[...] = h.astype(o_ref.dtype)
    ot_ref[...] = h.T.astype(ot_ref.dtype)


def norm_mod(x, g, mods, tpc, which, transposed=False, tm=512):
    m = x.shape[0]
    in_specs = [pl.BlockSpec((tm, D), lambda i: (i, 0)),
                pl.BlockSpec((1, D), lambda i: (0, 0)),
                pl.BlockSpec((1, N_MOD, D), lambda i: ((i * tm) // tpc, 0, 0))]
    if not transposed:
        return pl.pallas_call(
            functools.partial(_norm_mod_kernel, which=which),
            out_shape=jax.ShapeDtypeStruct((m, D), BF16),
            grid=(m // tm,), in_specs=in_specs,
            out_specs=pl.BlockSpec((tm, D), lambda i: (i, 0)),
            compiler_params=_cp(("parallel",)), name="norm_mod",
        )(x, g.reshape(1, D), mods)
    return pl.pallas_call(
        functools.partial(_norm_mod_t_kernel, which=which),
        out_shape=(jax.ShapeDtypeStruct((m, D), BF16), jax.ShapeDtypeStruct((D, m), BF16)),
        grid=(m // tm,), in_specs=in_specs,
        out_specs=(pl.BlockSpec((tm, D), lambda i: (i, 0)), pl.BlockSpec((D, tm), lambda i: (0, i))),
        compiler_params=_cp(("parallel",)), name="norm_mod_t",
    )(x, g.reshape(1, D), mods)


def _final_norm_kernel(x_ref, g_ref, o_ref):
    x = x_ref[...]
    o_ref[...] = x * lax.rsqrt(jnp.mean(x * x, axis=-1, keepdims=True) + EPS) * g_ref[...]


def rms_final(x, g, tm=512):
    m = x.shape[0]
    return pl.pallas_call(
        _final_norm_kernel,
        out_shape=jax.ShapeDtypeStruct((m, D), F32),
        grid=(m // tm,),
        in_specs=[pl.BlockSpec((tm, D), lambda i: (i, 0)), pl.BlockSpec((1, D), lambda i: (0, 0))],
        out_specs=pl.BlockSpec((tm, D), lambda i: (i, 0)),
        compiler_params=_cp(("parallel",)), name="final_norm",
    )(x, g.reshape(1, D))


def _mm_kernel(a_ref, w_ref, o_ref):
    o_ref[...] = _dot(a_ref[...], w_ref[...]).astype(o_ref.dtype)


def _mm_res_kernel(a_ref, w_ref, res_ref, m_ref, o_ref, *, gate):
    o_ref[...] = res_ref[...] + m_ref[0, gate:gate + 1, :] * _dot(a_ref[...], w_ref[...])


def matmul(a, w, out_dtype=F32, tm=512):
    m, n = a.shape[0], w.shape[1]
    tn = n
    return pl.pallas_call(
        _mm_kernel,
        out_shape=jax.ShapeDtypeStruct((m, n), out_dtype),
        grid=(m // tm, n // tn),
        in_specs=[pl.BlockSpec((tm, D), lambda i, j: (i, 0)), pl.BlockSpec((D, tn), lambda i, j: (0, j))],
        out_specs=pl.BlockSpec((tm, tn), lambda i, j: (i, j)),
        compiler_params=_cp(("parallel", "parallel")), name="matmul",
    )(a, w)


def matmul_res(a, w, res, mods, tpc, gate, tm=512):
    m = a.shape[0]
    return pl.pallas_call(
        functools.partial(_mm_res_kernel, gate=gate),
        out_shape=jax.ShapeDtypeStruct((m, D), F32),
        grid=(m // tm,),
        in_specs=[pl.BlockSpec((tm, D), lambda i: (i, 0)),
                  pl.BlockSpec((D, D), lambda i: (0, 0)),
                  pl.BlockSpec((tm, D), lambda i: (i, 0)),
                  pl.BlockSpec((1, N_MOD, D), lambda i: ((i * tm) // tpc, 0, 0))],
        out_specs=pl.BlockSpec((tm, D), lambda i: (i, 0)),
        compiler_params=_cp(("parallel",)), name="matmul_res",
    )(a, w, res, mods)


def _dft_mats(t):
    def cs(n):
        k = np.arange(n)
        ang = 2.0 * np.pi * ((k[:, None] * k[None, :]) % n) / n
        s = 1.0 / math.sqrt(n)
        return np.cos(ang) * s, np.sin(ang) * s
    cc, sc = cs(FNET_GD)
    ct, st = cs(t)
    return (jnp.asarray(np.concatenate([cc, sc], axis=1), BF16), jnp.asarray(ct, BF16), jnp.asarray(st, BF16))


def _dft_kernel(u_ref, cs_ref, ct_ref, st_ref, o_ref):
    p = _dot(u_ref[...], cs_ref[...])
    pc = p[:, :FNET_GD].astype(BF16)
    ps = p[:, FNET_GD:].astype(BF16)
    o_ref[...] = (_dot(ct_ref[...], pc) - _dot(st_ref[...], ps)).astype(o_ref.dtype)


def fnet_dft(u, t):
    m = u.shape[0]
    cs, ct, st = _dft_mats(t)
    return pl.pallas_call(
        _dft_kernel,
        out_shape=jax.ShapeDtypeStruct((m, D), BF16),
        grid=(m // t, FNET_GROUPS),
        in_specs=[pl.BlockSpec((t, FNET_GD), lambda s, g: (s, g)),
                  pl.BlockSpec((FNET_GD, 2 * FNET_GD), lambda s, g: (0, 0)),
                  pl.BlockSpec((t, t), lambda s, g: (0, 0)),
                  pl.BlockSpec((t, t), lambda s, g: (0, 0))],
        out_specs=pl.BlockSpec((t, FNET_GD), lambda s, g: (s, g)),
        compiler_params=_cp(("parallel", "parallel")), name="fnet_dft",
    )(u, cs, ct, st)


def _head_select():
    return jnp.asarray((np.arange(D) // HEAD)[:, None] == np.arange(128)[None, :], BF16)


def _head_sum(x, sel):
    c = _split_dot(x, sel)
    hi = c.astype(BF16)
    lo = (c - hi.astype(F32)).astype(BF16)
    return _dot_nt(hi, sel) + _dot_nt(lo, sel)


def _rwkv_proj_kernel(x_ref, xp_ref, xn_ref, g_ref, m_ref, mu_ref, wr_ref, wk_ref, wv_ref, g1_ref, g2_ref,
                      w0_ref, w1_ref, w2_ref, a0_ref, a1_ref, a2_ref, kk_ref, ka_ref, rk_ref, sel_ref,
                      r_out, v_out, kkn_out, g_out, bonus_out, lw_out, kd_out, bd_out, *, tm, t):
    i = pl.program_id(0)
    shift = m_ref[0, 0:1, :]
    scale = m_ref[0, 1:2, :]
    g = g_ref[...]
    h = _modulate(x_ref[...], g, shift, scale)
    first = (i * tm) % t == 0
    last = ((i + 1) * tm) % t == 0
    hp = jnp.where(first, 0.0, _modulate(xp_ref[7:8, :], g, shift, scale))
    hn = jnp.where(last, 0.0, _modulate(xn_ref[0:1, :], g, shift, scale))
    row = lax.broadcasted_iota(jnp.int32, (tm, 1), 0)
    prev = jnp.where(row == 0, hp, pltpu.roll(h, 1, axis=0))
    nxt = jnp.where(row == tm - 1, hn, pltpu.roll(h, tm - 1, axis=0))
    xx = 0.5 * (prev + nxt) - h

    def mix(j):
        return (h + xx * mu_ref[j:j + 1, :]).astype(BF16)

    r = _dot(mix(0), wr_ref[...])
    k = _dot(mix(2), wk_ref[...])
    v = _dot(mix(3), wv_ref[...])
    gate = _dot(jax.nn.sigmoid(_dot(mix(5), g1_ref[...])).astype(BF16), g2_ref[...])
    xw = mix(1)
    xa = mix(4)
    sel = sel_ref[...]
    kk = k * kk_ref[...]
    kk = kk * lax.rsqrt(_head_sum(kk * kk, sel) + 1e-12)
    ksum = jnp.zeros_like(k)
    for j in range(2):
        w_raw = w0_ref[j:j + 1, :] + _dot(jnp.tanh(_dot(xw, w1_ref[j])).astype(BF16), w2_ref[j])
        lw_out[j] = -jnp.exp(-jax.nn.softplus(-w_raw) - 0.5)
        a = jax.nn.sigmoid(a0_ref[j:j + 1, :] + _dot(_dot(xa, a1_ref[j]).astype(BF16), a2_ref[j]))
        kd = k * (1 + (a - 1) * ka_ref[...])
        kd_out[j] = kd
        bd_out[j] = kk * a
        ksum = ksum + kd
    r_out[...] = r
    v_out[...] = v
    kkn_out[...] = kk
    g_out[...] = gate
    bonus_out[...] = _head_sum(r * ksum * rk_ref[...], sel) * v


def rwkv_proj(x, norm_g, mods, tpc, t, p, tm=256):
    m = x.shape[0]
    nb8 = m // 8
    full = lambda *shape: pl.BlockSpec(shape, lambda i: (0,) * len(shape))
    tok = pl.BlockSpec((tm, D), lambda i: (i, 0))
    tok2 = pl.BlockSpec((2, tm, D), lambda i: (0, i, 0))
    in_specs = [tok,
                pl.BlockSpec((8, D), lambda i: (jnp.maximum(i * (tm // 8) - 1, 0), 0)),
                pl.BlockSpec((8, D), lambda i: (jnp.minimum((i + 1) * (tm // 8), nb8 - 1), 0)),
                full(1, D),
                pl.BlockSpec((1, N_MOD, D), lambda i: ((i * tm) // tpc, 0, 0)),
                full(6, D), full(D, D), full(D, D), full(D, D), full(D, 128), full(128, D),
                full(2, D), full(2, D, 64), full(2, 64, D), full(2, D), full(2, D, 64), full(2, 64, D),
                full(1, D), full(1, D), full(1, D), full(D, 128)]
    sd = jax.ShapeDtypeStruct
    return pl.pallas_call(
        functools.partial(_rwkv_proj_kernel, tm=tm, t=t),
        out_shape=(sd((m, D), F32),) * 5 + (sd((2, m, D), F32),) * 3,
        grid=(m // tm,), in_specs=in_specs,
        out_specs=(tok,) * 5 + (tok2,) * 3,
        compiler_params=_cp(("parallel",)), name="rwkv_proj",
    )(x, x, x, norm_g.reshape(1, D), mods, p["mu"], p["w_r"], p["w_k"], p["w_v"], p["g1"], p["g2"],
      p["w0"], p["w1"], p["w2"], p["a0"], p["a1"], p["a2"], p["k_k"], p["k_a"], p["r_k"], _head_select())


def _rwkv_scan_kernel(r_ref, v_ref, kk_ref, lw_ref, kd_ref, b_ref, z0_ref, y_ref, zout_ref, z_scr, *, n_chunks):
    L = SCAN_L
    d = pl.program_id(1)
    c = pl.program_id(2)

    row = lax.broadcasted_iota(jnp.int32, (L, L), 0)
    col = lax.broadcasted_iota(jnp.int32, (L, L), 1)

    @pl.when(c == 0)
    def _():
        dup = (lax.broadcasted_iota(jnp.int32, (HEAD, L), 1) % HEAD
               == lax.broadcasted_iota(jnp.int32, (HEAD, L), 0)).astype(BF16)
        for p in range(N_HEADS // 2):
            both = jnp.concatenate([_split3_dot(z0_ref[0, 0, 2 * p], dup), _split3_dot(z0_ref[0, 0, 2 * p + 1], dup)],
                                   axis=0)
            z_scr[p] = jnp.where((row // HEAD) == (col // HEAD), both, 0.0)

    fwd = d == 0
    order = (col - row) * (1 - 2 * d)
    before = order < 0
    upto = order <= 0
    cum_mat = upto.astype(BF16)
    same_head = (row // HEAD) == (col // HEAD)
    lane = lax.broadcasted_iota(jnp.int32, (1, 2 * HEAD), 1)
    head_mask = (lane < HEAD, lane >= HEAD)
    n_double = int(math.log2(L))

    def prepare(p):
        sl = slice(p * 2 * HEAD, (p + 1) * 2 * HEAD)
        lw = lw_ref[0, :, sl]
        cum = _split_dot_left(cum_mat, lw)
        tot = jnp.where(fwd, cum[L - 1:L, :], cum[0:1, :])
        inv = jnp.exp(-cum)
        ar = jnp.concatenate([-kk_ref[:, sl] * jnp.exp(cum - lw), r_ref[:, sl] * jnp.exp(cum)], axis=0)
        bk = jnp.concatenate([b_ref[0, :, sl] * inv, kd_ref[0, :, sl] * inv], axis=0).astype(BF16)
        z = z_scr[p]
        base = _dot_nt(ar.astype(BF16), z.astype(BF16))
        return dict(ar=ar, bk=bk, z=z, base=base, v=v_ref[:, sl], tot=tot)

    def start_chain(pp, hm):
        g4 = _dot_nt(jnp.where(hm, pp["ar"], 0.0).astype(BF16), pp["bk"])
        vm = jnp.where(hm, pp["v"], 0.0).astype(BF16)
        n = jnp.where(before, g4[:L, :L], 0.0)
        x = jnp.where(hm, pp["base"][:L], 0.0) + _dot(jnp.where(before, g4[:L, L:], 0.0).astype(BF16), vm)
        out = jnp.concatenate([jnp.where(upto, g4[L:, :L], 0.0), jnp.where(upto, g4[L:, L:], 0.0)], axis=1)
        return dict(n=n, x=x, out=out.astype(BF16), vm=vm)

    pairs = [prepare(p) for p in range(N_HEADS // 2)]
    chains = [[start_chain(pp, hm) for hm in head_mask] for pp in pairs]
    for it in range(n_double):
        for ch in (ch for pair in chains for ch in pair):
            nb = ch["n"].astype(BF16)
            xb = ch["x"].astype(BF16)
            if it + 1 < n_double:
                res = _dot(nb, jnp.concatenate([xb, nb], axis=1))
                ch["x"] = ch["x"] + res[:, :2 * HEAD]
                ch["n"] = res[:, 2 * HEAD:]
            else:
                ch["x"] = ch["x"] + _dot(nb, xb)
    for p, (pp, pair) in enumerate(zip(pairs, chains)):
        y = pp["base"][L:]
        for ch in pair:
            y = y + _dot(ch["out"], jnp.concatenate([ch["x"].astype(BF16), ch["vm"]], axis=0))
        uv = jnp.concatenate([pair[0]["x"] + pair[1]["x"], pp["v"]], axis=0)
        inc = _dot(uv.T.astype(BF16), pp["bk"])
        z_scr[p] = jnp.where(same_head, pp["z"] + inc, 0.0) * jnp.exp(pp["tot"])
        y_ref[0, :, p * 2 * HEAD:(p + 1) * 2 * HEAD] = y

    @pl.when(c == n_chunks - 1)
    def _():
        prow = lax.broadcasted_iota(jnp.int32, (L, HEAD), 0)
        pcol = lax.broadcasted_iota(jnp.int32, (L, HEAD), 1)
        pick_a = (prow == pcol).astype(BF16)
        pick_b = (prow == pcol + HEAD).astype(BF16)
        for p in range(N_HEADS // 2):
            z = z_scr[p]
            zout_ref[0, 0, 2 * p] = _split3_dot(z[:HEAD, :], pick_a)
            zout_ref[0, 0, 2 * p + 1] = _split3_dot(z[HEAD:, :], pick_b)


def _split3_dot(x, w):
    hi = x.astype(BF16)
    r1 = x - hi.astype(F32)
    mid = r1.astype(BF16)
    lo = (r1 - mid.astype(F32)).astype(BF16)
    return _dot(hi, w) + _dot(mid, w) + _dot(lo, w)


def _split_dot_left(w, x):
    hi = x.astype(BF16)
    lo = (x - hi.astype(F32)).astype(BF16)
    return _dot(w, hi) + _dot(w, lo)


def rwkv_scan(r, v, kk, lw, kd, bd, z0, t):
    m = r.shape[0]
    n_seq = m // t
    nc = t // SCAN_L

    def blk(s, d, c):
        return s * nc + c + d * (nc - 1 - 2 * c)

    tok = pl.BlockSpec((SCAN_L, D), lambda s, d, c: (blk(s, d, c), 0))
    tok2 = pl.BlockSpec((1, SCAN_L, D), lambda s, d, c: (d, blk(s, d, c), 0))
    zspec = pl.BlockSpec((1, 1, N_HEADS, HEAD, HEAD), lambda s, d, c: (s, d, 0, 0, 0))
    return pl.pallas_call(
        functools.partial(_rwkv_scan_kernel, n_chunks=nc),
        out_shape=(jax.ShapeDtypeStruct((2, m, D), F32), jax.ShapeDtypeStruct(z0.shape, F32)),
        grid=(n_seq, 2, nc),
        in_specs=[tok, tok, tok, tok2, tok2, tok2, zspec],
        out_specs=(tok2, zspec),
        scratch_shapes=[pltpu.VMEM((N_HEADS // 2, 2 * HEAD, 2 * HEAD), F32)],
        compiler_params=_cp(("parallel", "parallel", "arbitrary")), name="rwkv_scan",
    )(r, v, kk, lw, kd, bd, z0)


def _rwkv_post_kernel(y_ref, bonus_ref, g_ref, lg_ref, lb_ref, sel_ref, o_ref):
    sel = sel_ref[...]
    o = y_ref[0] + y_ref[1]
    cen = o - _head_sum(o, sel) * (1.0 / HEAD)
    var = _head_sum(cen * cen, sel) * (1.0 / HEAD)
    o = cen * lax.rsqrt(var + LNX_EPS) * lg_ref[...] + lb_ref[...] + bonus_ref[...]
    o_ref[...] = (o * g_ref[...]).astype(o_ref.dtype)


def rwkv_post(y, bonus, g, lnx_g, lnx_b, tm=256):
    m = bonus.shape[0]
    tok = pl.BlockSpec((tm, D), lambda i: (i, 0))
    row = pl.BlockSpec((1, D), lambda i: (0, 0))
    return pl.pallas_call(
        _rwkv_post_kernel,
        out_shape=jax.ShapeDtypeStruct((m, D), BF16),
        grid=(m // tm,),
        in_specs=[pl.BlockSpec((2, tm, D), lambda i: (0, i, 0)), tok, tok, row, row,
                  pl.BlockSpec((D, 128), lambda i: (0, 0))],
        out_specs=tok,
        compiler_params=_cp(("parallel",)), name="rwkv_post",
    )(y, bonus, g, lnx_g.reshape(1, D), lnx_b.reshape(1, D), _head_select())


def rwkv_layer(x, norm_g, mods, tpc, t, s0, p):
    r, v, kk, g, bonus, lw, kd, bd = rwkv_proj(x, norm_g, mods, tpc, t, p)
    y, zf = rwkv_scan(r, v, kk, lw, kd, bd, s0, t)
    o = rwkv_post(y, bonus, g, p["lnx_g"], p["lnx_b"])
    return matmul_res(o, p["w_o"], x, mods, tpc, 2), zf


def _softmax_rows(parts):
    m = parts[0].max(axis=-1, keepdims=True)
    for s in parts[1:]:
        m = jnp.maximum(m, s.max(axis=-1, keepdims=True))
    es = [jnp.exp(s - m) for s in parts]
    den = es[0].sum(axis=-1, keepdims=True)
    for e in es[1:]:
        den = den + e.sum(axis=-1, keepdims=True)
    inv = 1.0 / den
    return [(e * inv).astype(BF16) for e in es]


def _qkv_kernel(a_ref, w_ref, *out_refs, with_f32):
    for idx in range(3):
        r = _dot(a_ref[...], w_ref[:, idx * D:(idx + 1) * D])
        out_refs[idx][...] = r.astype(BF16)
        if with_f32 and idx > 0:
            out_refs[2 + idx][...] = r


def qkv_proj(a, w, with_f32, tm=512):
    m = a.shape[0]
    tok = pl.BlockSpec((tm, D), lambda i: (i, 0))
    sd = jax.ShapeDtypeStruct
    n_f32 = 2 if with_f32 else 0
    return pl.pallas_call(
        functools.partial(_qkv_kernel, with_f32=with_f32),
        out_shape=(sd((m, D), BF16),) * 3 + (sd((m, D), F32),) * n_f32,
        grid=(m // tm,),
        in_specs=[tok, pl.BlockSpec((D, 3 * D), lambda i: (0, 0))],
        out_specs=(tok,) * (3 + n_f32),
        compiler_params=_cp(("parallel",)), name="qkv_proj",
    )(a, w)


def _pair_masks():
    lane = lax.broadcasted_iota(jnp.int32, (1, 2 * HEAD), 1)
    return lane < HEAD, lane >= HEAD


def _na_ctx_kernel(q_ref, k_ref, v_ref, o_ref):
    masks = _pair_masks()
    zero = jnp.zeros((), BF16)
    scores = []
    for p in range(N_HEADS // 2):
        sl = slice(p * 2 * HEAD, (p + 1) * 2 * HEAD)
        q = q_ref[:, sl]
        k = k_ref[:, sl]
        scores.append([_dot_nt(jnp.where(hm, q, zero), k) * NA_SCALE for hm in masks])
    probs = [[_softmax_rows([s])[0] for s in pair] for pair in scores]
    for p in range(N_HEADS // 2):
        sl = slice(p * 2 * HEAD, (p + 1) * 2 * HEAD)
        v = v_ref[:, sl]
        o_ref[:, sl] = jnp.where(masks[0], _dot(probs[p][0], v), _dot(probs[p][1], v)).astype(o_ref.dtype)


def na_ctx_attn(q, k, v, t):
    m = q.shape[0]
    seq = pl.BlockSpec((t, D), lambda b: (b, 0))
    return pl.pallas_call(
        _na_ctx_kernel,
        out_shape=jax.ShapeDtypeStruct((m, D), BF16),
        grid=(m // t,),
        in_specs=[seq, seq, seq],
        out_specs=seq,
        compiler_params=_cp(("parallel",)), name="na_ctx_attn",
    )(q, k, v)


def _win_start(r, rows):
    return jnp.clip(r - WIN_ROWS // 2, 0, rows - WIN_ROWS)


def _na_lat_kernel(q_ref, k_ref, v_ref, ck_ref, cv_ref, bias_ref, o_ref, *, rows):
    r = pl.program_id(1)
    start = pl.multiple_of(_win_start(r, rows) * GRID_W, GRID_W)
    n_loc = WIN_ROWS * GRID_W
    qc = lax.broadcasted_iota(jnp.int32, (GRID_W, n_loc), 0)
    kc = lax.broadcasted_iota(jnp.int32, (GRID_W, n_loc), 1) % GRID_W
    cs = jnp.clip(qc - WIN_COLS // 2, 0, GRID_W - WIN_COLS)
    valid = (kc >= cs) & (kc < cs + WIN_COLS)
    masks = _pair_masks()
    zero = jnp.zeros((), BF16)
    scores = []
    for p in range(N_HEADS // 2):
        sl = slice(p * 2 * HEAD, (p + 1) * 2 * HEAD)
        q = q_ref[:, sl]
        kw = k_ref[pl.ds(start, n_loc), sl]
        ck = ck_ref[:, sl]
        pair = []
        for i, hm in enumerate(masks):
            qm = jnp.where(hm, q, zero)
            s_loc = jnp.where(valid, _dot_nt(qm, kw) * NA_SCALE + bias_ref[0, 2 * p + i], NEG_INF)
            pair.append([s_loc, _dot_nt(qm, ck) * NA_SCALE])
        scores.append(pair)
    probs = [[_softmax_rows(parts) for parts in pair] for pair in scores]
    for p in range(N_HEADS // 2):
        sl = slice(p * 2 * HEAD, (p + 1) * 2 * HEAD)
        vw = v_ref[pl.ds(start, n_loc), sl]
        cv = cv_ref[:, sl]
        outs = [_dot(p_loc, vw) + _dot(p_ctx, cv) for p_loc, p_ctx in probs[p]]
        o_ref[:, sl] = jnp.where(masks[0], outs[0], outs[1]).astype(o_ref.dtype)


def _na_bias(rpb, rows):
    wr = min(WIN_ROWS, rows)
    qc = np.arange(GRID_W)
    col_off = np.clip(qc[None, :] - qc[:, None], -(WIN_COLS - 1), WIN_COLS - 1) + WIN_COLS - 1
    onehot = jnp.asarray(np.arange(2 * WIN_COLS - 1)[:, None, None] == col_off[None], F32)
    b = jnp.einsum('hrc,cqk->hrqk', rpb.astype(F32), onehot, precision=lax.Precision.HIGHEST)
    b = jnp.stack([b[:, ro0:ro0 + wr] for ro0 in range(WIN_ROWS)], axis=0)
    return jnp.transpose(b, (0, 1, 3, 2, 4)).reshape(WIN_ROWS, N_HEADS, GRID_W, wr * GRID_W)


def na_lat_attn(q, k, v, ck, cv, rpb, t, past):
    m = q.shape[0]
    rows = t // GRID_W
    assert rows >= WIN_ROWS
    n_seq = m // t
    n_loc = WIN_ROWS * GRID_W

    def bias_idx(b, r):
        return (_win_start(r, rows) - r + WIN_ROWS - 1, 0, 0, 0)

    return pl.pallas_call(
        functools.partial(_na_lat_kernel, rows=rows),
        out_shape=jax.ShapeDtypeStruct((m, D), BF16),
        grid=(n_seq, rows),
        in_specs=[pl.BlockSpec((GRID_W, D), lambda b, r: (b * rows + r, 0)),
                  pl.BlockSpec((t, D), lambda b, r: (b, 0)),
                  pl.BlockSpec((t, D), lambda b, r: (b, 0)),
                  pl.BlockSpec((past, D), lambda b, r: (b, 0)),
                  pl.BlockSpec((past, D), lambda b, r: (b, 0)),
                  pl.BlockSpec((1, N_HEADS, GRID_W, n_loc), bias_idx)],
        out_specs=pl.BlockSpec((GRID_W, D), lambda b, r: (b * rows + r, 0)),
        compiler_params=_cp(("parallel", "arbitrary")), name="na_lat_attn",
    )(q, k, v, ck, cv, _na_bias(rpb, rows))


def _sort16_net():
    def merge(lo, hi, r):
        step = r * 2
        if step < hi - lo:
            yield from merge(lo, hi, step)
            yield from merge(lo + r, hi, step)
            yield from ((i, i + r) for i in range(lo + r, hi - r, step))
        else:
            yield (lo, lo + r)

    def sort(lo, hi):
        if hi - lo >= 1:
            mid = lo + (hi - lo) // 2
            yield from sort(lo, mid)
            yield from sort(mid + 1, hi)
            yield from merge(lo, hi, 1)

    return tuple(sort(0, PEER_TOPK - 1))


_SORT16 = _sort16_net()
_BITONIC16 = tuple((i, i + d) for d in (8, 4, 2, 1) for i in range(PEER_TOPK) if not i & d)


def _exchange(x, net):
    x = list(x)
    for i, j in net:
        x[i], x[j] = jnp.maximum(x[i], x[j]), jnp.minimum(x[i], x[j])
    return x


def _merge_top16(a, b):
    return _exchange([jnp.maximum(a[i], b[PEER_TOPK - 1 - i]) for i in range(PEER_TOPK)], _BITONIC16)


def _merge_sublanes(x):
    for shift in (4, 2, 1):
        x = _merge_top16(x, [pltpu.roll(v, shift, axis=0) for v in x])
    return x


def _peer_stats_kernel(q_ref, keys_ref, cut_out, e1_out, tab_out):
    tm = q_ref.shape[0]
    sub = lax.broadcasted_iota(jnp.int32, (8, tm), 0)
    ninf = jnp.full((8, tm), -jnp.inf, F32)
    for h in range(PEER_HEADS):
        s, tops = [], []
        for c in range(2):
            hc = 2 * h + c
            q = q_ref[:, hc * PEER_KEYS:(hc + 1) * PEER_KEYS]
            sc = _dot_nt(keys_ref[hc], q)
            s.append(sc)
            groups = [sc[8 * i:8 * i + 8, :] for i in range(PEER_KEYS // 8)]
            tops.append(_merge_sublanes(_exchange(groups, _SORT16)))
        a1, a2 = tops
        a1col = a1[7]
        for jj in range(6, -1, -1):
            a1col = jnp.where(sub == jj, a1[jj], a1col)
        lists = [jnp.where(sub < min(8, PEER_TOPK // (k + 1)), a1col + a2[k], ninf) for k in range(PEER_TOPK)]
        tail = [a1[8 + k] + a2[0] for k in range(8)] + [ninf] * 8
        best = _merge_top16(_merge_sublanes(lists), tail)
        z = jnp.zeros((8, tm), F32)
        for b in best:
            z = z + jnp.exp(b - best[0])
        thr = best[PEER_TOPK - 1][0:1, :]
        m2 = a2[0][0:1, :]
        cut = jnp.full((PEER_KEYS, tm), jnp.inf, F32)
        for k in range(PEER_TOPK):
            a2k = a2[k][0:1, :]
            cut = jnp.where(s[0] + a2k >= thr, jnp.exp(a2k - m2), cut)
        cut_out[h] = cut
        e1_out[h] = jnp.exp(s[0] - a1[0][0:1, :]) * (0.5 / z[0:1, :])
        e2 = jnp.exp(s[1] - m2)
        for lb in range(tm // 128):
            lanes = slice(lb * 128, (lb + 1) * 128)
            tab_out[lb, h] = e2[:, lanes].reshape(PEER_KEYS // 8, 8, 128)


def peer_stats(q, keys, tm=256):
    m = q.shape[0]
    nq = 2 * PEER_HEADS * PEER_KEYS
    sd = jax.ShapeDtypeStruct
    big = pl.BlockSpec((PEER_HEADS, PEER_KEYS, tm), lambda i: (0, 0, i))
    tab_shape = (PEER_HEADS, PEER_KEYS // 8, 8, 128)
    return pl.pallas_call(
        _peer_stats_kernel,
        out_shape=(sd((PEER_HEADS, PEER_KEYS, m), F32),) * 2 + (sd((m // 128,) + tab_shape, F32),),
        grid=(m // tm,),
        in_specs=[pl.BlockSpec((tm, nq), lambda i: (i, 0)),
                  pl.BlockSpec((2 * PEER_HEADS, PEER_KEYS, PEER_KEYS), lambda i: (0, 0, 0))],
        out_specs=(big,) * 2 + (pl.BlockSpec((tm // 128,) + tab_shape, lambda i: (i, 0, 0, 0, 0)),),
        compiler_params=_cp(("parallel",)), name="peer_stats",
    )(q, keys)


def _gelu_x2(x):
    return x * (1.0 + lax.erf(x * (1.0 / math.sqrt(2.0))))


def _peer_dense_kernel(xt_ref, u_ref, un_ref, vt_ref, vp_ref, cut_ref, e1_ref, tab_ref, res_ref, m_ref, o_ref,
                       acc_scr, a_scr, w_scr, *, tm, tn, sub, gate):
    j = pl.program_id(1)
    n_sub = tn // sub
    n_cb = tm // 128
    out_rows = D // n_cb

    def act_matmul(sb):
        return _dot(u_ref[sb * sub:(sb + 1) * sub, :], xt_ref[...])

    @pl.when(j == 0)
    def _():
        acc_scr[...] = jnp.zeros_like(acc_scr)
        w_scr[...] = jnp.zeros_like(w_scr)
        a_scr[...] = act_matmul(0)

    def weights(sb, cb, a):
        lanes = slice(cb * 128, (cb + 1) * 128)
        n_il = sub // PEER_KEYS
        n_rg = PEER_KEYS // PEER_ROWS
        parts = [[None] * n_rg for _ in range(n_il)]
        for rg in range(n_rg):
            grp = slice(rg * PEER_ROWS // 8, (rg + 1) * PEER_ROWS // 8)
            g = [jnp.zeros((PEER_ROWS, 128), F32)] * n_il
            for h in range(PEER_HEADS):
                e2 = tab_ref[cb, h, grp].reshape(PEER_ROWS, 128)
                for il in range(n_il):
                    i1 = sb * n_il + il
                    hit = e2 >= cut_ref[h, i1:i1 + 1, lanes]
                    g[il] = g[il] + jnp.where(hit, e2 * e1_ref[h, i1:i1 + 1, lanes], 0.0)
            for il in range(n_il):
                r0 = il * PEER_KEYS + rg * PEER_ROWS
                parts[il][rg] = (g[il] * _gelu_x2(a[r0:r0 + PEER_ROWS, lanes])).astype(BF16)
        return jnp.concatenate([p for row in parts for p in row], axis=0)

    def out_piece(sb, q, w):
        rows = slice(q * out_rows, (q + 1) * out_rows)
        if sb < 0:
            return _dot(vp_ref[rows, :], w)
        return _dot(vt_ref[rows, sb * sub:(sb + 1) * sub], w)

    a = a_scr[...]
    w_prev = w_scr[...]
    contrib = [None] * n_cb
    for sb in range(n_sub):
        cols = []
        for cb in range(n_cb):
            if cb == 0:
                a_next = act_matmul(sb + 1) if sb + 1 < n_sub else _dot(un_ref[...], xt_ref[...])
            piece = out_piece(sb - 1, cb, w_prev)
            contrib[cb] = piece if contrib[cb] is None else contrib[cb] + piece
            cols.append(weights(sb, cb, a))
        w_prev = jnp.concatenate(cols, axis=1)
        a = a_next
    for q in range(n_cb):
        acc_scr[q * out_rows:(q + 1) * out_rows, :] += contrib[q]
    a_scr[...] = a
    w_scr[...] = w_prev

    @pl.when(j == pl.num_programs(1) - 1)
    def _():
        for q in range(n_cb):
            acc_scr[q * out_rows:(q + 1) * out_rows, :] += out_piece(n_sub - 1, q, w_scr[...])
        o_ref[...] = res_ref[...] + m_ref[0, gate:gate + 1, :] * acc_scr[...].T


def peer_dense(xt, u, vt, layer, cut, e1, tab, res, mods, tpc, gate, tm=512, tn=1024, sub=256):
    m = xt.shape[1]
    n1 = tn // PEER_KEYS
    part = pl.BlockSpec((PEER_HEADS, n1, tm), lambda i, j: (0, j, i))
    full = pl.BlockSpec((tm // 128,) + tab.shape[1:], lambda i, j: (i, 0, 0, 0, 0))
    n_sub = tn // sub
    last_sub = N_EXPERTS // sub - 1
    return pl.pallas_call(
        functools.partial(_peer_dense_kernel, tm=tm, tn=tn, sub=sub, gate=gate),
        out_shape=jax.ShapeDtypeStruct((m, D), F32),
        grid=(m // tm, N_EXPERTS // tn),
        in_specs=[pl.BlockSpec((D, tm), lambda i, j: (0, i)),
                  pl.BlockSpec((None, tn, D), lambda i, j: (layer, j, 0)),
                  pl.BlockSpec((None, sub, D), lambda i, j: (layer, jnp.minimum((j + 1) * n_sub, last_sub), 0)),
                  pl.BlockSpec((None, D, tn), lambda i, j: (layer, 0, j)),
                  pl.BlockSpec((None, D, sub), lambda i, j: (layer, 0, jnp.maximum(j * n_sub - 1, 0))),
                  part, part, full,
                  pl.BlockSpec((tm, D), lambda i, j: (i, 0)),
                  pl.BlockSpec((1, N_MOD, D), lambda i, j: ((i * tm) // tpc, 0, 0))],
        out_specs=pl.BlockSpec((tm, D), lambda i, j: (i, 0)),
        scratch_shapes=[pltpu.VMEM((D, tm), F32), pltpu.VMEM((sub, tm), F32), pltpu.VMEM((sub, tm), BF16)],
        compiler_params=_cp(("parallel", "arbitrary")), name="peer_dense",
    )(xt, u, u, vt, vt, cut, e1, tab, res, mods)


def peer_layer(x, norm_g, mods, tpc, wq, keys, u, vt, layer):
    h, ht = norm_mod(x, norm_g, mods, tpc, 3, transposed=True)
    q = matmul(h, wq, out_dtype=BF16)
    cut, e1, tab = peer_stats(q, keys)
    return peer_dense(ht, u, vt, layer, cut, e1, tab, x, mods, tpc, 5)


def kernel(x_prompt, x_sample, c, state_wkv, cache_k, cache_v, c_ctx, ada_w, ada_b, norm_mix, norm_ffn, fnet_w_in, fnet_w_out, rwkv_mu, rwkv_w_r, rwkv_w_k, rwkv_w_v, rwkv_w_o, rwkv_w0, rwkv_w1, rwkv_w2, rwkv_a0, rwkv_a1, rwkv_a2, rwkv_g1, rwkv_g2, rwkv_k_k, rwkv_k_a, rwkv_r_k, rwkv_lnx_g, rwkv_lnx_b, na_w_qkv, na_w_o, na_rpb, peer_w_q, peer_sub_keys, peer_u, peer_v, final_norm):
    nb_c, t_c, _ = x_prompt.shape
    nb_s, t_s, _ = x_sample.shape
    depth = ada_w.shape[0]
    past = cache_k.shape[2]
    bf = lambda w: w.astype(BF16)

    cond = jnp.concatenate([c_ctx[None, :], c, jnp.zeros((16 - 1 - nb_s, D), F32)], axis=0)
    mods_all = ada_all(cond, ada_w, ada_b).reshape(depth, 16, N_MOD, D)

    xc = x_prompt.reshape(nb_c * t_c, D)
    xs = x_sample.reshape(nb_s * t_s, D)
    tpc_c, tpc_s = nb_c * t_c, t_s
    new_wkv, new_k, new_v = [], [], []
    u_all = bf(peer_u)
    vt_all = jnp.swapaxes(bf(peer_v), 1, 2)

    for l in range(depth):
        kind, j = l % 3, l // 3
        mc = mods_all[l, 0:1]
        ms = mods_all[l, 1:1 + nb_s]
        if kind == 0:
            w_in, w_out = bf(fnet_w_in[j]), bf(fnet_w_out[j])
            outs = []
            for x, m, tpc, t in ((xc, mc, tpc_c, t_c), (xs, ms, tpc_s, t_s)):
                h = norm_mod(x, norm_mix[l], m, tpc, 0)
                f = fnet_dft(matmul(h, w_in, out_dtype=BF16), t)
                outs.append(matmul_res(f, w_out, x, m, tpc, 2))
            xc, xs = outs
        elif kind == 1:
            p = dict(mu=rwkv_mu[j], w_r=bf(rwkv_w_r[j]), w_k=bf(rwkv_w_k[j]), w_v=bf(rwkv_w_v[j]), w_o=bf(rwkv_w_o[j]),
                     w0=rwkv_w0[j], w1=bf(rwkv_w1[j]), w2=bf(rwkv_w2[j]), a0=rwkv_a0[j], a1=bf(rwkv_a1[j]),
                     a2=bf(rwkv_a2[j]), g1=bf(rwkv_g1[j]), g2=bf(rwkv_g2[j]), k_k=rwkv_k_k[j].reshape(1, D),
                     k_a=rwkv_k_a[j].reshape(1, D), r_k=rwkv_r_k[j].reshape(1, D), lnx_g=rwkv_lnx_g[j],
                     lnx_b=rwkv_lnx_b[j])
            s_zero = jnp.zeros((nb_c, 2, N_HEADS, HEAD, HEAD), F32)
            xc, sc = rwkv_layer(xc, norm_mix[l], mc, tpc_c, t_c, s_zero, p)
            xs, _ = rwkv_layer(xs, norm_mix[l], ms, tpc_s, t_s, state_wkv[:, j], p)
            new_wkv.append(sc)
        else:
            w_qkv, w_o = bf(na_w_qkv[j]), bf(na_w_o[j])
            qc, kc, vc, kc32, vc32 = qkv_proj(norm_mod(xc, norm_mix[l], mc, tpc_c, 0), w_qkv, True)
            qs, ks, vs = qkv_proj(norm_mod(xs, norm_mix[l], ms, tpc_s, 0), w_qkv, False)
            new_k.append(kc32.reshape(nb_c, t_c, N_HEADS, HEAD))
            new_v.append(vc32.reshape(nb_c, t_c, N_HEADS, HEAD))
            oc = na_ctx_attn(qc, kc, vc, t_c)
            os_ = na_lat_attn(qs, ks, vs, bf(cache_k[:, j]).reshape(nb_s * past, D),
                              bf(cache_v[:, j]).reshape(nb_s * past, D), na_rpb[j], t_s, past)
            xc = matmul_res(oc, w_o, xc, mc, tpc_c, 2)
            xs = matmul_res(os_, w_o, xs, ms, tpc_s, 2)
        wq = bf(peer_w_q[l])
        keys = bf(peer_sub_keys[l]).reshape(2 * PEER_HEADS, PEER_KEYS, PEER_KEYS)
        xc = peer_layer(xc, norm_ffn[l], mc, tpc_c, wq, keys, u_all, vt_all, l)
        xs = peer_layer(xs, norm_ffn[l], ms, tpc_s, wq, keys, u_all, vt_all, l)

    y_prompt = rms_final(xc, final_norm).reshape(nb_c, t_c, D)
    y_sample = rms_final(xs, final_norm).reshape(nb_s, t_s, D)
    return (y_prompt, y_sample, jnp.stack(new_wkv, axis=1), jnp.stack(new_k, axis=1), jnp.stack(new_v, axis=1))
```

```python
import functools
import math

import numpy as np
import jax
import jax.numpy as jnp
from jax import lax
from jax.experimental import pallas as pl
from jax.experimental.pallas import tpu as pltpu

F32 = jnp.float32
BF16 = jnp.bfloat16

D = 1024
N_MOD = 6
EPS = 1e-6
HEAD = 64
N_HEADS = D // HEAD
LNX_EPS = 64e-5
GRID_W = 64
WIN_ROWS = 8
WIN_COLS = 16
NA_SCALE = HEAD ** -0.5
NEG_INF = -1e30
FNET_GROUPS = 4
FNET_GD = D // FNET_GROUPS
PEER_KEYS = 128
PEER_HEADS = 8
PEER_TOPK = 16
N_EXPERTS = PEER_KEYS * PEER_KEYS
PEER_ROWS = 32
SCAN_L = 128

VMEM_LIMIT = 56 * 1024 * 1024


def _cp(sem, vmem=VMEM_LIMIT):
    return pltpu.CompilerParams(dimension_semantics=sem, vmem_limit_bytes=vmem)


def _dot(a, b):
    return jnp.dot(a, b, preferred_element_type=F32)


def _dot_nt(a, b):
    return lax.dot_general(a, b, (((1,), (1,)), ((), ())), preferred_element_type=F32)


def _split_dot(x, w):
    hi = x.astype(BF16)
    lo = (x - hi.astype(F32)).astype(BF16)
    return _dot(hi, w) + _dot(lo, w)


def _ada_kernel(c_ref, w_ref, b_ref, o_ref):
    c = c_ref[...]
    s = c * jax.nn.sigmoid(c)
    o_ref[0] = _dot(s.astype(BF16), w_ref[0].astype(BF16)) + b_ref[0]


def ada_all(cond16, ada_w, ada_b):
    depth = ada_w.shape[0]
    tn = 1024
    return pl.pallas_call(
        _ada_kernel,
        out_shape=jax.ShapeDtypeStruct((depth, 16, N_MOD * D), F32),
        grid=(depth, N_MOD * D // tn),
        in_specs=[pl.BlockSpec((16, D), lambda l, j: (0, 0)),
                  pl.BlockSpec((1, D, tn), lambda l, j: (l, 0, j)),
                  pl.BlockSpec((1, 1, tn), lambda l, j: (l, 0, j))],
        out_specs=pl.BlockSpec((1, 16, tn), lambda l, j: (l, 0, j)),
        compiler_params=_cp(("parallel", "parallel")),
        name="ada",
    )(cond16, ada_w, ada_b.reshape(depth, 1, N_MOD * D))


def _modulate(x, g, shift, scale):
    y = x * lax.rsqrt(jnp.mean(x * x, axis=-1, keepdims=True) + EPS)
    return (y * g) * (1 + scale) + shift


def _norm_mod_q_kernel(x_ref, g_ref, m_ref, w_ref, q_ref, ht_ref, *, which):
    h = _modulate(x_ref[...], g_ref[...], m_ref[0, which:which + 1, :], m_ref[0, which + 1:which + 2, :])
    q_ref[...] = _dot(h.astype(BF16), w_ref[...]).astype(q_ref.dtype)
    ht_ref[...] = h.T.astype(ht_ref.dtype)


def _norm_mod_specs(tm, tpc):
    return [pl.BlockSpec((tm, D), lambda i: (i, 0)),
            pl.BlockSpec((1, D), lambda i: (0, 0)),
            pl.BlockSpec((1, N_MOD, D), lambda i: ((i * tm) // tpc, 0, 0))]


def _norm_mod_mm_kernel(x_ref, g_ref, m_ref, w_ref, o_ref, *, which):
    h = _modulate(x_ref[...], g_ref[...], m_ref[0, which:which + 1, :], m_ref[0, which + 1:which + 2, :])
    o_ref[...] = _dot(h.astype(BF16), w_ref[...]).astype(o_ref.dtype)


def norm_mod_mm(x, g, mods, tpc, which, w, tm=512):
    m, n = x.shape[0], w.shape[1]
    return pl.pallas_call(
        functools.partial(_norm_mod_mm_kernel, which=which),
        out_shape=jax.ShapeDtypeStruct((m, n), BF16),
        grid=(m // tm,), in_specs=_norm_mod_specs(tm, tpc) + [pl.BlockSpec((D, n), lambda i: (0, 0))],
        out_specs=pl.BlockSpec((tm, n), lambda i: (i, 0)),
        compiler_params=_cp(("parallel",)), name="norm_mod_mm",
    )(x, g.reshape(1, D), mods, w)


def norm_mod_q(x, g, mods, tpc, which, w, tm=512):
    m, n = x.shape[0], w.shape[1]
    return pl.pallas_call(
        functools.partial(_norm_mod_q_kernel, which=which),
        out_shape=(jax.ShapeDtypeStruct((m, n), BF16), jax.ShapeDtypeStruct((D, m), BF16)),
        grid=(m // tm,), in_specs=_norm_mod_specs(tm, tpc) + [pl.BlockSpec((D, n), lambda i: (0, 0))],
        out_specs=(pl.BlockSpec((tm, n), lambda i: (i, 0)), pl.BlockSpec((D, tm), lambda i: (0, i))),
        compiler_params=_cp(("parallel",)), name="norm_mod_q",
    )(x, g.reshape(1, D), mods, w)


def _final_norm_kernel(x_ref, g_ref, o_ref):
    x = x_ref[...]
    o_ref[...] = x * lax.rsqrt(jnp.mean(x * x, axis=-1, keepdims=True) + EPS) * g_ref[...]


def rms_final(x, g, tm=512):
    m = x.shape[0]
    return pl.pallas_call(
        _final_norm_kernel,
        out_shape=jax.ShapeDtypeStruct((m, D), F32),
        grid=(m // tm,),
        in_specs=[pl.BlockSpec((tm, D), lambda i: (i, 0)), pl.BlockSpec((1, D), lambda i: (0, 0))],
        out_specs=pl.BlockSpec((tm, D), lambda i: (i, 0)),
        compiler_params=_cp(("parallel",)), name="final_norm",
    )(x, g.reshape(1, D))


def _mm_res_kernel(a_ref, w_ref, res_ref, m_ref, o_ref, *, gate):
    o_ref[...] = res_ref[...] + m_ref[0, gate:gate + 1, :] * _dot(a_ref[...], w_ref[...])


def matmul_res(a, w, res, mods, tpc, gate, tm=512):
    m = a.shape[0]
    return pl.pallas_call(
        functools.partial(_mm_res_kernel, gate=gate),
        out_shape=jax.ShapeDtypeStruct((m, D), F32),
        grid=(m // tm,),
        in_specs=[pl.BlockSpec((tm, D), lambda i: (i, 0)),
                  pl.BlockSpec((D, D), lambda i: (0, 0)),
                  pl.BlockSpec((tm, D), lambda i: (i, 0)),
                  pl.BlockSpec((1, N_MOD, D), lambda i: ((i * tm) // tpc, 0, 0))],
        out_specs=pl.BlockSpec((tm, D), lambda i: (i, 0)),
        compiler_params=_cp(("parallel",)), name="matmul_res",
    )(a, w, res, mods)


def _dft_mats(t):
    def cs(n):
        k = np.arange(n)
        ang = 2.0 * np.pi * ((k[:, None] * k[None, :]) % n) / n
        s = 1.0 / math.sqrt(n)
        return np.cos(ang) * s, np.sin(ang) * s
    cc, sc = cs(FNET_GD)
    ct, st = cs(t)
    return (jnp.asarray(np.concatenate([cc, sc], axis=1), BF16), jnp.asarray(ct, BF16), jnp.asarray(st, BF16))


def _dft_kernel(u_ref, cs_ref, ct_ref, st_ref, o_ref):
    p = _dot(u_ref[...], cs_ref[...])
    pc = p[:, :FNET_GD].astype(BF16)
    ps = p[:, FNET_GD:].astype(BF16)
    o_ref[...] = (_dot(ct_ref[...], pc) - _dot(st_ref[...], ps)).astype(o_ref.dtype)


def fnet_dft(u, t):
    m = u.shape[0]
    cs, ct, st = _dft_mats(t)
    return pl.pallas_call(
        _dft_kernel,
        out_shape=jax.ShapeDtypeStruct((m, D), BF16),
        grid=(m // t, FNET_GROUPS),
        in_specs=[pl.BlockSpec((t, FNET_GD), lambda s, g: (s, g)),
                  pl.BlockSpec((FNET_GD, 2 * FNET_GD), lambda s, g: (0, 0)),
                  pl.BlockSpec((t, t), lambda s, g: (0, 0)),
                  pl.BlockSpec((t, t), lambda s, g: (0, 0))],
        out_specs=pl.BlockSpec((t, FNET_GD), lambda s, g: (s, g)),
        compiler_params=_cp(("parallel", "parallel")), name="fnet_dft",
    )(u, cs, ct, st)


def _head_select():
    return jnp.asarray((np.arange(D) // HEAD)[:, None] == np.arange(128)[None, :], BF16)


def _head_sum(x, sel):
    c = _split_dot(x, sel)
    hi = c.astype(BF16)
    lo = (c - hi.astype(F32)).astype(BF16)
    return _dot_nt(hi, sel) + _dot_nt(lo, sel)


def _rwkv_proj_kernel(x_ref, xp_ref, xn_ref, g_ref, m_ref, mu_ref, wr_ref, wk_ref, wv_ref, g1_ref, g2_ref,
                      w0_ref, w1_ref, w2_ref, a0_ref, a1_ref, a2_ref, kk_ref, ka_ref, rk_ref, sel_ref,
                      r_out, v_out, kkn_out, g_out, bonus_out, lw_out, kd_out, bd_out, *, tm, t):
    i = pl.program_id(0)
    shift = m_ref[0, 0:1, :]
    scale = m_ref[0, 1:2, :]
    g = g_ref[...]
    h = _modulate(x_ref[...], g, shift, scale)
    first = (i * tm) % t == 0
    last = ((i + 1) * tm) % t == 0
    hp = jnp.where(first, 0.0, _modulate(xp_ref[7:8, :], g, shift, scale))
    hn = jnp.where(last, 0.0, _modulate(xn_ref[0:1, :], g, shift, scale))
    row = lax.broadcasted_iota(jnp.int32, (tm, 1), 0)
    prev = jnp.where(row == 0, hp, pltpu.roll(h, 1, axis=0))
    nxt = jnp.where(row == tm - 1, hn, pltpu.roll(h, tm - 1, axis=0))
    xx = 0.5 * (prev + nxt) - h

    def mix(j):
        return (h + xx * mu_ref[j:j + 1, :]).astype(BF16)

    r = _dot(mix(0), wr_ref[...])
    k = _dot(mix(2), wk_ref[...])
    v = _dot(mix(3), wv_ref[...])
    gate = _dot(jax.nn.sigmoid(_dot(mix(5), g1_ref[...])).astype(BF16), g2_ref[...])
    xw = mix(1)
    xa = mix(4)
    sel = sel_ref[...]
    kk = k * kk_ref[...]
    kk = kk * lax.rsqrt(_head_sum(kk * kk, sel) + 1e-12)
    ksum = jnp.zeros_like(k)
    for j in range(2):
        w_raw = w0_ref[j:j + 1, :] + _dot(jnp.tanh(_dot(xw, w1_ref[j])).astype(BF16), w2_ref[j])
        lw_out[j] = -jnp.exp(-jax.nn.softplus(-w_raw) - 0.5)
        a = jax.nn.sigmoid(a0_ref[j:j + 1, :] + _dot(_dot(xa, a1_ref[j]).astype(BF16), a2_ref[j]))
        kd = k * (1 + (a - 1) * ka_ref[...])
        kd_out[j] = kd
        bd_out[j] = kk * a
        ksum = ksum + kd
    r_out[...] = r
    v_out[...] = v
    kkn_out[...] = kk
    g_out[...] = gate
    bonus_out[...] = _head_sum(r * ksum * rk_ref[...], sel) * v


def rwkv_proj(x, norm_g, mods, tpc, t, p, tm=256):
    m = x.shape[0]
    nb8 = m // 8
    full = lambda *shape: pl.BlockSpec(shape, lambda i: (0,) * len(shape))
    tok = pl.BlockSpec((tm, D), lambda i: (i, 0))
    tok2 = pl.BlockSpec((2, tm, D), lambda i: (0, i, 0))
    in_specs = [tok,
                pl.BlockSpec((8, D), lambda i: (jnp.maximum(i * (tm // 8) - 1, 0), 0)),
                pl.BlockSpec((8, D), lambda i: (jnp.minimum((i + 1) * (tm // 8), nb8 - 1), 0)),
                full(1, D),
                pl.BlockSpec((1, N_MOD, D), lambda i: ((i * tm) // tpc, 0, 0)),
                full(6, D), full(D, D), full(D, D), full(D, D), full(D, 128), full(128, D),
                full(2, D), full(2, D, 64), full(2, 64, D), full(2, D), full(2, D, 64), full(2, 64, D),
                full(1, D), full(1, D), full(1, D), full(D, 128)]
    sd = jax.ShapeDtypeStruct
    return pl.pallas_call(
        functools.partial(_rwkv_proj_kernel, tm=tm, t=t),
        out_shape=(sd((m, D), F32),) * 5 + (sd((2, m, D), F32),) * 3,
        grid=(m // tm,), in_specs=in_specs,
        out_specs=(tok,) * 5 + (tok2,) * 3,
        compiler_params=_cp(("parallel",)), name="rwkv_proj",
    )(x, x, x, norm_g.reshape(1, D), mods, p["mu"], p["w_r"], p["w_k"], p["w_v"], p["g1"], p["g2"],
      p["w0"], p["w1"], p["w2"], p["a0"], p["a1"], p["a2"], p["k_k"], p["k_a"], p["r_k"], _head_select())


def _rwkv_scan_kernel(r_ref, v_ref, kk_ref, lw_ref, kd_ref, b_ref, z0_ref, y_ref, zout_ref, z_scr, *, n_chunks):
    L = SCAN_L
    d = pl.program_id(1)
    c = pl.program_id(2)

    row = lax.broadcasted_iota(jnp.int32, (L, L), 0)
    col = lax.broadcasted_iota(jnp.int32, (L, L), 1)

    @pl.when(c == 0)
    def _():
        dup = (lax.broadcasted_iota(jnp.int32, (HEAD, L), 1) % HEAD
               == lax.broadcasted_iota(jnp.int32, (HEAD, L), 0)).astype(BF16)
        for p in range(N_HEADS // 2):
            both = jnp.concatenate([_split3_dot(z0_ref[0, 0, 2 * p], dup), _split3_dot(z0_ref[0, 0, 2 * p + 1], dup)],
                                   axis=0)
            z_scr[p] = jnp.where((row // HEAD) == (col // HEAD), both, 0.0)

    fwd = d == 0
    order = (col - row) * (1 - 2 * d)
    before = order < 0
    upto = order <= 0
    cum_mat = upto.astype(BF16)
    same_head = (row // HEAD) == (col // HEAD)
    lane = lax.broadcasted_iota(jnp.int32, (1, 2 * HEAD), 1)
    head_mask = (lane < HEAD, lane >= HEAD)
    n_double = int(math.log2(L))

    def prepare(p):
        sl = slice(p * 2 * HEAD, (p + 1) * 2 * HEAD)
        lw = lw_ref[0, :, sl]
        cum = _split_dot_left(cum_mat, lw)
        tot = jnp.where(fwd, cum[L - 1:L, :], cum[0:1, :])
        inv = jnp.exp(-cum)
        ar = jnp.concatenate([-kk_ref[:, sl] * jnp.exp(cum - lw), r_ref[:, sl] * jnp.exp(cum)], axis=0)
        bk = jnp.concatenate([b_ref[0, :, sl] * inv, kd_ref[0, :, sl] * inv], axis=0).astype(BF16)
        z = z_scr[p]
        base = _dot_nt(ar.astype(BF16), z.astype(BF16))
        return dict(ar=ar, bk=bk, z=z, base=base, v=v_ref[:, sl], tot=tot)

    def start_chain(pp, hm):
        g4 = _dot_nt(jnp.where(hm, pp["ar"], 0.0).astype(BF16), pp["bk"])
        vm = jnp.where(hm, pp["v"], 0.0).astype(BF16)
        n = jnp.where(before, g4[:L, :L], 0.0).astype(BF16)
        x = jnp.where(hm, pp["base"][:L], 0.0) + _dot(jnp.where(before, g4[:L, L:], 0.0).astype(BF16), vm)
        out = jnp.concatenate([jnp.where(upto, g4[L:, :L], 0.0), jnp.where(upto, g4[L:, L:], 0.0)], axis=1)
        return dict(n=n, x=x, out=out.astype(BF16), vm=vm)

    pairs = [prepare(p) for p in range(N_HEADS // 2)]
    chains = [[start_chain(pp, hm) for hm in head_mask] for pp in pairs]
    for it in range(n_double):
        for ch in (ch for pair in chains for ch in pair):
            nb = ch["n"]
            xb = ch["x"].astype(BF16)
            if it + 1 < n_double:
                res = _dot(nb, jnp.concatenate([xb, nb], axis=1))
                ch["x"] = ch["x"] + res[:, :2 * HEAD]
                ch["n"] = res[:, 2 * HEAD:].astype(BF16)
            else:
                ch["x"] = ch["x"] + _dot(nb, xb)
    for p, (pp, pair) in enumerate(zip(pairs, chains)):
        y = pp["base"][L:]
        for ch in pair:
            y = y + _dot(ch["out"], jnp.concatenate([ch["x"].astype(BF16), ch["vm"]], axis=0))
        uv = jnp.concatenate([pair[0]["x"] + pair[1]["x"], pp["v"]], axis=0)
        inc = _dot(uv.T.astype(BF16), pp["bk"])
        z_scr[p] = jnp.where(same_head, pp["z"] + inc, 0.0) * jnp.exp(pp["tot"])
        y_ref[0, :, p * 2 * HEAD:(p + 1) * 2 * HEAD] = y

    @pl.when(c == n_chunks - 1)
    def _():
        prow = lax.broadcasted_iota(jnp.int32, (L, HEAD), 0)
        pcol = lax.broadcasted_iota(jnp.int32, (L, HEAD), 1)
        pick_a = (prow == pcol).astype(BF16)
        pick_b = (prow == pcol + HEAD).astype(BF16)
        for p in range(N_HEADS // 2):
            z = z_scr[p]
            zout_ref[0, 0, 2 * p] = _split3_dot(z[:HEAD, :], pick_a)
            zout_ref[0, 0, 2 * p + 1] = _split3_dot(z[HEAD:, :], pick_b)


def _split3_dot(x, w):
    hi = x.astype(BF16)
    r1 = x - hi.astype(F32)
    mid = r1.astype(BF16)
    lo = (r1 - mid.astype(F32)).astype(BF16)
    return _dot(hi, w) + _dot(mid, w) + _dot(lo, w)


def _split_dot_left(w, x):
    hi = x.astype(BF16)
    lo = (x - hi.astype(F32)).astype(BF16)
    return _dot(w, hi) + _dot(w, lo)


def rwkv_scan(r, v, kk, lw, kd, bd, z0, t):
    m = r.shape[0]
    n_seq = m // t
    nc = t // SCAN_L

    def blk(s, d, c):
        return s * nc + c + d * (nc - 1 - 2 * c)

    tok = pl.BlockSpec((SCAN_L, D), lambda s, d, c: (blk(s, d, c), 0))
    tok2 = pl.BlockSpec((1, SCAN_L, D), lambda s, d, c: (d, blk(s, d, c), 0))
    zspec = pl.BlockSpec((1, 1, N_HEADS, HEAD, HEAD), lambda s, d, c: (s, d, 0, 0, 0))
    return pl.pallas_call(
        functools.partial(_rwkv_scan_kernel, n_chunks=nc),
        out_shape=(jax.ShapeDtypeStruct((2, m, D), F32), jax.ShapeDtypeStruct(z0.shape, F32)),
        grid=(n_seq, 2, nc),
        in_specs=[tok, tok, tok, tok2, tok2, tok2, zspec],
        out_specs=(tok2, zspec),
        scratch_shapes=[pltpu.VMEM((N_HEADS // 2, 2 * HEAD, 2 * HEAD), F32)],
        compiler_params=_cp(("parallel", "parallel", "arbitrary")), name="rwkv_scan",
    )(r, v, kk, lw, kd, bd, z0)


def _rwkv_post_kernel(y_ref, bonus_ref, g_ref, lg_ref, lb_ref, sel_ref, o_ref):
    sel = sel_ref[...]
    o = y_ref[0] + y_ref[1]
    cen = o - _head_sum(o, sel) * (1.0 / HEAD)
    var = _head_sum(cen * cen, sel) * (1.0 / HEAD)
    o = cen * lax.rsqrt(var + LNX_EPS) * lg_ref[...] + lb_ref[...] + bonus_ref[...]
    o_ref[...] = (o * g_ref[...]).astype(o_ref.dtype)


def rwkv_post(y, bonus, g, lnx_g, lnx_b, tm=256):
    m = bonus.shape[0]
    tok = pl.BlockSpec((tm, D), lambda i: (i, 0))
    row = pl.BlockSpec((1, D), lambda i: (0, 0))
    return pl.pallas_call(
        _rwkv_post_kernel,
        out_shape=jax.ShapeDtypeStruct((m, D), BF16),
        grid=(m // tm,),
        in_specs=[pl.BlockSpec((2, tm, D), lambda i: (0, i, 0)), tok, tok, row, row,
                  pl.BlockSpec((D, 128), lambda i: (0, 0))],
        out_specs=tok,
        compiler_params=_cp(("parallel",)), name="rwkv_post",
    )(y, bonus, g, lnx_g.reshape(1, D), lnx_b.reshape(1, D), _head_select())


def rwkv_layer(x, norm_g, mods, tpc, t, s0, p):
    r, v, kk, g, bonus, lw, kd, bd = rwkv_proj(x, norm_g, mods, tpc, t, p)
    y, zf = rwkv_scan(r, v, kk, lw, kd, bd, s0, t)
    o = rwkv_post(y, bonus, g, p["lnx_g"], p["lnx_b"])
    return matmul_res(o, p["w_o"], x, mods, tpc, 2), zf


def _softmax_rows(parts):
    m = parts[0].max(axis=-1, keepdims=True)
    for s in parts[1:]:
        m = jnp.maximum(m, s.max(axis=-1, keepdims=True))
    es = [jnp.exp(s - m) for s in parts]
    den = es[0].sum(axis=-1, keepdims=True)
    for e in es[1:]:
        den = den + e.sum(axis=-1, keepdims=True)
    inv = 1.0 / den
    return [(e * inv).astype(BF16) for e in es]


def _qkv_kernel(x_ref, g_ref, m_ref, w_ref, *out_refs, with_f32):
    h = _modulate(x_ref[...], g_ref[...], m_ref[0, 0:1, :], m_ref[0, 1:2, :]).astype(BF16)
    for idx in range(3):
        r = _dot(h, w_ref[:, idx * D:(idx + 1) * D])
        out_refs[idx][...] = r.astype(BF16)
        if with_f32 and idx > 0:
            out_refs[2 + idx][...] = r


def qkv_proj(x, g, mods, tpc, w, with_f32, tm=512):
    m = x.shape[0]
    tok = pl.BlockSpec((tm, D), lambda i: (i, 0))
    sd = jax.ShapeDtypeStruct
    n_f32 = 2 if with_f32 else 0
    return pl.pallas_call(
        functools.partial(_qkv_kernel, with_f32=with_f32),
        out_shape=(sd((m, D), BF16),) * 3 + (sd((m, D), F32),) * n_f32,
        grid=(m // tm,),
        in_specs=_norm_mod_specs(tm, tpc) + [pl.BlockSpec((D, 3 * D), lambda i: (0, 0))],
        out_specs=(tok,) * (3 + n_f32),
        compiler_params=_cp(("parallel",)), name="qkv_proj",
    )(x, g.reshape(1, D), mods, w)


def _pair_masks():
    lane = lax.broadcasted_iota(jnp.int32, (1, 2 * HEAD), 1)
    return lane < HEAD, lane >= HEAD


def _na_ctx_kernel(q_ref, k_ref, v_ref, o_ref):
    masks = _pair_masks()
    zero = jnp.zeros((), BF16)
    scores = []
    for p in range(N_HEADS // 2):
        sl = slice(p * 2 * HEAD, (p + 1) * 2 * HEAD)
        q = q_ref[:, sl]
        k = k_ref[:, sl]
        scores.append([_dot_nt(jnp.where(hm, q, zero), k) * NA_SCALE for hm in masks])
    probs = [[_softmax_rows([s])[0] for s in pair] for pair in scores]
    for p in range(N_HEADS // 2):
        sl = slice(p * 2 * HEAD, (p + 1) * 2 * HEAD)
        v = v_ref[:, sl]
        o_ref[:, sl] = jnp.where(masks[0], _dot(probs[p][0], v), _dot(probs[p][1], v)).astype(o_ref.dtype)


def na_ctx_attn(q, k, v, t):
    m = q.shape[0]
    seq = pl.BlockSpec((t, D), lambda b: (b, 0))
    return pl.pallas_call(
        _na_ctx_kernel,
        out_shape=jax.ShapeDtypeStruct((m, D), BF16),
        grid=(m // t,),
        in_specs=[seq, seq, seq],
        out_specs=seq,
        compiler_params=_cp(("parallel",)), name="na_ctx_attn",
    )(q, k, v)


def _win_start(r, rows):
    return jnp.clip(r - WIN_ROWS // 2, 0, rows - WIN_ROWS)


def _na_lat_kernel(q_ref, k_ref, v_ref, ck_ref, cv_ref, bias_ref, o_ref, *, rows):
    r = pl.program_id(1)
    start = pl.multiple_of(_win_start(r, rows) * GRID_W, GRID_W)
    n_loc = WIN_ROWS * GRID_W
    qc = lax.broadcasted_iota(jnp.int32, (GRID_W, n_loc), 0)
    kc = lax.broadcasted_iota(jnp.int32, (GRID_W, n_loc), 1) % GRID_W
    cs = jnp.clip(qc - WIN_COLS // 2, 0, GRID_W - WIN_COLS)
    valid = (kc >= cs) & (kc < cs + WIN_COLS)
    masks = _pair_masks()
    zero = jnp.zeros((), BF16)
    scores = []
    for p in range(N_HEADS // 2):
        sl = slice(p * 2 * HEAD, (p + 1) * 2 * HEAD)
        q = q_ref[:, sl]
        kw = k_ref[pl.ds(start, n_loc), sl]
        ck = ck_ref[:, sl]
        pair = []
        for i, hm in enumerate(masks):
            qm = jnp.where(hm, q, zero)
            s_loc = jnp.where(valid, _dot_nt(qm, kw) * NA_SCALE + bias_ref[0, 2 * p + i], NEG_INF)
            pair.append([s_loc, _dot_nt(qm, ck) * NA_SCALE])
        scores.append(pair)
    probs = [[_softmax_rows(parts) for parts in pair] for pair in scores]
    for p in range(N_HEADS // 2):
        sl = slice(p * 2 * HEAD, (p + 1) * 2 * HEAD)
        vw = v_ref[pl.ds(start, n_loc), sl]
        cv = cv_ref[:, sl]
        outs = [_dot(p_loc, vw) + _dot(p_ctx, cv) for p_loc, p_ctx in probs[p]]
        o_ref[:, sl] = jnp.where(masks[0], outs[0], outs[1]).astype(o_ref.dtype)


def _na_bias(rpb, rows):
    wr = min(WIN_ROWS, rows)
    qc = np.arange(GRID_W)
    col_off = np.clip(qc[None, :] - qc[:, None], -(WIN_COLS - 1), WIN_COLS - 1) + WIN_COLS - 1
    onehot = jnp.asarray(np.arange(2 * WIN_COLS - 1)[:, None, None] == col_off[None], F32)
    b = jnp.einsum('hrc,cqk->hrqk', rpb.astype(F32), onehot, precision=lax.Precision.HIGHEST)
    b = jnp.stack([b[:, ro0:ro0 + wr] for ro0 in range(WIN_ROWS)], axis=0)
    return jnp.transpose(b, (0, 1, 3, 2, 4)).reshape(WIN_ROWS, N_HEADS, GRID_W, wr * GRID_W)


def na_lat_attn(q, k, v, ck, cv, rpb, t, past):
    m = q.shape[0]
    rows = t // GRID_W
    assert rows >= WIN_ROWS
    n_seq = m // t
    n_loc = WIN_ROWS * GRID_W

    def bias_idx(b, r):
        return (_win_start(r, rows) - r + WIN_ROWS - 1, 0, 0, 0)

    return pl.pallas_call(
        functools.partial(_na_lat_kernel, rows=rows),
        out_shape=jax.ShapeDtypeStruct((m, D), BF16),
        grid=(n_seq, rows),
        in_specs=[pl.BlockSpec((GRID_W, D), lambda b, r: (b * rows + r, 0)),
                  pl.BlockSpec((t, D), lambda b, r: (b, 0)),
                  pl.BlockSpec((t, D), lambda b, r: (b, 0)),
                  pl.BlockSpec((past, D), lambda b, r: (b, 0)),
                  pl.BlockSpec((past, D), lambda b, r: (b, 0)),
                  pl.BlockSpec((1, N_HEADS, GRID_W, n_loc), bias_idx)],
        out_specs=pl.BlockSpec((GRID_W, D), lambda b, r: (b * rows + r, 0)),
        compiler_params=_cp(("parallel", "arbitrary")), name="na_lat_attn",
    )(q, k, v, ck, cv, _na_bias(rpb, rows))


def _sort16_net():
    def merge(lo, hi, r):
        step = r * 2
        if step < hi - lo:
            yield from merge(lo, hi, step)
            yield from merge(lo + r, hi, step)
            yield from ((i, i + r) for i in range(lo + r, hi - r, step))
        else:
            yield (lo, lo + r)

    def sort(lo, hi):
        if hi - lo >= 1:
            mid = lo + (hi - lo) // 2
            yield from sort(lo, mid)
            yield from sort(mid + 1, hi)
            yield from merge(lo, hi, 1)

    return tuple(sort(0, PEER_TOPK - 1))


_SORT16 = _sort16_net()
_BITONIC16 = tuple((i, i + d) for d in (8, 4, 2, 1) for i in range(PEER_TOPK) if not i & d)


def _exchange(x, net):
    x = list(x)
    for i, j in net:
        x[i], x[j] = jnp.maximum(x[i], x[j]), jnp.minimum(x[i], x[j])
    return x


def _merge_top16(a, b):
    return _exchange([jnp.maximum(a[i], b[PEER_TOPK - 1 - i]) for i in range(PEER_TOPK)], _BITONIC16)


def _merge_sublanes(x):
    for shift in (4, 2, 1):
        x = _merge_top16(x, [pltpu.roll(v, shift, axis=0) for v in x])
    return x


def _peer_stats_kernel(q_ref, keys_ref, cut_out, e1_out, tab_out):
    tm = q_ref.shape[0]
    sub = lax.broadcasted_iota(jnp.int32, (8, tm), 0)
    ninf = jnp.full((8, tm), -jnp.inf, F32)
    for h in range(PEER_HEADS):
        s, tops = [], []
        for c in range(2):
            hc = 2 * h + c
            q = q_ref[:, hc * PEER_KEYS:(hc + 1) * PEER_KEYS]
            sc = _dot_nt(keys_ref[hc], q)
            s.append(sc)
            groups = [sc[8 * i:8 * i + 8, :] for i in range(PEER_KEYS // 8)]
            tops.append(_merge_sublanes(_exchange(groups, _SORT16)))
        a1, a2 = tops
        a1col = a1[7]
        for jj in range(6, -1, -1):
            a1col = jnp.where(sub == jj, a1[jj], a1col)
        lists = [jnp.where(sub < min(8, PEER_TOPK // (k + 1)), a1col + a2[k], ninf) for k in range(PEER_TOPK)]
        tail = [a1[8 + k] + a2[0] for k in range(8)] + [ninf] * 8
        best = _merge_top16(_merge_sublanes(lists), tail)
        z = jnp.zeros((8, tm), F32)
        for b in best:
            z = z + jnp.exp(b - best[0])
        thr = best[PEER_TOPK - 1]
        m2 = a2[0][0:1, :]
        a1hi = a1[15]
        for jj in range(6, -1, -1):
            a1hi = jnp.where(sub == jj, a1[8 + jj], a1hi)
        ranked = [a1col, a1hi]
        cutc = [jnp.full((8, tm), jnp.inf, F32)] * 2
        for k in range(PEER_TOPK):
            e2k = jnp.exp(a2[k] - a2[0])
            cutc = [jnp.where(r + a2[k] >= thr, e2k, c) for r, c in zip(ranked, cutc)]
        cut = jnp.full((PEER_KEYS, tm), jnp.inf, F32)
        for jj in range(PEER_TOPK):
            cut = jnp.where(s[0] == a1[jj][0:1, :], cutc[jj // 8][jj % 8:jj % 8 + 1, :], cut)
        cut_out[h] = cut
        e1_out[h] = jnp.exp(s[0] - a1[0][0:1, :]) * (0.5 / z[0:1, :])
        e2 = jnp.exp(s[1] - m2)
        for lb in range(tm // 128):
            lanes = slice(lb * 128, (lb + 1) * 128)
            tab_out[lb, h] = e2[:, lanes].reshape(PEER_KEYS // 8, 8, 128)


def peer_stats(q, keys, tm=256):
    m = q.shape[0]
    nq = 2 * PEER_HEADS * PEER_KEYS
    sd = jax.ShapeDtypeStruct
    big = pl.BlockSpec((PEER_HEADS, PEER_KEYS, tm), lambda i: (0, 0, i))
    tab_shape = (PEER_HEADS, PEER_KEYS // 8, 8, 128)
    return pl.pallas_call(
        _peer_stats_kernel,
        out_shape=(sd((PEER_HEADS, PEER_KEYS, m), F32),) * 2 + (sd((m // 128,) + tab_shape, F32),),
        grid=(m // tm,),
        in_specs=[pl.BlockSpec((tm, nq), lambda i: (i, 0)),
                  pl.BlockSpec((2 * PEER_HEADS, PEER_KEYS, PEER_KEYS), lambda i: (0, 0, 0))],
        out_specs=(big,) * 2 + (pl.BlockSpec((tm // 128,) + tab_shape, lambda i: (i, 0, 0, 0, 0)),),
        compiler_params=_cp(("parallel",)), name="peer_stats",
    )(q, keys)


def _gelu_x2(x):
    return x * (1.0 + lax.erf(x * (1.0 / math.sqrt(2.0))))


def _peer_dense_kernel(xt_ref, u_ref, un_ref, vt_ref, vp_ref, cut_ref, e1_ref, tab_ref, res_ref, m_ref, o_ref,
                       acc_scr, a_scr, w_scr, *, tm, tn, sub, gate):
    j = pl.program_id(1)
    n_sub = tn // sub
    n_cb = tm // 128
    out_rows = D // n_cb

    def act_matmul(sb):
        return _dot(u_ref[sb * sub:(sb + 1) * sub, :], xt_ref[...])

    @pl.when(j == 0)
    def _():
        acc_scr[...] = jnp.zeros_like(acc_scr)
        w_scr[...] = jnp.zeros_like(w_scr)
        a_scr[...] = act_matmul(0)

    def weights(sb, cb, a):
        lanes = slice(cb * 128, (cb + 1) * 128)
        n_il = sub // PEER_KEYS
        n_rg = PEER_KEYS // PEER_ROWS
        parts = [[None] * n_rg for _ in range(n_il)]
        for rg in range(n_rg):
            grp = slice(rg * PEER_ROWS // 8, (rg + 1) * PEER_ROWS // 8)
            g = [jnp.zeros((PEER_ROWS, 128), F32)] * n_il
            for h in range(PEER_HEADS):
                e2 = tab_ref[cb, h, grp].reshape(PEER_ROWS, 128)
                for il in range(n_il):
                    i1 = sb * n_il + il
                    hit = e2 >= cut_ref[h, i1:i1 + 1, lanes]
                    g[il] = g[il] + jnp.where(hit, e2 * e1_ref[h, i1:i1 + 1, lanes], 0.0)
            for il in range(n_il):
                r0 = il * PEER_KEYS + rg * PEER_ROWS
                parts[il][rg] = (g[il] * _gelu_x2(a[r0:r0 + PEER_ROWS, lanes])).astype(BF16)
        return jnp.concatenate([p for row in parts for p in row], axis=0)

    def out_piece(sb, q, w):
        rows = slice(q * out_rows, (q + 1) * out_rows)
        if sb < 0:
            return _dot(vp_ref[rows, :], w)
        return _dot(vt_ref[rows, sb * sub:(sb + 1) * sub], w)

    a = a_scr[...]
    w_prev = w_scr[...]
    contrib = [None] * n_cb
    for sb in range(n_sub):
        cols = []
        for cb in range(n_cb):
            if cb == 0:
                a_next = act_matmul(sb + 1) if sb + 1 < n_sub else _dot(un_ref[...], xt_ref[...])
            piece = out_piece(sb - 1, cb, w_prev)
            contrib[cb] = piece if contrib[cb] is None else contrib[cb] + piece
            cols.append(weights(sb, cb, a))
        w_prev = jnp.concatenate(cols, axis=1)
        a = a_next
    for q in range(n_cb):
        acc_scr[q * out_rows:(q + 1) * out_rows, :] += contrib[q]
    a_scr[...] = a
    w_scr[...] = w_prev

    @pl.when(j == pl.num_programs(1) - 1)
    def _():
        for q in range(n_cb):
            acc_scr[q * out_rows:(q + 1) * out_rows, :] += out_piece(n_sub - 1, q, w_scr[...])
        o_ref[...] = res_ref[...] + m_ref[0, gate:gate + 1, :] * acc_scr[...].T


def peer_dense(xt, u, vt, layer, cut, e1, tab, res, mods, tpc, gate, tm=512, tn=1024, sub=256):
    m = xt.shape[1]
    n1 = tn // PEER_KEYS
    part = pl.BlockSpec((PEER_HEADS, n1, tm), lambda i, j: (0, j, i))
    full = pl.BlockSpec((tm // 128,) + tab.shape[1:], lambda i, j: (i, 0, 0, 0, 0))
    n_sub = tn // sub
    last_sub = N_EXPERTS // sub - 1
    return pl.pallas_call(
        functools.partial(_peer_dense_kernel, tm=tm, tn=tn, sub=sub, gate=gate),
        out_shape=jax.ShapeDtypeStruct((m, D), F32),
        grid=(m // tm, N_EXPERTS // tn),
        in_specs=[pl.BlockSpec((D, tm), lambda i, j: (0, i)),
                  pl.BlockSpec((None, tn, D), lambda i, j: (layer, j, 0)),
                  pl.BlockSpec((None, sub, D), lambda i, j: (layer, jnp.minimum((j + 1) * n_sub, last_sub), 0)),
                  pl.BlockSpec((None, D, tn), lambda i, j: (layer, 0, j)),
                  pl.BlockSpec((None, D, sub), lambda i, j: (layer, 0, jnp.maximum(j * n_sub - 1, 0))),
                  part, part, full,
                  pl.BlockSpec((tm, D), lambda i, j: (i, 0)),
                  pl.BlockSpec((1, N_MOD, D), lambda i, j: ((i * tm) // tpc, 0, 0))],
        out_specs=pl.BlockSpec((tm, D), lambda i, j: (i, 0)),
        scratch_shapes=[pltpu.VMEM((D, tm), F32), pltpu.VMEM((sub, tm), F32), pltpu.VMEM((sub, tm), BF16)],
        compiler_params=_cp(("parallel", "arbitrary")), name="peer_dense",
    )(xt, u, u, vt, vt, cut, e1, tab, res, mods)


def peer_layer(x, norm_g, mods, tpc, wq, keys, u, vt, layer):
    q, ht = norm_mod_q(x, norm_g, mods, tpc, 3, wq)
    cut, e1, tab = peer_stats(q, keys)
    return peer_dense(ht, u, vt, layer, cut, e1, tab, x, mods, tpc, 5)


def kernel(x_prompt, x_sample, c, state_wkv, cache_k, cache_v, c_ctx, ada_w, ada_b, norm_mix, norm_ffn, fnet_w_in, fnet_w_out, rwkv_mu, rwkv_w_r, rwkv_w_k, rwkv_w_v, rwkv_w_o, rwkv_w0, rwkv_w1, rwkv_w2, rwkv_a0, rwkv_a1, rwkv_a2, rwkv_g1, rwkv_g2, rwkv_k_k, rwkv_k_a, rwkv_r_k, rwkv_lnx_g, rwkv_lnx_b, na_w_qkv, na_w_o, na_rpb, peer_w_q, peer_sub_keys, peer_u, peer_v, final_norm):
    nb_c, t_c, _ = x_prompt.shape
    nb_s, t_s, _ = x_sample.shape
    depth = ada_w.shape[0]
    past = cache_k.shape[2]
    bf = lambda w: w.astype(BF16)

    cond = jnp.concatenate([c_ctx[None, :], c, jnp.zeros((16 - 1 - nb_s, D), F32)], axis=0)
    mods_all = ada_all(cond, ada_w, ada_b).reshape(depth, 16, N_MOD, D)

    xc = x_prompt.reshape(nb_c * t_c, D)
    xs = x_sample.reshape(nb_s * t_s, D)
    tpc_c, tpc_s = nb_c * t_c, t_s
    new_wkv, new_k, new_v = [], [], []
    u_all = bf(peer_u)
    vt_all = jnp.swapaxes(bf(peer_v), 1, 2)

    for l in range(depth):
        kind, j = l % 3, l // 3
        mc = mods_all[l, 0:1]
        ms = mods_all[l, 1:1 + nb_s]
        if kind == 0:
            w_in, w_out = bf(fnet_w_in[j]), bf(fnet_w_out[j])
            outs = []
            for x, m, tpc, t in ((xc, mc, tpc_c, t_c), (xs, ms, tpc_s, t_s)):
                f = fnet_dft(norm_mod_mm(x, norm_mix[l], m, tpc, 0, w_in), t)
                outs.append(matmul_res(f, w_out, x, m, tpc, 2))
            xc, xs = outs
        elif kind == 1:
            p = dict(mu=rwkv_mu[j], w_r=bf(rwkv_w_r[j]), w_k=bf(rwkv_w_k[j]), w_v=bf(rwkv_w_v[j]), w_o=bf(rwkv_w_o[j]),
                     w0=rwkv_w0[j], w1=bf(rwkv_w1[j]), w2=bf(rwkv_w2[j]), a0=rwkv_a0[j], a1=bf(rwkv_a1[j]),
                     a2=bf(rwkv_a2[j]), g1=bf(rwkv_g1[j]), g2=bf(rwkv_g2[j]), k_k=rwkv_k_k[j].reshape(1, D),
                     k_a=rwkv_k_a[j].reshape(1, D), r_k=rwkv_r_k[j].reshape(1, D), lnx_g=rwkv_lnx_g[j],
                     lnx_b=rwkv_lnx_b[j])
            s_zero = jnp.zeros((nb_c, 2, N_HEADS, HEAD, HEAD), F32)
            xc, sc = rwkv_layer(xc, norm_mix[l], mc, tpc_c, t_c, s_zero, p)
            xs, _ = rwkv_layer(xs, norm_mix[l], ms, tpc_s, t_s, state_wkv[:, j], p)
            new_wkv.append(sc)
        else:
            w_qkv, w_o = bf(na_w_qkv[j]), bf(na_w_o[j])
            qc, kc, vc, kc32, vc32 = qkv_proj(xc, norm_mix[l], mc, tpc_c, w_qkv, True)
            qs, ks, vs = qkv_proj(xs, norm_mix[l], ms, tpc_s, w_qkv, False)
            new_k.append(kc32.reshape(nb_c, t_c, N_HEADS, HEAD))
            new_v.append(vc32.reshape(nb_c, t_c, N_HEADS, HEAD))
            oc = na_ctx_attn(qc, kc, vc, t_c)
            os_ = na_lat_attn(qs, ks, vs, bf(cache_k[:, j]).reshape(nb_s * past, D),
                              bf(cache_v[:, j]).reshape(nb_s * past, D), na_rpb[j], t_s, past)
            xc = matmul_res(oc, w_o, xc, mc, tpc_c, 2)
            xs = matmul_res(os_, w_o, xs, ms, tpc_s, 2)
        wq = bf(peer_w_q[l])
        keys = bf(peer_sub_keys[l]).reshape(2 * PEER_HEADS, PEER_KEYS, PEER_KEYS)
        xc = peer_layer(xc, norm_ffn[l], mc, tpc_c, wq, keys, u_all, vt_all, l)
        xs = peer_layer(xs, norm_ffn[l], ms, tpc_s, wq, keys, u_all, vt_all, l)

    y_prompt = rms_final(xc, final_norm).reshape(nb_c, t_c, D)
    y_sample = rms_final(xs, final_norm).reshape(nb_s, t_s, D)
    return (y_prompt, y_sample, jnp.stack(new_wkv, axis=1), jnp.stack(new_k, axis=1), jnp.stack(new_v, axis=1))
```

```python
import functools
import math

import numpy as np
import jax
import jax.numpy as jnp
from jax import lax
from jax.experimental import pallas as pl
from jax.experimental.pallas import tpu as pltpu

F32 = jnp.float32
BF16 = jnp.bfloat16

D = 1024
N_MOD = 6
EPS = 1e-6
HEAD = 64
N_HEADS = D // HEAD
LNX_EPS = 64e-5
GRID_W = 64
WIN_ROWS = 8
WIN_COLS = 16
NA_SCALE = HEAD ** -0.5
NEG_INF = -1e30
FNET_GROUPS = 4
FNET_GD = D // FNET_GROUPS
PEER_KEYS = 128
PEER_HEADS = 8
PEER_TOPK = 16
N_EXPERTS = PEER_KEYS * PEER_KEYS
PEER_ROWS = 32
SCAN_L = 128

VMEM_LIMIT = 56 * 1024 * 1024


def _cp(sem, vmem=VMEM_LIMIT):
    return pltpu.CompilerParams(dimension_semantics=sem, vmem_limit_bytes=vmem)


def _dot(a, b):
    return jnp.dot(a, b, preferred_element_type=F32)


def _dot_nt(a, b):
    return lax.dot_general(a, b, (((1,), (1,)), ((), ())), preferred_element_type=F32)


def _split_dot(x, w):
    hi = x.astype(BF16)
    lo = (x - hi.astype(F32)).astype(BF16)
    return _dot(hi, w) + _dot(lo, w)


def _ada_kernel(c_ref, w_ref, b_ref, o_ref):
    c = c_ref[...]
    s = c * jax.nn.sigmoid(c)
    o_ref[0] = _dot(s.astype(BF16), w_ref[0].astype(BF16)) + b_ref[0]


def ada_all(cond16, ada_w, ada_b):
    depth = ada_w.shape[0]
    tn = 1024
    return pl.pallas_call(
        _ada_kernel,
        out_shape=jax.ShapeDtypeStruct((depth, 16, N_MOD * D), F32),
        grid=(depth, N_MOD * D // tn),
        in_specs=[pl.BlockSpec((16, D), lambda l, j: (0, 0)),
                  pl.BlockSpec((1, D, tn), lambda l, j: (l, 0, j)),
                  pl.BlockSpec((1, 1, tn), lambda l, j: (l, 0, j))],
        out_specs=pl.BlockSpec((1, 16, tn), lambda l, j: (l, 0, j)),
        compiler_params=_cp(("parallel", "parallel")),
        name="ada",
    )(cond16, ada_w, ada_b.reshape(depth, 1, N_MOD * D))


def _modulate(x, g, shift, scale):
    y = x * lax.rsqrt(jnp.mean(x * x, axis=-1, keepdims=True) + EPS)
    return (y * g) * (1 + scale) + shift


def _norm_mod_q_kernel(x_ref, g_ref, m_ref, w_ref, q_ref, ht_ref, *, which):
    h = _modulate(x_ref[...], g_ref[...], m_ref[0, which:which + 1, :], m_ref[0, which + 1:which + 2, :])
    q_ref[...] = _dot(h.astype(BF16), w_ref[...]).astype(q_ref.dtype)
    ht_ref[...] = h.T.astype(ht_ref.dtype)


def _norm_mod_specs(tm, tpc):
    return [pl.BlockSpec((tm, D), lambda i: (i, 0)),
            pl.BlockSpec((1, D), lambda i: (0, 0)),
            pl.BlockSpec((1, N_MOD, D), lambda i: ((i * tm) // tpc, 0, 0))]


def _norm_mod_mm_kernel(x_ref, g_ref, m_ref, w_ref, o_ref, *, which):
    h = _modulate(x_ref[...], g_ref[...], m_ref[0, which:which + 1, :], m_ref[0, which + 1:which + 2, :])
    o_ref[...] = _dot(h.astype(BF16), w_ref[...]).astype(o_ref.dtype)


def norm_mod_mm(x, g, mods, tpc, which, w, tm=512):
    m, n = x.shape[0], w.shape[1]
    return pl.pallas_call(
        functools.partial(_norm_mod_mm_kernel, which=which),
        out_shape=jax.ShapeDtypeStruct((m, n), BF16),
        grid=(m // tm,), in_specs=_norm_mod_specs(tm, tpc) + [pl.BlockSpec((D, n), lambda i: (0, 0))],
        out_specs=pl.BlockSpec((tm, n), lambda i: (i, 0)),
        compiler_params=_cp(("parallel",)), name="norm_mod_mm",
    )(x, g.reshape(1, D), mods, w)


def norm_mod_q(x, g, mods, tpc, which, w, tm=512):
    m, n = x.shape[0], w.shape[1]
    return pl.pallas_call(
        functools.partial(_norm_mod_q_kernel, which=which),
        out_shape=(jax.ShapeDtypeStruct((m, n), BF16), jax.ShapeDtypeStruct((D, m), BF16)),
        grid=(m // tm,), in_specs=_norm_mod_specs(tm, tpc) + [pl.BlockSpec((D, n), lambda i: (0, 0))],
        out_specs=(pl.BlockSpec((tm, n), lambda i: (i, 0)), pl.BlockSpec((D, tm), lambda i: (0, i))),
        compiler_params=_cp(("parallel",)), name="norm_mod_q",
    )(x, g.reshape(1, D), mods, w)


def _final_norm_kernel(x_ref, g_ref, o_ref):
    x = x_ref[...]
    o_ref[...] = x * lax.rsqrt(jnp.mean(x * x, axis=-1, keepdims=True) + EPS) * g_ref[...]


def rms_final(x, g, tm=512):
    m = x.shape[0]
    return pl.pallas_call(
        _final_norm_kernel,
        out_shape=jax.ShapeDtypeStruct((m, D), F32),
        grid=(m // tm,),
        in_specs=[pl.BlockSpec((tm, D), lambda i: (i, 0)), pl.BlockSpec((1, D), lambda i: (0, 0))],
        out_specs=pl.BlockSpec((tm, D), lambda i: (i, 0)),
        compiler_params=_cp(("parallel",)), name="final_norm",
    )(x, g.reshape(1, D))


def _mm_res_kernel(a_ref, w_ref, res_ref, m_ref, o_ref, *, gate):
    o_ref[...] = res_ref[...] + m_ref[0, gate:gate + 1, :] * _dot(a_ref[...], w_ref[...])


def matmul_res(a, w, res, mods, tpc, gate, tm=512):
    m = a.shape[0]
    return pl.pallas_call(
        functools.partial(_mm_res_kernel, gate=gate),
        out_shape=jax.ShapeDtypeStruct((m, D), F32),
        grid=(m // tm,),
        in_specs=[pl.BlockSpec((tm, D), lambda i: (i, 0)),
                  pl.BlockSpec((D, D), lambda i: (0, 0)),
                  pl.BlockSpec((tm, D), lambda i: (i, 0)),
                  pl.BlockSpec((1, N_MOD, D), lambda i: ((i * tm) // tpc, 0, 0))],
        out_specs=pl.BlockSpec((tm, D), lambda i: (i, 0)),
        compiler_params=_cp(("parallel",)), name="matmul_res",
    )(a, w, res, mods)


def _dft_mats(t):
    def cs(n):
        k = np.arange(n)
        ang = 2.0 * np.pi * ((k[:, None] * k[None, :]) % n) / n
        s = 1.0 / math.sqrt(n)
        return np.cos(ang) * s, np.sin(ang) * s
    cc, sc = cs(FNET_GD)
    ct, st = cs(t)
    return (jnp.asarray(np.concatenate([cc, sc], axis=1), BF16), jnp.asarray(ct, BF16), jnp.asarray(st, BF16))


def _dft_kernel(u_ref, cs_ref, ct_ref, st_ref, o_ref):
    p = _dot(u_ref[...], cs_ref[...])
    pc = p[:, :FNET_GD].astype(BF16)
    ps = p[:, FNET_GD:].astype(BF16)
    o_ref[...] = (_dot(ct_ref[...], pc) - _dot(st_ref[...], ps)).astype(o_ref.dtype)


def fnet_dft(u, t):
    m = u.shape[0]
    cs, ct, st = _dft_mats(t)
    return pl.pallas_call(
        _dft_kernel,
        out_shape=jax.ShapeDtypeStruct((m, D), BF16),
        grid=(m // t, FNET_GROUPS),
        in_specs=[pl.BlockSpec((t, FNET_GD), lambda s, g: (s, g)),
                  pl.BlockSpec((FNET_GD, 2 * FNET_GD), lambda s, g: (0, 0)),
                  pl.BlockSpec((t, t), lambda s, g: (0, 0)),
                  pl.BlockSpec((t, t), lambda s, g: (0, 0))],
        out_specs=pl.BlockSpec((t, FNET_GD), lambda s, g: (s, g)),
        compiler_params=_cp(("parallel", "parallel")), name="fnet_dft",
    )(u, cs, ct, st)


def _head_select():
    return jnp.asarray((np.arange(D) // HEAD)[:, None] == np.arange(128)[None, :], BF16)


def _head_sum(x, sel):
    c = _split_dot(x, sel)
    hi = c.astype(BF16)
    lo = (c - hi.astype(F32)).astype(BF16)
    return _dot_nt(hi, sel) + _dot_nt(lo, sel)


def _rwkv_proj_kernel(x_ref, xp_ref, xn_ref, g_ref, m_ref, mu_ref, wr_ref, wk_ref, wv_ref, g1_ref, g2_ref,
                      w0_ref, w1_ref, w2_ref, a0_ref, a1_ref, a2_ref, kk_ref, ka_ref, rk_ref, sel_ref,
                      r_out, v_out, kkn_out, g_out, bonus_out, lw_out, kd_out, bd_out, *, tm, t):
    i = pl.program_id(0)
    shift = m_ref[0, 0:1, :]
    scale = m_ref[0, 1:2, :]
    g = g_ref[...]
    h = _modulate(x_ref[...], g, shift, scale)
    first = (i * tm) % t == 0
    last = ((i + 1) * tm) % t == 0
    hp = jnp.where(first, 0.0, _modulate(xp_ref[7:8, :], g, shift, scale))
    hn = jnp.where(last, 0.0, _modulate(xn_ref[0:1, :], g, shift, scale))
    row = lax.broadcasted_iota(jnp.int32, (tm, 1), 0)
    prev = jnp.where(row == 0, hp, pltpu.roll(h, 1, axis=0))
    nxt = jnp.where(row == tm - 1, hn, pltpu.roll(h, tm - 1, axis=0))
    xx = 0.5 * (prev + nxt) - h

    def mix(j):
        return (h + xx * mu_ref[j:j + 1, :]).astype(BF16)

    r = _dot(mix(0), wr_ref[...])
    k = _dot(mix(2), wk_ref[...])
    v = _dot(mix(3), wv_ref[...])
    gate = _dot(jax.nn.sigmoid(_dot(mix(5), g1_ref[...])).astype(BF16), g2_ref[...])
    xw = mix(1)
    xa = mix(4)
    sel = sel_ref[...]
    kk = k * kk_ref[...]
    kk = kk * lax.rsqrt(_head_sum(kk * kk, sel) + 1e-12)
    ksum = jnp.zeros_like(k)
    for j in range(2):
        w_raw = w0_ref[j:j + 1, :] + _dot(jnp.tanh(_dot(xw, w1_ref[j])).astype(BF16), w2_ref[j])
        lw_out[j] = -jnp.exp(-jax.nn.softplus(-w_raw) - 0.5)
        a = jax.nn.sigmoid(a0_ref[j:j + 1, :] + _dot(_dot(xa, a1_ref[j]).astype(BF16), a2_ref[j]))
        kd = k * (1 + (a - 1) * ka_ref[...])
        kd_out[j] = kd
        bd_out[j] = kk * a
        ksum = ksum + kd
    r_out[...] = r
    v_out[...] = v
    kkn_out[...] = kk
    g_out[...] = gate
    bonus_out[...] = _head_sum(r * ksum * rk_ref[...], sel) * v


def rwkv_proj(x, norm_g, mods, tpc, t, p, tm=256):
    m = x.shape[0]
    nb8 = m // 8
    full = lambda *shape: pl.BlockSpec(shape, lambda i: (0,) * len(shape))
    tok = pl.BlockSpec((tm, D), lambda i: (i, 0))
    tok2 = pl.BlockSpec((2, tm, D), lambda i: (0, i, 0))
    in_specs = [tok,
                pl.BlockSpec((8, D), lambda i: (jnp.maximum(i * (tm // 8) - 1, 0), 0)),
                pl.BlockSpec((8, D), lambda i: (jnp.minimum((i + 1) * (tm // 8), nb8 - 1), 0)),
                full(1, D),
                pl.BlockSpec((1, N_MOD, D), lambda i: ((i * tm) // tpc, 0, 0)),
                full(6, D), full(D, D), full(D, D), full(D, D), full(D, 128), full(128, D),
                full(2, D), full(2, D, 64), full(2, 64, D), full(2, D), full(2, D, 64), full(2, 64, D),
                full(1, D), full(1, D), full(1, D), full(D, 128)]
    sd = jax.ShapeDtypeStruct
    return pl.pallas_call(
        functools.partial(_rwkv_proj_kernel, tm=tm, t=t),
        out_shape=(sd((m, D), F32),) * 5 + (sd((2, m, D), F32),) * 3,
        grid=(m // tm,), in_specs=in_specs,
        out_specs=(tok,) * 5 + (tok2,) * 3,
        compiler_params=_cp(("parallel",)), name="rwkv_proj",
    )(x, x, x, norm_g.reshape(1, D), mods, p["mu"], p["w_r"], p["w_k"], p["w_v"], p["g1"], p["g2"],
      p["w0"], p["w1"], p["w2"], p["a0"], p["a1"], p["a2"], p["k_k"], p["k_a"], p["r_k"], _head_select())


def _rwkv_scan_kernel(r_ref, v_ref, kk_ref, lw_ref, kd_ref, b_ref, z0_ref, y_ref, zout_ref, z_scr, *, n_chunks):
    L = SCAN_L
    d = pl.program_id(1)
    c = pl.program_id(2)

    row = lax.broadcasted_iota(jnp.int32, (L, L), 0)
    col = lax.broadcasted_iota(jnp.int32, (L, L), 1)

    @pl.when(c == 0)
    def _():
        dup = (lax.broadcasted_iota(jnp.int32, (HEAD, L), 1) % HEAD
               == lax.broadcasted_iota(jnp.int32, (HEAD, L), 0)).astype(BF16)
        for p in range(N_HEADS // 2):
            both = jnp.concatenate([_split3_dot(z0_ref[0, 0, 2 * p], dup), _split3_dot(z0_ref[0, 0, 2 * p + 1], dup)],
                                   axis=0)
            z_scr[p] = jnp.where((row // HEAD) == (col // HEAD), both, 0.0)

    fwd = d == 0
    order = (col - row) * (1 - 2 * d)
    before = order < 0
    upto = order <= 0
    cum_mat = upto.astype(BF16)
    same_head = (row // HEAD) == (col // HEAD)
    lane = lax.broadcasted_iota(jnp.int32, (1, 2 * HEAD), 1)
    head_mask = (lane < HEAD, lane >= HEAD)
    n_double = int(math.log2(L))

    def prepare(p):
        sl = slice(p * 2 * HEAD, (p + 1) * 2 * HEAD)
        lw = lw_ref[0, :, sl]
        cum = _split_dot_left(cum_mat, lw)
        tot = jnp.where(fwd, cum[L - 1:L, :], cum[0:1, :])
        inv = jnp.exp(-cum)
        ar = jnp.concatenate([-kk_ref[:, sl] * jnp.exp(cum - lw), r_ref[:, sl] * jnp.exp(cum)], axis=0)
        bk = jnp.concatenate([b_ref[0, :, sl] * inv, kd_ref[0, :, sl] * inv], axis=0).astype(BF16)
        z = z_scr[p]
        base = _dot_nt(ar.astype(BF16), z.astype(BF16))
        return dict(ar=ar, bk=bk, z=z, base=base, v=v_ref[:, sl], tot=tot)

    def start_chain(pp, hm):
        g4 = _dot_nt(jnp.where(hm, pp["ar"], 0.0).astype(BF16), pp["bk"])
        vm = jnp.where(hm, pp["v"], 0.0).astype(BF16)
        n = jnp.where(before, g4[:L, :L], 0.0).astype(BF16)
        x = jnp.where(hm, pp["base"][:L], 0.0) + _dot(jnp.where(before, g4[:L, L:], 0.0).astype(BF16), vm)
        out = jnp.concatenate([jnp.where(upto, g4[L:, :L], 0.0), jnp.where(upto, g4[L:, L:], 0.0)], axis=1)
        return dict(n=n, x=x, out=out.astype(BF16), vm=vm)

    pairs = [prepare(p) for p in range(N_HEADS // 2)]
    chains = [[start_chain(pp, hm) for hm in head_mask] for pp in pairs]
    for it in range(n_double):
        for ch in (ch for pair in chains for ch in pair):
            nb = ch["n"]
            xb = ch["x"].astype(BF16)
            if it + 1 < n_double:
                res = _dot(nb, jnp.concatenate([xb, nb], axis=1))
                ch["x"] = ch["x"] + res[:, :2 * HEAD]
                ch["n"] = res[:, 2 * HEAD:].astype(BF16)
            else:
                ch["x"] = ch["x"] + _dot(nb, xb)
    for p, (pp, pair) in enumerate(zip(pairs, chains)):
        y = pp["base"][L:]
        for ch in pair:
            y = y + _dot(ch["out"], jnp.concatenate([ch["x"].astype(BF16), ch["vm"]], axis=0))
        uv = jnp.concatenate([pair[0]["x"] + pair[1]["x"], pp["v"]], axis=0)
        inc = _dot(uv.T.astype(BF16), pp["bk"])
        z_scr[p] = jnp.where(same_head, pp["z"] + inc, 0.0) * jnp.exp(pp["tot"])
        y_ref[0, :, p * 2 * HEAD:(p + 1) * 2 * HEAD] = y

    @pl.when(c == n_chunks - 1)
    def _():
        prow = lax.broadcasted_iota(jnp.int32, (L, HEAD), 0)
        pcol = lax.broadcasted_iota(jnp.int32, (L, HEAD), 1)
        pick_a = (prow == pcol).astype(BF16)
        pick_b = (prow == pcol + HEAD).astype(BF16)
        for p in range(N_HEADS // 2):
            z = z_scr[p]
            zout_ref[0, 0, 2 * p] = _split3_dot(z[:HEAD, :], pick_a)
            zout_ref[0, 0, 2 * p + 1] = _split3_dot(z[HEAD:, :], pick_b)


def _split3_dot(x, w):
    hi = x.astype(BF16)
    r1 = x - hi.astype(F32)
    mid = r1.astype(BF16)
    lo = (r1 - mid.astype(F32)).astype(BF16)
    return _dot(hi, w) + _dot(mid, w) + _dot(lo, w)


def _split_dot_left(w, x):
    hi = x.astype(BF16)
    lo = (x - hi.astype(F32)).astype(BF16)
    return _dot(w, hi) + _dot(w, lo)


def rwkv_scan(r, v, kk, lw, kd, bd, z0, t):
    m = r.shape[0]
    n_seq = m // t
    nc = t // SCAN_L

    def blk(s, d, c):
        return s * nc + c + d * (nc - 1 - 2 * c)

    tok = pl.BlockSpec((SCAN_L, D), lambda s, d, c: (blk(s, d, c), 0))
    tok2 = pl.BlockSpec((1, SCAN_L, D), lambda s, d, c: (d, blk(s, d, c), 0))
    zspec = pl.BlockSpec((1, 1, N_HEADS, HEAD, HEAD), lambda s, d, c: (s, d, 0, 0, 0))
    return pl.pallas_call(
        functools.partial(_rwkv_scan_kernel, n_chunks=nc),
        out_shape=(jax.ShapeDtypeStruct((2, m, D), F32), jax.ShapeDtypeStruct(z0.shape, F32)),
        grid=(n_seq, 2, nc),
        in_specs=[tok, tok, tok, tok2, tok2, tok2, zspec],
        out_specs=(tok2, zspec),
        scratch_shapes=[pltpu.VMEM((N_HEADS // 2, 2 * HEAD, 2 * HEAD), F32)],
        compiler_params=_cp(("parallel", "parallel", "arbitrary")), name="rwkv_scan",
    )(r, v, kk, lw, kd, bd, z0)


def _rwkv_post_kernel(y_ref, bonus_ref, g_ref, lg_ref, lb_ref, sel_ref, o_ref):
    sel = sel_ref[...]
    o = y_ref[0] + y_ref[1]
    cen = o - _head_sum(o, sel) * (1.0 / HEAD)
    var = _head_sum(cen * cen, sel) * (1.0 / HEAD)
    o = cen * lax.rsqrt(var + LNX_EPS) * lg_ref[...] + lb_ref[...] + bonus_ref[...]
    o_ref[...] = (o * g_ref[...]).astype(o_ref.dtype)


def rwkv_post(y, bonus, g, lnx_g, lnx_b, tm=256):
    m = bonus.shape[0]
    tok = pl.BlockSpec((tm, D), lambda i: (i, 0))
    row = pl.BlockSpec((1, D), lambda i: (0, 0))
    return pl.pallas_call(
        _rwkv_post_kernel,
        out_shape=jax.ShapeDtypeStruct((m, D), BF16),
        grid=(m // tm,),
        in_specs=[pl.BlockSpec((2, tm, D), lambda i: (0, i, 0)), tok, tok, row, row,
                  pl.BlockSpec((D, 128), lambda i: (0, 0))],
        out_specs=tok,
        compiler_params=_cp(("parallel",)), name="rwkv_post",
    )(y, bonus, g, lnx_g.reshape(1, D), lnx_b.reshape(1, D), _head_select())


def rwkv_layer(x, norm_g, mods, tpc, t, s0, p):
    r, v, kk, g, bonus, lw, kd, bd = rwkv_proj(x, norm_g, mods, tpc, t, p)
    y, zf = rwkv_scan(r, v, kk, lw, kd, bd, s0, t)
    o = rwkv_post(y, bonus, g, p["lnx_g"], p["lnx_b"])
    return matmul_res(o, p["w_o"], x, mods, tpc, 2), zf


def _softmax_rows(parts):
    m = parts[0].max(axis=-1, keepdims=True)
    for s in parts[1:]:
        m = jnp.maximum(m, s.max(axis=-1, keepdims=True))
    es = [jnp.exp(s - m) for s in parts]
    den = es[0].sum(axis=-1, keepdims=True)
    for e in es[1:]:
        den = den + e.sum(axis=-1, keepdims=True)
    inv = 1.0 / den
    return [(e * inv).astype(BF16) for e in es]


def _qkv_kernel(x_ref, g_ref, m_ref, w_ref, *out_refs, with_f32):
    h = _modulate(x_ref[...], g_ref[...], m_ref[0, 0:1, :], m_ref[0, 1:2, :]).astype(BF16)
    for idx in range(3):
        r = _dot(h, w_ref[:, idx * D:(idx + 1) * D])
        out_refs[idx][...] = r.astype(BF16)
        if with_f32 and idx > 0:
            out_refs[2 + idx][...] = r


def qkv_proj(x, g, mods, tpc, w, with_f32, tm=512):
    m = x.shape[0]
    tok = pl.BlockSpec((tm, D), lambda i: (i, 0))
    sd = jax.ShapeDtypeStruct
    n_f32 = 2 if with_f32 else 0
    return pl.pallas_call(
        functools.partial(_qkv_kernel, with_f32=with_f32),
        out_shape=(sd((m, D), BF16),) * 3 + (sd((m, D), F32),) * n_f32,
        grid=(m // tm,),
        in_specs=_norm_mod_specs(tm, tpc) + [pl.BlockSpec((D, 3 * D), lambda i: (0, 0))],
        out_specs=(tok,) * (3 + n_f32),
        compiler_params=_cp(("parallel",)), name="qkv_proj",
    )(x, g.reshape(1, D), mods, w)


def _pair_masks():
    lane = lax.broadcasted_iota(jnp.int32, (1, 2 * HEAD), 1)
    return lane < HEAD, lane >= HEAD


def _na_ctx_kernel(q_ref, k_ref, v_ref, o_ref):
    masks = _pair_masks()
    zero = jnp.zeros((), BF16)
    scores = []
    for p in range(N_HEADS // 2):
        sl = slice(p * 2 * HEAD, (p + 1) * 2 * HEAD)
        q = q_ref[:, sl]
        k = k_ref[:, sl]
        scores.append([_dot_nt(jnp.where(hm, q, zero), k) * NA_SCALE for hm in masks])
    probs = [[_softmax_rows([s])[0] for s in pair] for pair in scores]
    for p in range(N_HEADS // 2):
        sl = slice(p * 2 * HEAD, (p + 1) * 2 * HEAD)
        v = v_ref[:, sl]
        o_ref[:, sl] = jnp.where(masks[0], _dot(probs[p][0], v), _dot(probs[p][1], v)).astype(o_ref.dtype)


def na_ctx_attn(q, k, v, t):
    m = q.shape[0]
    seq = pl.BlockSpec((t, D), lambda b: (b, 0))
    return pl.pallas_call(
        _na_ctx_kernel,
        out_shape=jax.ShapeDtypeStruct((m, D), BF16),
        grid=(m // t,),
        in_specs=[seq, seq, seq],
        out_specs=seq,
        compiler_params=_cp(("parallel",)), name="na_ctx_attn",
    )(q, k, v)


def _win_start(r, rows):
    return jnp.clip(r - WIN_ROWS // 2, 0, rows - WIN_ROWS)


def _na_lat_kernel(q_ref, k_ref, v_ref, ck_ref, cv_ref, bias_ref, o_ref, *, rows):
    r = pl.program_id(1)
    start = pl.multiple_of(_win_start(r, rows) * GRID_W, GRID_W)
    n_loc = WIN_ROWS * GRID_W
    qc = lax.broadcasted_iota(jnp.int32, (GRID_W, n_loc), 0)
    kc = lax.broadcasted_iota(jnp.int32, (GRID_W, n_loc), 1) % GRID_W
    cs = jnp.clip(qc - WIN_COLS // 2, 0, GRID_W - WIN_COLS)
    valid = (kc >= cs) & (kc < cs + WIN_COLS)
    masks = _pair_masks()
    zero = jnp.zeros((), BF16)
    scores = []
    for p in range(N_HEADS // 2):
        sl = slice(p * 2 * HEAD, (p + 1) * 2 * HEAD)
        q = q_ref[:, sl]
        kw = k_ref[pl.ds(start, n_loc), sl]
        ck = ck_ref[:, sl]
        pair = []
        for i, hm in enumerate(masks):
            qm = jnp.where(hm, q, zero)
            s_loc = jnp.where(valid, _dot_nt(qm, kw) * NA_SCALE + bias_ref[0, 2 * p + i], NEG_INF)
            pair.append([s_loc, _dot_nt(qm, ck) * NA_SCALE])
        scores.append(pair)
    probs = [[_softmax_rows(parts) for parts in pair] for pair in scores]
    for p in range(N_HEADS // 2):
        sl = slice(p * 2 * HEAD, (p + 1) * 2 * HEAD)
        vw = v_ref[pl.ds(start, n_loc), sl]
        cv = cv_ref[:, sl]
        outs = [_dot(p_loc, vw) + _dot(p_ctx, cv) for p_loc, p_ctx in probs[p]]
        o_ref[:, sl] = jnp.where(masks[0], outs[0], outs[1]).astype(o_ref.dtype)


def _na_bias(rpb, rows):
    wr = min(WIN_ROWS, rows)
    qc = np.arange(GRID_W)
    col_off = np.clip(qc[None, :] - qc[:, None], -(WIN_COLS - 1), WIN_COLS - 1) + WIN_COLS - 1
    onehot = jnp.asarray(np.arange(2 * WIN_COLS - 1)[:, None, None] == col_off[None], F32)
    b = jnp.einsum('hrc,cqk->hrqk', rpb.astype(F32), onehot, precision=lax.Precision.HIGHEST)
    b = jnp.stack([b[:, ro0:ro0 + wr] for ro0 in range(WIN_ROWS)], axis=0)
    return jnp.transpose(b, (0, 1, 3, 2, 4)).reshape(WIN_ROWS, N_HEADS, GRID_W, wr * GRID_W)


def na_lat_attn(q, k, v, ck, cv, rpb, t, past):
    m = q.shape[0]
    rows = t // GRID_W
    assert rows >= WIN_ROWS
    n_seq = m // t
    n_loc = WIN_ROWS * GRID_W

    def bias_idx(b, r):
        return (_win_start(r, rows) - r + WIN_ROWS - 1, 0, 0, 0)

    return pl.pallas_call(
        functools.partial(_na_lat_kernel, rows=rows),
        out_shape=jax.ShapeDtypeStruct((m, D), BF16),
        grid=(n_seq, rows),
        in_specs=[pl.BlockSpec((GRID_W, D), lambda b, r: (b * rows + r, 0)),
                  pl.BlockSpec((t, D), lambda b, r: (b, 0)),
                  pl.BlockSpec((t, D), lambda b, r: (b, 0)),
                  pl.BlockSpec((past, D), lambda b, r: (b, 0)),
                  pl.BlockSpec((past, D), lambda b, r: (b, 0)),
                  pl.BlockSpec((1, N_HEADS, GRID_W, n_loc), bias_idx)],
        out_specs=pl.BlockSpec((GRID_W, D), lambda b, r: (b * rows + r, 0)),
        compiler_params=_cp(("parallel", "arbitrary")), name="na_lat_attn",
    )(q, k, v, ck, cv, _na_bias(rpb, rows))


def _sort16_net():
    def merge(lo, hi, r):
        step = r * 2
        if step < hi - lo:
            yield from merge(lo, hi, step)
            yield from merge(lo + r, hi, step)
            yield from ((i, i + r) for i in range(lo + r, hi - r, step))
        else:
            yield (lo, lo + r)

    def sort(lo, hi):
        if hi - lo >= 1:
            mid = lo + (hi - lo) // 2
            yield from sort(lo, mid)
            yield from sort(mid + 1, hi)
            yield from merge(lo, hi, 1)

    return tuple(sort(0, PEER_TOPK - 1))


_SORT16 = _sort16_net()
_BITONIC16 = tuple((i, i + d) for d in (8, 4, 2, 1) for i in range(PEER_TOPK) if not i & d)


def _exchange(x, net):
    x = list(x)
    for i, j in net:
        x[i], x[j] = jnp.maximum(x[i], x[j]), jnp.minimum(x[i], x[j])
    return x


def _merge_top16(a, b):
    return _exchange([jnp.maximum(a[i], b[PEER_TOPK - 1 - i]) for i in range(PEER_TOPK)], _BITONIC16)


def _merge_sublanes(x):
    for shift in (4, 2, 1):
        x = _merge_top16(x, [pltpu.roll(v, shift, axis=0) for v in x])
    return x


def _peer_stats_kernel(q_ref, keys_ref, cut_out, e1_out, tab_out):
    tm = q_ref.shape[0]
    sub = lax.broadcasted_iota(jnp.int32, (8, tm), 0)
    ninf = jnp.full((8, tm), -jnp.inf, F32)
    for h in range(PEER_HEADS):
        s, tops = [], []
        for c in range(2):
            hc = 2 * h + c
            q = q_ref[:, hc * PEER_KEYS:(hc + 1) * PEER_KEYS]
            sc = _dot_nt(keys_ref[hc], q)
            s.append(sc)
            groups = [sc[8 * i:8 * i + 8, :] for i in range(PEER_KEYS // 8)]
            tops.append(_merge_sublanes(_exchange(groups, _SORT16)))
        a1, a2 = tops
        a1col = a1[7]
        for jj in range(6, -1, -1):
            a1col = jnp.where(sub == jj, a1[jj], a1col)
        lists = [jnp.where(sub < min(8, PEER_TOPK // (k + 1)), a1col + a2[k], ninf) for k in range(PEER_TOPK)]
        tail = [a1[8 + k] + a2[0] for k in range(8)] + [ninf] * 8
        best = _merge_top16(_merge_sublanes(lists), tail)
        z = jnp.zeros((8, tm), F32)
        for b in best:
            z = z + jnp.exp(b - best[0])
        thr = best[PEER_TOPK - 1]
        m2 = a2[0][0:1, :]
        a1hi = a1[15]
        for jj in range(6, -1, -1):
            a1hi = jnp.where(sub == jj, a1[8 + jj], a1hi)
        ranked = [a1col, a1hi]
        cutc = [jnp.full((8, tm), jnp.inf, F32)] * 2
        for k in range(PEER_TOPK):
            e2k = jnp.exp(a2[k] - a2[0])
            cutc = [jnp.where(r + a2[k] >= thr, e2k, c) for r, c in zip(ranked, cutc)]
        cut = jnp.full((PEER_KEYS, tm), jnp.inf, F32)
        for jj in range(PEER_TOPK):
            cut = jnp.where(s[0] == a1[jj][0:1, :], cutc[jj // 8][jj % 8:jj % 8 + 1, :], cut)
        cut_out[h] = cut
        e1_out[h] = jnp.exp(s[0] - a1[0][0:1, :]) * (0.5 / z[0:1, :])
        e2 = jnp.exp(s[1] - m2)
        for lb in range(tm // 128):
            lanes = slice(lb * 128, (lb + 1) * 128)
            tab_out[lb, h] = e2[:, lanes].reshape(PEER_KEYS // 8, 8, 128)


def peer_stats(q, keys, tm=256):
    m = q.shape[0]
    nq = 2 * PEER_HEADS * PEER_KEYS
    sd = jax.ShapeDtypeStruct
    big = pl.BlockSpec((PEER_HEADS, PEER_KEYS, tm), lambda i: (0, 0, i))
    tab_shape = (PEER_HEADS, PEER_KEYS // 8, 8, 128)
    return pl.pallas_call(
        _peer_stats_kernel,
        out_shape=(sd((PEER_HEADS, PEER_KEYS, m), F32),) * 2 + (sd((m // 128,) + tab_shape, F32),),
        grid=(m // tm,),
        in_specs=[pl.BlockSpec((tm, nq), lambda i: (i, 0)),
                  pl.BlockSpec((2 * PEER_HEADS, PEER_KEYS, PEER_KEYS), lambda i: (0, 0, 0))],
        out_specs=(big,) * 2 + (pl.BlockSpec((tm // 128,) + tab_shape, lambda i: (i, 0, 0, 0, 0)),),
        compiler_params=_cp(("parallel",)), name="peer_stats",
    )(q, keys)


def _cast_t_kernel(x_ref, o_ref):
    o_ref[...] = x_ref[...].T.astype(o_ref.dtype)


def cast_transposed(w, te=512):
    n_l, n_e, _ = w.shape
    return pl.pallas_call(
        _cast_t_kernel,
        out_shape=jax.ShapeDtypeStruct((n_l, D, n_e), BF16),
        grid=(n_l, n_e // te),
        in_specs=[pl.BlockSpec((None, te, D), lambda l, e: (l, e, 0))],
        out_specs=pl.BlockSpec((None, D, te), lambda l, e: (l, 0, e)),
        compiler_params=_cp(("parallel", "parallel")), name="cast_transposed",
    )(w)


def _gelu_x2(x):
    return x * (1.0 + lax.erf(x * (1.0 / math.sqrt(2.0))))


def _peer_dense_kernel(xt_ref, u_ref, un_ref, vt_ref, vp_ref, cut_ref, e1_ref, tab_ref, res_ref, m_ref, o_ref,
                       acc_scr, a_scr, w_scr, *, tm, tn, sub, gate):
    j = pl.program_id(1)
    n_sub = tn // sub
    n_cb = tm // 128
    out_rows = D // n_cb

    def act_matmul(sb):
        return _dot(u_ref[sb * sub:(sb + 1) * sub, :], xt_ref[...])

    @pl.when(j == 0)
    def _():
        acc_scr[...] = jnp.zeros_like(acc_scr)
        w_scr[...] = jnp.zeros_like(w_scr)
        a_scr[...] = act_matmul(0)

    def weights(sb, cb, a):
        lanes = slice(cb * 128, (cb + 1) * 128)
        n_il = sub // PEER_KEYS
        n_rg = PEER_KEYS // PEER_ROWS
        parts = [[None] * n_rg for _ in range(n_il)]
        for rg in range(n_rg):
            grp = slice(rg * PEER_ROWS // 8, (rg + 1) * PEER_ROWS // 8)
            g = [jnp.zeros((PEER_ROWS, 128), F32)] * n_il
            for h in range(PEER_HEADS):
                e2 = tab_ref[cb, h, grp].reshape(PEER_ROWS, 128)
                for il in range(n_il):
                    i1 = sb * n_il + il
                    hit = e2 >= cut_ref[h, i1:i1 + 1, lanes]
                    g[il] = g[il] + jnp.where(hit, e2 * e1_ref[h, i1:i1 + 1, lanes], 0.0)
            for il in range(n_il):
                r0 = il * PEER_KEYS + rg * PEER_ROWS
                parts[il][rg] = (g[il] * _gelu_x2(a[r0:r0 + PEER_ROWS, lanes])).astype(BF16)
        return jnp.concatenate([p for row in parts for p in row], axis=0)

    def out_piece(sb, q, w):
        rows = slice(q * out_rows, (q + 1) * out_rows)
        if sb < 0:
            return _dot(vp_ref[rows, :], w)
        return _dot(vt_ref[rows, sb * sub:(sb + 1) * sub], w)

    a = a_scr[...]
    w_prev = w_scr[...]
    contrib = [None] * n_cb
    for sb in range(n_sub):
        cols = []
        for cb in range(n_cb):
            if cb == 0:
                a_next = act_matmul(sb + 1) if sb + 1 < n_sub else _dot(un_ref[...], xt_ref[...])
            piece = out_piece(sb - 1, cb, w_prev)
            contrib[cb] = piece if contrib[cb] is None else contrib[cb] + piece
            cols.append(weights(sb, cb, a))
        w_prev = jnp.concatenate(cols, axis=1)
        a = a_next
    for q in range(n_cb):
        acc_scr[q * out_rows:(q + 1) * out_rows, :] += contrib[q]
    a_scr[...] = a
    w_scr[...] = w_prev

    @pl.when(j == pl.num_programs(1) - 1)
    def _():
        for q in range(n_cb):
            acc_scr[q * out_rows:(q + 1) * out_rows, :] += out_piece(n_sub - 1, q, w_scr[...])
        o_ref[...] = res_ref[...] + m_ref[0, gate:gate + 1, :] * acc_scr[...].T


def peer_dense(xt, u, vt, layer, cut, e1, tab, res, mods, tpc, gate, tm=512, tn=1024, sub=256):
    m = xt.shape[1]
    n1 = tn // PEER_KEYS
    part = pl.BlockSpec((PEER_HEADS, n1, tm), lambda i, j: (0, j, i))
    full = pl.BlockSpec((tm // 128,) + tab.shape[1:], lambda i, j: (i, 0, 0, 0, 0))
    n_sub = tn // sub
    last_sub = N_EXPERTS // sub - 1
    return pl.pallas_call(
        functools.partial(_peer_dense_kernel, tm=tm, tn=tn, sub=sub, gate=gate),
        out_shape=jax.ShapeDtypeStruct((m, D), F32),
        grid=(m // tm, N_EXPERTS // tn),
        in_specs=[pl.BlockSpec((D, tm), lambda i, j: (0, i)),
                  pl.BlockSpec((None, tn, D), lambda i, j: (layer, j, 0)),
                  pl.BlockSpec((None, sub, D), lambda i, j: (layer, jnp.minimum((j + 1) * n_sub, last_sub), 0)),
                  pl.BlockSpec((None, D, tn), lambda i, j: (layer, 0, j)),
                  pl.BlockSpec((None, D, sub), lambda i, j: (layer, 0, jnp.maximum(j * n_sub - 1, 0))),
                  part, part, full,
                  pl.BlockSpec((tm, D), lambda i, j: (i, 0)),
                  pl.BlockSpec((1, N_MOD, D), lambda i, j: ((i * tm) // tpc, 0, 0))],
        out_specs=pl.BlockSpec((tm, D), lambda i, j: (i, 0)),
        scratch_shapes=[pltpu.VMEM((D, tm), F32), pltpu.VMEM((sub, tm), F32), pltpu.VMEM((sub, tm), BF16)],
        compiler_params=_cp(("parallel", "arbitrary")), name="peer_dense",
    )(xt, u, u, vt, vt, cut, e1, tab, res, mods)


def peer_layer(x, norm_g, mods, tpc, wq, keys, u, vt, layer):
    q, ht = norm_mod_q(x, norm_g, mods, tpc, 3, wq)
    cut, e1, tab = peer_stats(q, keys)
    return peer_dense(ht, u, vt, layer, cut, e1, tab, x, mods, tpc, 5)


def kernel(x_prompt, x_sample, c, state_wkv, cache_k, cache_v, c_ctx, ada_w, ada_b, norm_mix, norm_ffn, fnet_w_in, fnet_w_out, rwkv_mu, rwkv_w_r, rwkv_w_k, rwkv_w_v, rwkv_w_o, rwkv_w0, rwkv_w1, rwkv_w2, rwkv_a0, rwkv_a1, rwkv_a2, rwkv_g1, rwkv_g2, rwkv_k_k, rwkv_k_a, rwkv_r_k, rwkv_lnx_g, rwkv_lnx_b, na_w_qkv, na_w_o, na_rpb, peer_w_q, peer_sub_keys, peer_u, peer_v, final_norm):
    nb_c, t_c, _ = x_prompt.shape
    nb_s, t_s, _ = x_sample.shape
    depth = ada_w.shape[0]
    past = cache_k.shape[2]
    bf = lambda w: w.astype(BF16)

    cond = jnp.concatenate([c_ctx[None, :], c, jnp.zeros((16 - 1 - nb_s, D), F32)], axis=0)
    mods_all = ada_all(cond, ada_w, ada_b).reshape(depth, 16, N_MOD, D)

    xc = x_prompt.reshape(nb_c * t_c, D)
    xs = x_sample.reshape(nb_s * t_s, D)
    tpc_c, tpc_s = nb_c * t_c, t_s
    new_wkv, new_k, new_v = [], [], []
    u_all = bf(peer_u)
    vt_all = cast_transposed(peer_v)

    for l in range(depth):
        kind, j = l % 3, l // 3
        mc = mods_all[l, 0:1]
        ms = mods_all[l, 1:1 + nb_s]
        if kind == 0:
            w_in, w_out = bf(fnet_w_in[j]), bf(fnet_w_out[j])
            outs = []
            for x, m, tpc, t in ((xc, mc, tpc_c, t_c), (xs, ms, tpc_s, t_s)):
                f = fnet_dft(norm_mod_mm(x, norm_mix[l], m, tpc, 0, w_in), t)
                outs.append(matmul_res(f, w_out, x, m, tpc, 2))
            xc, xs = outs
        elif kind == 1:
            p = dict(mu=rwkv_mu[j], w_r=bf(rwkv_w_r[j]), w_k=bf(rwkv_w_k[j]), w_v=bf(rwkv_w_v[j]), w_o=bf(rwkv_w_o[j]),
                     w0=rwkv_w0[j], w1=bf(rwkv_w1[j]), w2=bf(rwkv_w2[j]), a0=rwkv_a0[j], a1=bf(rwkv_a1[j]),
                     a2=bf(rwkv_a2[j]), g1=bf(rwkv_g1[j]), g2=bf(rwkv_g2[j]), k_k=rwkv_k_k[j].reshape(1, D),
                     k_a=rwkv_k_a[j].reshape(1, D), r_k=rwkv_r_k[j].reshape(1, D), lnx_g=rwkv_lnx_g[j],
                     lnx_b=rwkv_lnx_b[j])
            s_zero = jnp.zeros((nb_c, 2, N_HEADS, HEAD, HEAD), F32)
            xc, sc = rwkv_layer(xc, norm_mix[l], mc, tpc_c, t_c, s_zero, p)
            xs, _ = rwkv_layer(xs, norm_mix[l], ms, tpc_s, t_s, state_wkv[:, j], p)
            new_wkv.append(sc)
        else:
            w_qkv, w_o = bf(na_w_qkv[j]), bf(na_w_o[j])
            qc, kc, vc, kc32, vc32 = qkv_proj(xc, norm_mix[l], mc, tpc_c, w_qkv, True)
            qs, ks, vs = qkv_proj(xs, norm_mix[l], ms, tpc_s, w_qkv, False)
            new_k.append(kc32.reshape(nb_c, t_c, N_HEADS, HEAD))
            new_v.append(vc32.reshape(nb_c, t_c, N_HEADS, HEAD))
            oc = na_ctx_attn(qc, kc, vc, t_c)
            os_ = na_lat_attn(qs, ks, vs, bf(cache_k[:, j]).reshape(nb_s * past, D),
                              bf(cache_v[:, j]).reshape(nb_s * past, D), na_rpb[j], t_s, past)
            xc = matmul_res(oc, w_o, xc, mc, tpc_c, 2)
            xs = matmul_res(os_, w_o, xs, ms, tpc_s, 2)
        wq = bf(peer_w_q[l])
        keys = bf(peer_sub_keys[l]).reshape(2 * PEER_HEADS, PEER_KEYS, PEER_KEYS)
        xc = peer_layer(xc, norm_ffn[l], mc, tpc_c, wq, keys, u_all, vt_all, l)
        xs = peer_layer(xs, norm_ffn[l], ms, tpc_s, wq, keys, u_all, vt_all, l)

    y_prompt = rms_final(xc, final_norm).reshape(nb_c, t_c, D)
    y_sample = rms_final(xs, final_norm).reshape(nb_s, t_s, D)
    return (y_prompt, y_sample, jnp.stack(new_wkv, axis=1), jnp.stack(new_k, axis=1), jnp.stack(new_v, axis=1))
```

```python
import functools
import math

import numpy as np
import jax
import jax.numpy as jnp
from jax import lax
from jax.experimental import pallas as pl
from jax.experimental.pallas import tpu as pltpu

F32 = jnp.float32
BF16 = jnp.bfloat16

D = 1024
N_MOD = 6
EPS = 1e-6
HEAD = 64
N_HEADS = D // HEAD
LNX_EPS = 64e-5
GRID_W = 64
WIN_ROWS = 8
WIN_COLS = 16
NA_SCALE = HEAD ** -0.5
NEG_INF = -1e30
FNET_GROUPS = 4
FNET_GD = D // FNET_GROUPS
PEER_KEYS = 128
PEER_HEADS = 8
PEER_TOPK = 16
N_EXPERTS = PEER_KEYS * PEER_KEYS
PEER_ROWS = 32
SCAN_L = 128

VMEM_LIMIT = 56 * 1024 * 1024


def _cp(sem, vmem=VMEM_LIMIT):
    return pltpu.CompilerParams(dimension_semantics=sem, vmem_limit_bytes=vmem)


def _dot(a, b):
    return jnp.dot(a, b, preferred_element_type=F32)


def _dot_nt(a, b):
    return lax.dot_general(a, b, (((1,), (1,)), ((), ())), preferred_element_type=F32)


def _split_dot(x, w):
    hi = x.astype(BF16)
    lo = (x - hi.astype(F32)).astype(BF16)
    return _dot(hi, w) + _dot(lo, w)


def _ada_kernel(c_ref, w_ref, b_ref, o_ref):
    c = c_ref[...]
    s = c * jax.nn.sigmoid(c)
    o_ref[0] = _dot(s.astype(BF16), w_ref[0].astype(BF16)) + b_ref[0]


def ada_all(cond16, ada_w, ada_b):
    depth = ada_w.shape[0]
    tn = 1024
    return pl.pallas_call(
        _ada_kernel,
        out_shape=jax.ShapeDtypeStruct((depth, 16, N_MOD * D), F32),
        grid=(depth, N_MOD * D // tn),
        in_specs=[pl.BlockSpec((16, D), lambda l, j: (0, 0)),
                  pl.BlockSpec((1, D, tn), lambda l, j: (l, 0, j)),
                  pl.BlockSpec((1, 1, tn), lambda l, j: (l, 0, j))],
        out_specs=pl.BlockSpec((1, 16, tn), lambda l, j: (l, 0, j)),
        compiler_params=_cp(("parallel", "parallel")),
        name="ada",
    )(cond16, ada_w, ada_b.reshape(depth, 1, N_MOD * D))


def _modulate(x, g, shift, scale):
    y = x * lax.rsqrt(jnp.mean(x * x, axis=-1, keepdims=True) + EPS)
    return (y * g) * (1 + scale) + shift


def _norm_mod_q_kernel(x_ref, g_ref, m_ref, w_ref, q_ref, ht_ref, *, which):
    h = _modulate(x_ref[...], g_ref[...], m_ref[0, which:which + 1, :], m_ref[0, which + 1:which + 2, :])
    q_ref[...] = _dot(h.astype(BF16), w_ref[...]).astype(q_ref.dtype)
    ht_ref[...] = h.T.astype(ht_ref.dtype)


def _norm_mod_specs(tm, tpc):
    return [pl.BlockSpec((tm, D), lambda i: (i, 0)),
            pl.BlockSpec((1, D), lambda i: (0, 0)),
            pl.BlockSpec((1, N_MOD, D), lambda i: ((i * tm) // tpc, 0, 0))]


def _norm_mod_mm_kernel(x_ref, g_ref, m_ref, w_ref, o_ref, *, which):
    h = _modulate(x_ref[...], g_ref[...], m_ref[0, which:which + 1, :], m_ref[0, which + 1:which + 2, :])
    o_ref[...] = _dot(h.astype(BF16), w_ref[...]).astype(o_ref.dtype)


def norm_mod_mm(x, g, mods, tpc, which, w, tm=512):
    m, n = x.shape[0], w.shape[1]
    return pl.pallas_call(
        functools.partial(_norm_mod_mm_kernel, which=which),
        out_shape=jax.ShapeDtypeStruct((m, n), BF16),
        grid=(m // tm,), in_specs=_norm_mod_specs(tm, tpc) + [pl.BlockSpec((D, n), lambda i: (0, 0))],
        out_specs=pl.BlockSpec((tm, n), lambda i: (i, 0)),
        compiler_params=_cp(("parallel",)), name="norm_mod_mm",
    )(x, g.reshape(1, D), mods, w)


def norm_mod_q(x, g, mods, tpc, which, w, tm=512):
    m, n = x.shape[0], w.shape[1]
    return pl.pallas_call(
        functools.partial(_norm_mod_q_kernel, which=which),
        out_shape=(jax.ShapeDtypeStruct((m, n), BF16), jax.ShapeDtypeStruct((D, m), BF16)),
        grid=(m // tm,), in_specs=_norm_mod_specs(tm, tpc) + [pl.BlockSpec((D, n), lambda i: (0, 0))],
        out_specs=(pl.BlockSpec((tm, n), lambda i: (i, 0)), pl.BlockSpec((D, tm), lambda i: (0, i))),
        compiler_params=_cp(("parallel",)), name="norm_mod_q",
    )(x, g.reshape(1, D), mods, w)


def _final_norm_kernel(x_ref, g_ref, o_ref):
    x = x_ref[...]
    o_ref[...] = x * lax.rsqrt(jnp.mean(x * x, axis=-1, keepdims=True) + EPS) * g_ref[...]


def rms_final(x, g, tm=512):
    m = x.shape[0]
    return pl.pallas_call(
        _final_norm_kernel,
        out_shape=jax.ShapeDtypeStruct((m, D), F32),
        grid=(m // tm,),
        in_specs=[pl.BlockSpec((tm, D), lambda i: (i, 0)), pl.BlockSpec((1, D), lambda i: (0, 0))],
        out_specs=pl.BlockSpec((tm, D), lambda i: (i, 0)),
        compiler_params=_cp(("parallel",)), name="final_norm",
    )(x, g.reshape(1, D))


def _mm_res_kernel(a_ref, w_ref, res_ref, m_ref, o_ref, *, gate):
    o_ref[...] = res_ref[...] + m_ref[0, gate:gate + 1, :] * _dot(a_ref[...], w_ref[...])


def matmul_res(a, w, res, mods, tpc, gate, tm=512):
    m = a.shape[0]
    return pl.pallas_call(
        functools.partial(_mm_res_kernel, gate=gate),
        out_shape=jax.ShapeDtypeStruct((m, D), F32),
        grid=(m // tm,),
        in_specs=[pl.BlockSpec((tm, D), lambda i: (i, 0)),
                  pl.BlockSpec((D, D), lambda i: (0, 0)),
                  pl.BlockSpec((tm, D), lambda i: (i, 0)),
                  pl.BlockSpec((1, N_MOD, D), lambda i: ((i * tm) // tpc, 0, 0))],
        out_specs=pl.BlockSpec((tm, D), lambda i: (i, 0)),
        compiler_params=_cp(("parallel",)), name="matmul_res",
    )(a, w, res, mods)


def _dft_mats(t):
    def cs(n):
        k = np.arange(n)
        ang = 2.0 * np.pi * ((k[:, None] * k[None, :]) % n) / n
        s = 1.0 / math.sqrt(n)
        return np.cos(ang) * s, np.sin(ang) * s
    cc, sc = cs(FNET_GD)
    ct, st = cs(t)
    return (jnp.asarray(np.concatenate([cc, sc], axis=1), BF16), jnp.asarray(ct, BF16), jnp.asarray(st, BF16))


def _dft_kernel(u_ref, cs_ref, ct_ref, st_ref, o_ref):
    p = _dot(u_ref[...], cs_ref[...])
    pc = p[:, :FNET_GD].astype(BF16)
    ps = p[:, FNET_GD:].astype(BF16)
    o_ref[...] = (_dot(ct_ref[...], pc) - _dot(st_ref[...], ps)).astype(o_ref.dtype)


def fnet_dft(u, t):
    m = u.shape[0]
    cs, ct, st = _dft_mats(t)
    return pl.pallas_call(
        _dft_kernel,
        out_shape=jax.ShapeDtypeStruct((m, D), BF16),
        grid=(m // t, FNET_GROUPS),
        in_specs=[pl.BlockSpec((t, FNET_GD), lambda s, g: (s, g)),
                  pl.BlockSpec((FNET_GD, 2 * FNET_GD), lambda s, g: (0, 0)),
                  pl.BlockSpec((t, t), lambda s, g: (0, 0)),
                  pl.BlockSpec((t, t), lambda s, g: (0, 0))],
        out_specs=pl.BlockSpec((t, FNET_GD), lambda s, g: (s, g)),
        compiler_params=_cp(("parallel", "parallel")), name="fnet_dft",
    )(u, cs, ct, st)


def _head_select():
    return jnp.asarray((np.arange(D) // HEAD)[:, None] == np.arange(128)[None, :], BF16)


def _head_sum(x, sel):
    c = _split_dot(x, sel)
    hi = c.astype(BF16)
    lo = (c - hi.astype(F32)).astype(BF16)
    return _dot_nt(hi, sel) + _dot_nt(lo, sel)


def _rwkv_proj_kernel(x_ref, xp_ref, xn_ref, g_ref, m_ref, mu_ref, wr_ref, wk_ref, wv_ref, g1_ref, g2_ref,
                      w0_ref, w1_ref, w2_ref, a0_ref, a1_ref, a2_ref, kk_ref, ka_ref, rk_ref, sel_ref,
                      r_out, v_out, kkn_out, g_out, bonus_out, lw_out, kd_out, bd_out, *, tm, t):
    i = pl.program_id(0)
    shift = m_ref[0, 0:1, :]
    scale = m_ref[0, 1:2, :]
    g = g_ref[...]
    h = _modulate(x_ref[...], g, shift, scale)
    first = (i * tm) % t == 0
    last = ((i + 1) * tm) % t == 0
    hp = jnp.where(first, 0.0, _modulate(xp_ref[7:8, :], g, shift, scale))
    hn = jnp.where(last, 0.0, _modulate(xn_ref[0:1, :], g, shift, scale))
    row = lax.broadcasted_iota(jnp.int32, (tm, 1), 0)
    prev = jnp.where(row == 0, hp, pltpu.roll(h, 1, axis=0))
    nxt = jnp.where(row == tm - 1, hn, pltpu.roll(h, tm - 1, axis=0))
    xx = 0.5 * (prev + nxt) - h

    def mix(j):
        return (h + xx * mu_ref[j:j + 1, :]).astype(BF16)

    r = _dot(mix(0), wr_ref[...])
    k = _dot(mix(2), wk_ref[...])
    v = _dot(mix(3), wv_ref[...])
    gate = _dot(jax.nn.sigmoid(_dot(mix(5), g1_ref[...])).astype(BF16), g2_ref[...])
    xw = mix(1)
    xa = mix(4)
    sel = sel_ref[...]
    kk = k * kk_ref[...]
    kk = kk * lax.rsqrt(_head_sum(kk * kk, sel) + 1e-12)
    ksum = jnp.zeros_like(k)
    for j in range(2):
        w_raw = w0_ref[j:j + 1, :] + _dot(jnp.tanh(_dot(xw, w1_ref[j])).astype(BF16), w2_ref[j])
        lw_out[j] = -jnp.exp(-jax.nn.softplus(-w_raw) - 0.5)
        a = jax.nn.sigmoid(a0_ref[j:j + 1, :] + _dot(_dot(xa, a1_ref[j]).astype(BF16), a2_ref[j]))
        kd = k * (1 + (a - 1) * ka_ref[...])
        kd_out[j] = kd
        bd_out[j] = kk * a
        ksum = ksum + kd
    r_out[...] = r
    v_out[...] = v
    kkn_out[...] = kk
    g_out[...] = gate
    bonus_out[...] = _head_sum(r * ksum * rk_ref[...], sel) * v


def rwkv_proj(x, norm_g, mods, tpc, t, p, tm=256):
    m = x.shape[0]
    nb8 = m // 8
    full = lambda *shape: pl.BlockSpec(shape, lambda i: (0,) * len(shape))
    tok = pl.BlockSpec((tm, D), lambda i: (i, 0))
    tok2 = pl.BlockSpec((2, tm, D), lambda i: (0, i, 0))
    in_specs = [tok,
                pl.BlockSpec((8, D), lambda i: (jnp.maximum(i * (tm // 8) - 1, 0), 0)),
                pl.BlockSpec((8, D), lambda i: (jnp.minimum((i + 1) * (tm // 8), nb8 - 1), 0)),
                full(1, D),
                pl.BlockSpec((1, N_MOD, D), lambda i: ((i * tm) // tpc, 0, 0)),
                full(6, D), full(D, D), full(D, D), full(D, D), full(D, 128), full(128, D),
                full(2, D), full(2, D, 64), full(2, 64, D), full(2, D), full(2, D, 64), full(2, 64, D),
                full(1, D), full(1, D), full(1, D), full(D, 128)]
    sd = jax.ShapeDtypeStruct
    return pl.pallas_call(
        functools.partial(_rwkv_proj_kernel, tm=tm, t=t),
        out_shape=(sd((m, D), F32),) * 5 + (sd((2, m, D), F32),) * 3,
        grid=(m // tm,), in_specs=in_specs,
        out_specs=(tok,) * 5 + (tok2,) * 3,
        compiler_params=_cp(("parallel",)), name="rwkv_proj",
    )(x, x, x, norm_g.reshape(1, D), mods, p["mu"], p["w_r"], p["w_k"], p["w_v"], p["g1"], p["g2"],
      p["w0"], p["w1"], p["w2"], p["a0"], p["a1"], p["a2"], p["k_k"], p["k_a"], p["r_k"], _head_select())


def _rwkv_scan_kernel(r_ref, v_ref, kk_ref, lw_ref, kd_ref, b_ref, z0_ref, y_ref, zout_ref, z_scr, *, n_chunks):
    L = SCAN_L
    d = pl.program_id(1)
    c = pl.program_id(2)

    row = lax.broadcasted_iota(jnp.int32, (L, L), 0)
    col = lax.broadcasted_iota(jnp.int32, (L, L), 1)

    @pl.when(c == 0)
    def _():
        dup = (lax.broadcasted_iota(jnp.int32, (HEAD, L), 1) % HEAD
               == lax.broadcasted_iota(jnp.int32, (HEAD, L), 0)).astype(BF16)
        for p in range(N_HEADS // 2):
            both = jnp.concatenate([_split3_dot(z0_ref[0, 0, 2 * p], dup), _split3_dot(z0_ref[0, 0, 2 * p + 1], dup)],
                                   axis=0)
            z_scr[p] = jnp.where((row // HEAD) == (col // HEAD), both, 0.0)

    fwd = d == 0
    order = (col - row) * (1 - 2 * d)
    before = order < 0
    upto = order <= 0
    cum_mat = upto.astype(BF16)
    same_head = (row // HEAD) == (col // HEAD)
    lane = lax.broadcasted_iota(jnp.int32, (1, 2 * HEAD), 1)
    head_mask = (lane < HEAD, lane >= HEAD)
    n_double = int(math.log2(L))

    def prepare(p):
        sl = slice(p * 2 * HEAD, (p + 1) * 2 * HEAD)
        lw = lw_ref[0, :, sl]
        cum = _split_dot_left(cum_mat, lw)
        tot = jnp.where(fwd, cum[L - 1:L, :], cum[0:1, :])
        inv = jnp.exp(-cum)
        ar = jnp.concatenate([-kk_ref[:, sl] * jnp.exp(cum - lw), r_ref[:, sl] * jnp.exp(cum)], axis=0)
        bk = jnp.concatenate([b_ref[0, :, sl] * inv, kd_ref[0, :, sl] * inv], axis=0).astype(BF16)
        z = z_scr[p]
        base = _dot_nt(ar.astype(BF16), z.astype(BF16))
        return dict(ar=ar, bk=bk, z=z, base=base, v=v_ref[:, sl], tot=tot)

    def start_chain(pp, hm):
        g4 = _dot_nt(jnp.where(hm, pp["ar"], 0.0).astype(BF16), pp["bk"])
        vm = jnp.where(hm, pp["v"], 0.0).astype(BF16)
        n = jnp.where(before, g4[:L, :L], 0.0).astype(BF16)
        x = jnp.where(hm, pp["base"][:L], 0.0) + _dot(jnp.where(before, g4[:L, L:], 0.0).astype(BF16), vm)
        out = jnp.concatenate([jnp.where(upto, g4[L:, :L], 0.0), jnp.where(upto, g4[L:, L:], 0.0)], axis=1)
        return dict(n=n, x=x, out=out.astype(BF16), vm=vm)

    pairs = [prepare(p) for p in range(N_HEADS // 2)]
    chains = [[start_chain(pp, hm) for hm in head_mask] for pp in pairs]
    for it in range(n_double):
        for ch in (ch for pair in chains for ch in pair):
            nb = ch["n"]
            xb = ch["x"].astype(BF16)
            if it + 1 < n_double:
                res = _dot(nb, jnp.concatenate([xb, nb], axis=1))
                ch["x"] = ch["x"] + res[:, :2 * HEAD]
                ch["n"] = res[:, 2 * HEAD:].astype(BF16)
            else:
                ch["x"] = ch["x"] + _dot(nb, xb)
    for p, (pp, pair) in enumerate(zip(pairs, chains)):
        y = pp["base"][L:]
        for ch in pair:
            y = y + _dot(ch["out"], jnp.concatenate([ch["x"].astype(BF16), ch["vm"]], axis=0))
        uv = jnp.concatenate([pair[0]["x"] + pair[1]["x"], pp["v"]], axis=0)
        inc = _dot(uv.T.astype(BF16), pp["bk"])
        z_scr[p] = jnp.where(same_head, pp["z"] + inc, 0.0) * jnp.exp(pp["tot"])
        y_ref[0, :, p * 2 * HEAD:(p + 1) * 2 * HEAD] = y

    @pl.when(c == n_chunks - 1)
    def _():
        prow = lax.broadcasted_iota(jnp.int32, (L, HEAD), 0)
        pcol = lax.broadcasted_iota(jnp.int32, (L, HEAD), 1)
        pick_a = (prow == pcol).astype(BF16)
        pick_b = (prow == pcol + HEAD).astype(BF16)
        for p in range(N_HEADS // 2):
            z = z_scr[p]
            zout_ref[0, 0, 2 * p] = _split3_dot(z[:HEAD, :], pick_a)
            zout_ref[0, 0, 2 * p + 1] = _split3_dot(z[HEAD:, :], pick_b)


def _split3_dot(x, w):
    hi = x.astype(BF16)
    r1 = x - hi.astype(F32)
    mid = r1.astype(BF16)
    lo = (r1 - mid.astype(F32)).astype(BF16)
    return _dot(hi, w) + _dot(mid, w) + _dot(lo, w)


def _split_dot_left(w, x):
    hi = x.astype(BF16)
    lo = (x - hi.astype(F32)).astype(BF16)
    return _dot(w, hi) + _dot(w, lo)


def rwkv_scan(r, v, kk, lw, kd, bd, z0, t):
    m = r.shape[0]
    n_seq = m // t
    nc = t // SCAN_L

    def blk(s, d, c):
        return s * nc + c + d * (nc - 1 - 2 * c)

    tok = pl.BlockSpec((SCAN_L, D), lambda s, d, c: (blk(s, d, c), 0))
    tok2 = pl.BlockSpec((1, SCAN_L, D), lambda s, d, c: (d, blk(s, d, c), 0))
    zspec = pl.BlockSpec((1, 1, N_HEADS, HEAD, HEAD), lambda s, d, c: (s, d, 0, 0, 0))
    return pl.pallas_call(
        functools.partial(_rwkv_scan_kernel, n_chunks=nc),
        out_shape=(jax.ShapeDtypeStruct((2, m, D), F32), jax.ShapeDtypeStruct(z0.shape, F32)),
        grid=(n_seq, 2, nc),
        in_specs=[tok, tok, tok, tok2, tok2, tok2, zspec],
        out_specs=(tok2, zspec),
        scratch_shapes=[pltpu.VMEM((N_HEADS // 2, 2 * HEAD, 2 * HEAD), F32)],
        compiler_params=_cp(("parallel", "parallel", "arbitrary")), name="rwkv_scan",
    )(r, v, kk, lw, kd, bd, z0)


def _rwkv_post_kernel(y_ref, bonus_ref, g_ref, lg_ref, lb_ref, sel_ref, o_ref):
    sel = sel_ref[...]
    o = y_ref[0] + y_ref[1]
    cen = o - _head_sum(o, sel) * (1.0 / HEAD)
    var = _head_sum(cen * cen, sel) * (1.0 / HEAD)
    o = cen * lax.rsqrt(var + LNX_EPS) * lg_ref[...] + lb_ref[...] + bonus_ref[...]
    o_ref[...] = (o * g_ref[...]).astype(o_ref.dtype)


def rwkv_post(y, bonus, g, lnx_g, lnx_b, tm=256):
    m = bonus.shape[0]
    tok = pl.BlockSpec((tm, D), lambda i: (i, 0))
    row = pl.BlockSpec((1, D), lambda i: (0, 0))
    return pl.pallas_call(
        _rwkv_post_kernel,
        out_shape=jax.ShapeDtypeStruct((m, D), BF16),
        grid=(m // tm,),
        in_specs=[pl.BlockSpec((2, tm, D), lambda i: (0, i, 0)), tok, tok, row, row,
                  pl.BlockSpec((D, 128), lambda i: (0, 0))],
        out_specs=tok,
        compiler_params=_cp(("parallel",)), name="rwkv_post",
    )(y, bonus, g, lnx_g.reshape(1, D), lnx_b.reshape(1, D), _head_select())


def rwkv_layer(x, norm_g, mods, tpc, t, s0, p):
    r, v, kk, g, bonus, lw, kd, bd = rwkv_proj(x, norm_g, mods, tpc, t, p)
    y, zf = rwkv_scan(r, v, kk, lw, kd, bd, s0, t)
    o = rwkv_post(y, bonus, g, p["lnx_g"], p["lnx_b"])
    return matmul_res(o, p["w_o"], x, mods, tpc, 2), zf


def _softmax_rows(parts):
    m = parts[0].max(axis=-1, keepdims=True)
    for s in parts[1:]:
        m = jnp.maximum(m, s.max(axis=-1, keepdims=True))
    es = [jnp.exp(s - m) for s in parts]
    den = es[0].sum(axis=-1, keepdims=True)
    for e in es[1:]:
        den = den + e.sum(axis=-1, keepdims=True)
    inv = 1.0 / den
    return [(e * inv).astype(BF16) for e in es]


def _qkv_kernel(x_ref, g_ref, m_ref, w_ref, *out_refs, with_f32):
    h = _modulate(x_ref[...], g_ref[...], m_ref[0, 0:1, :], m_ref[0, 1:2, :]).astype(BF16)
    for idx in range(3):
        r = _dot(h, w_ref[:, idx * D:(idx + 1) * D])
        out_refs[idx][...] = r.astype(BF16)
        if with_f32 and idx > 0:
            out_refs[2 + idx][...] = r


def qkv_proj(x, g, mods, tpc, w, with_f32, tm=512):
    m = x.shape[0]
    tok = pl.BlockSpec((tm, D), lambda i: (i, 0))
    sd = jax.ShapeDtypeStruct
    n_f32 = 2 if with_f32 else 0
    return pl.pallas_call(
        functools.partial(_qkv_kernel, with_f32=with_f32),
        out_shape=(sd((m, D), BF16),) * 3 + (sd((m, D), F32),) * n_f32,
        grid=(m // tm,),
        in_specs=_norm_mod_specs(tm, tpc) + [pl.BlockSpec((D, 3 * D), lambda i: (0, 0))],
        out_specs=(tok,) * (3 + n_f32),
        compiler_params=_cp(("parallel",)), name="qkv_proj",
    )(x, g.reshape(1, D), mods, w)


def _pair_masks():
    lane = lax.broadcasted_iota(jnp.int32, (1, 2 * HEAD), 1)
    return lane < HEAD, lane >= HEAD


def _na_ctx_kernel(q_ref, k_ref, v_ref, o_ref):
    masks = _pair_masks()
    zero = jnp.zeros((), BF16)
    scores = []
    for p in range(N_HEADS // 2):
        sl = slice(p * 2 * HEAD, (p + 1) * 2 * HEAD)
        q = q_ref[:, sl]
        k = k_ref[:, sl]
        scores.append([_dot_nt(jnp.where(hm, q, zero), k) * NA_SCALE for hm in masks])
    probs = [[_softmax_rows([s])[0] for s in pair] for pair in scores]
    for p in range(N_HEADS // 2):
        sl = slice(p * 2 * HEAD, (p + 1) * 2 * HEAD)
        v = v_ref[:, sl]
        o_ref[:, sl] = jnp.where(masks[0], _dot(probs[p][0], v), _dot(probs[p][1], v)).astype(o_ref.dtype)


def na_ctx_attn(q, k, v, t):
    m = q.shape[0]
    seq = pl.BlockSpec((t, D), lambda b: (b, 0))
    return pl.pallas_call(
        _na_ctx_kernel,
        out_shape=jax.ShapeDtypeStruct((m, D), BF16),
        grid=(m // t,),
        in_specs=[seq, seq, seq],
        out_specs=seq,
        compiler_params=_cp(("parallel",)), name="na_ctx_attn",
    )(q, k, v)


def _win_start(r, rows):
    return jnp.clip(r - WIN_ROWS // 2, 0, rows - WIN_ROWS)


def _na_lat_kernel(q_ref, k_ref, v_ref, ck_ref, cv_ref, bias_ref, o_ref, *, rows):
    r = pl.program_id(1)
    start = pl.multiple_of(_win_start(r, rows) * GRID_W, GRID_W)
    n_loc = WIN_ROWS * GRID_W
    qc = lax.broadcasted_iota(jnp.int32, (GRID_W, n_loc), 0)
    kc = lax.broadcasted_iota(jnp.int32, (GRID_W, n_loc), 1) % GRID_W
    cs = jnp.clip(qc - WIN_COLS // 2, 0, GRID_W - WIN_COLS)
    valid = (kc >= cs) & (kc < cs + WIN_COLS)
    masks = _pair_masks()
    zero = jnp.zeros((), BF16)
    scores = []
    for p in range(N_HEADS // 2):
        sl = slice(p * 2 * HEAD, (p + 1) * 2 * HEAD)
        q = q_ref[:, sl]
        kw = k_ref[pl.ds(start, n_loc), sl]
        ck = ck_ref[:, sl]
        pair = []
        for i, hm in enumerate(masks):
            qm = jnp.where(hm, q, zero)
            s_loc = jnp.where(valid, _dot_nt(qm, kw) * NA_SCALE + bias_ref[0, 2 * p + i], NEG_INF)
            pair.append([s_loc, _dot_nt(qm, ck) * NA_SCALE])
        scores.append(pair)
    probs = [[_softmax_rows(parts) for parts in pair] for pair in scores]
    for p in range(N_HEADS // 2):
        sl = slice(p * 2 * HEAD, (p + 1) * 2 * HEAD)
        vw = v_ref[pl.ds(start, n_loc), sl]
        cv = cv_ref[:, sl]
        outs = [_dot(p_loc, vw) + _dot(p_ctx, cv) for p_loc, p_ctx in probs[p]]
        o_ref[:, sl] = jnp.where(masks[0], outs[0], outs[1]).astype(o_ref.dtype)


def _na_bias(rpb, rows):
    wr = min(WIN_ROWS, rows)
    qc = np.arange(GRID_W)
    col_off = np.clip(qc[None, :] - qc[:, None], -(WIN_COLS - 1), WIN_COLS - 1) + WIN_COLS - 1
    onehot = jnp.asarray(np.arange(2 * WIN_COLS - 1)[:, None, None] == col_off[None], F32)
    b = jnp.einsum('hrc,cqk->hrqk', rpb.astype(F32), onehot, precision=lax.Precision.HIGHEST)
    b = jnp.stack([b[:, ro0:ro0 + wr] for ro0 in range(WIN_ROWS)], axis=0)
    return jnp.transpose(b, (0, 1, 3, 2, 4)).reshape(WIN_ROWS, N_HEADS, GRID_W, wr * GRID_W)


def na_lat_attn(q, k, v, ck, cv, rpb, t, past):
    m = q.shape[0]
    rows = t // GRID_W
    assert rows >= WIN_ROWS
    n_seq = m // t
    n_loc = WIN_ROWS * GRID_W

    def bias_idx(b, r):
        return (_win_start(r, rows) - r + WIN_ROWS - 1, 0, 0, 0)

    return pl.pallas_call(
        functools.partial(_na_lat_kernel, rows=rows),
        out_shape=jax.ShapeDtypeStruct((m, D), BF16),
        grid=(n_seq, rows),
        in_specs=[pl.BlockSpec((GRID_W, D), lambda b, r: (b * rows + r, 0)),
                  pl.BlockSpec((t, D), lambda b, r: (b, 0)),
                  pl.BlockSpec((t, D), lambda b, r: (b, 0)),
                  pl.BlockSpec((past, D), lambda b, r: (b, 0)),
                  pl.BlockSpec((past, D), lambda b, r: (b, 0)),
                  pl.BlockSpec((1, N_HEADS, GRID_W, n_loc), bias_idx)],
        out_specs=pl.BlockSpec((GRID_W, D), lambda b, r: (b * rows + r, 0)),
        compiler_params=_cp(("parallel", "arbitrary")), name="na_lat_attn",
    )(q, k, v, ck, cv, _na_bias(rpb, rows))


def _sort16_net():
    def merge(lo, hi, r):
        step = r * 2
        if step < hi - lo:
            yield from merge(lo, hi, step)
            yield from merge(lo + r, hi, step)
            yield from ((i, i + r) for i in range(lo + r, hi - r, step))
        else:
            yield (lo, lo + r)

    def sort(lo, hi):
        if hi - lo >= 1:
            mid = lo + (hi - lo) // 2
            yield from sort(lo, mid)
            yield from sort(mid + 1, hi)
            yield from merge(lo, hi, 1)

    return tuple(sort(0, PEER_TOPK - 1))


_SORT16 = _sort16_net()
_BITONIC16 = tuple((i, i + d) for d in (8, 4, 2, 1) for i in range(PEER_TOPK) if not i & d)


def _exchange(x, net):
    x = list(x)
    for i, j in net:
        x[i], x[j] = jnp.maximum(x[i], x[j]), jnp.minimum(x[i], x[j])
    return x


def _merge_top16(a, b):
    return _exchange([jnp.maximum(a[i], b[PEER_TOPK - 1 - i]) for i in range(PEER_TOPK)], _BITONIC16)


def _merge_sublanes(x):
    for shift in (4, 2, 1):
        x = _merge_top16(x, [pltpu.roll(v, shift, axis=0) for v in x])
    return x


def _peer_stats_kernel(q_ref, keys_ref, cut_out, e1_out, tab_out):
    tm = q_ref.shape[0]
    sub = lax.broadcasted_iota(jnp.int32, (8, tm), 0)
    ninf = jnp.full((8, tm), -jnp.inf, F32)
    for h in range(PEER_HEADS):
        s, tops = [], []
        for c in range(2):
            hc = 2 * h + c
            q = q_ref[:, hc * PEER_KEYS:(hc + 1) * PEER_KEYS]
            sc = _dot_nt(keys_ref[hc], q)
            s.append(sc)
            groups = [sc[8 * i:8 * i + 8, :] for i in range(PEER_KEYS // 8)]
            tops.append(_merge_sublanes(_exchange(groups, _SORT16)))
        a1, a2 = tops
        a1col = a1[7]
        for jj in range(6, -1, -1):
            a1col = jnp.where(sub == jj, a1[jj], a1col)
        lists = [jnp.where(sub < min(8, PEER_TOPK // (k + 1)), a1col + a2[k], ninf) for k in range(PEER_TOPK)]
        tail = [a1[8 + k] + a2[0] for k in range(8)] + [ninf] * 8
        best = _merge_top16(_merge_sublanes(lists), tail)
        z = jnp.zeros((8, tm), F32)
        for b in best:
            z = z + jnp.exp(b - best[0])
        thr = best[PEER_TOPK - 1]
        m2 = a2[0][0:1, :]
        a1hi = a1[15]
        for jj in range(6, -1, -1):
            a1hi = jnp.where(sub == jj, a1[8 + jj], a1hi)
        ranked = [a1col, a1hi]
        cutc = [jnp.full((8, tm), jnp.inf, F32)] * 2
        for k in range(PEER_TOPK):
            e2k = jnp.exp(a2[k] - a2[0])
            cutc = [jnp.where(r + a2[k] >= thr, e2k, c) for r, c in zip(ranked, cutc)]
        cut = jnp.full((PEER_KEYS, tm), jnp.inf, F32)
        for jj in range(PEER_TOPK):
            cut = jnp.where(s[0] == a1[jj][0:1, :], cutc[jj // 8][jj % 8:jj % 8 + 1, :], cut)
        cut_out[h] = cut
        e1_out[h] = jnp.exp(s[0] - a1[0][0:1, :]) * (0.5 / z[0:1, :])
        e2 = jnp.exp(s[1] - m2)
        for lb in range(tm // 128):
            lanes = slice(lb * 128, (lb + 1) * 128)
            tab_out[lb, h] = e2[:, lanes].reshape(PEER_KEYS // 8, 8, 128)


def peer_stats(q, keys, tm=256):
    m = q.shape[0]
    nq = 2 * PEER_HEADS * PEER_KEYS
    sd = jax.ShapeDtypeStruct
    big = pl.BlockSpec((PEER_HEADS, PEER_KEYS, tm), lambda i: (0, 0, i))
    tab_shape = (PEER_HEADS, PEER_KEYS // 8, 8, 128)
    return pl.pallas_call(
        _peer_stats_kernel,
        out_shape=(sd((PEER_HEADS, PEER_KEYS, m), F32),) * 2 + (sd((m // 128,) + tab_shape, F32),),
        grid=(m // tm,),
        in_specs=[pl.BlockSpec((tm, nq), lambda i: (i, 0)),
                  pl.BlockSpec((2 * PEER_HEADS, PEER_KEYS, PEER_KEYS), lambda i: (0, 0, 0))],
        out_specs=(big,) * 2 + (pl.BlockSpec((tm // 128,) + tab_shape, lambda i: (i, 0, 0, 0, 0)),),
        compiler_params=_cp(("parallel",)), name="peer_stats",
    )(q, keys)


def _cast_t_kernel(x_ref, o_ref):
    o_ref[...] = x_ref[...].T.astype(o_ref.dtype)


def cast_transposed(w, te=512):
    n_l, n_e, _ = w.shape
    return pl.pallas_call(
        _cast_t_kernel,
        out_shape=jax.ShapeDtypeStruct((n_l, D, n_e), BF16),
        grid=(n_l, n_e // te),
        in_specs=[pl.BlockSpec((None, te, D), lambda l, e: (l, e, 0))],
        out_specs=pl.BlockSpec((None, D, te), lambda l, e: (l, 0, e)),
        compiler_params=_cp(("parallel", "parallel")), name="cast_transposed",
    )(w)


def _gelu_x2(x):
    return x * (1.0 + lax.erf(x * (1.0 / math.sqrt(2.0))))


def _peer_dense_kernel(xt_ref, u_ref, un_ref, vt_ref, vp_ref, cut_ref, e1_ref, tab_ref, res_ref, m_ref, o_ref,
                       acc_scr, a_scr, w_scr, *, tm, tn, sub, gate):
    j = pl.program_id(1)
    n_sub = tn // sub
    n_cb = tm // 128
    out_rows = D // n_cb

    def act_matmul(sb):
        return _dot(u_ref[sb * sub:(sb + 1) * sub, :], xt_ref[...])

    @pl.when(j == 0)
    def _():
        acc_scr[...] = jnp.zeros_like(acc_scr)
        w_scr[...] = jnp.zeros_like(w_scr)
        a_scr[...] = act_matmul(0)

    def weights(sb, cb, a):
        lanes = slice(cb * 128, (cb + 1) * 128)
        n_il = sub // PEER_KEYS
        n_rg = PEER_KEYS // PEER_ROWS
        parts = [[None] * n_rg for _ in range(n_il)]
        for rg in range(n_rg):
            grp = slice(rg * PEER_ROWS // 8, (rg + 1) * PEER_ROWS // 8)
            g = [jnp.zeros((PEER_ROWS, 128), F32)] * n_il
            for h in range(PEER_HEADS):
                e2 = tab_ref[cb, h, grp].reshape(PEER_ROWS, 128)
                for il in range(n_il):
                    i1 = sb * n_il + il
                    hit = e2 >= cut_ref[h, i1:i1 + 1, lanes]
                    g[il] = g[il] + jnp.where(hit, e2 * e1_ref[h, i1:i1 + 1, lanes], 0.0)
            for il in range(n_il):
                r0 = il * PEER_KEYS + rg * PEER_ROWS
                parts[il][rg] = (g[il] * _gelu_x2(a[r0:r0 + PEER_ROWS, lanes])).astype(BF16)
        return jnp.concatenate([p for row in parts for p in row], axis=0)

    def out_piece(sb, q, w):
        rows = slice(q * out_rows, (q + 1) * out_rows)
        if sb < 0:
            return _dot(vp_ref[rows, :], w)
        return _dot(vt_ref[rows, sb * sub:(sb + 1) * sub], w)

    a = a_scr[...]
    w_prev = w_scr[...]
    contrib = [None] * n_cb
    for sb in range(n_sub):
        cols = []
        for cb in range(n_cb):
            if cb == 0:
                a_next = act_matmul(sb + 1) if sb + 1 < n_sub else _dot(un_ref[...], xt_ref[...])
            piece = out_piece(sb - 1, cb, w_prev)
            contrib[cb] = piece if contrib[cb] is None else contrib[cb] + piece
            cols.append(weights(sb, cb, a))
        w_prev = jnp.concatenate(cols, axis=1)
        a = a_next
    for q in range(n_cb):
        acc_scr[q * out_rows:(q + 1) * out_rows, :] += contrib[q]
    a_scr[...] = a
    w_scr[...] = w_prev

    @pl.when(j == pl.num_programs(1) - 1)
    def _():
        for q in range(n_cb):
            acc_scr[q * out_rows:(q + 1) * out_rows, :] += out_piece(n_sub - 1, q, w_scr[...])
        o_ref[...] = res_ref[...] + m_ref[0, gate:gate + 1, :] * acc_scr[...].T


def peer_dense(xt, u, vt, layer, cut, e1, tab, res, mods, tpc, gate, tm=512, tn=2048, sub=256):
    m = xt.shape[1]
    n1 = tn // PEER_KEYS
    part = pl.BlockSpec((PEER_HEADS, n1, tm), lambda i, j: (0, j, i))
    full = pl.BlockSpec((tm // 128,) + tab.shape[1:], lambda i, j: (i, 0, 0, 0, 0))
    n_sub = tn // sub
    last_sub = N_EXPERTS // sub - 1
    return pl.pallas_call(
        functools.partial(_peer_dense_kernel, tm=tm, tn=tn, sub=sub, gate=gate),
        out_shape=jax.ShapeDtypeStruct((m, D), F32),
        grid=(m // tm, N_EXPERTS // tn),
        in_specs=[pl.BlockSpec((D, tm), lambda i, j: (0, i)),
                  pl.BlockSpec((None, tn, D), lambda i, j: (layer, j, 0)),
                  pl.BlockSpec((None, sub, D), lambda i, j: (layer, jnp.minimum((j + 1) * n_sub, last_sub), 0)),
                  pl.BlockSpec((None, D, tn), lambda i, j: (layer, 0, j)),
                  pl.BlockSpec((None, D, sub), lambda i, j: (layer, 0, jnp.maximum(j * n_sub - 1, 0))),
                  part, part, full,
                  pl.BlockSpec((tm, D), lambda i, j: (i, 0)),
                  pl.BlockSpec((1, N_MOD, D), lambda i, j: ((i * tm) // tpc, 0, 0))],
        out_specs=pl.BlockSpec((tm, D), lambda i, j: (i, 0)),
        scratch_shapes=[pltpu.VMEM((D, tm), F32), pltpu.VMEM((sub, tm), F32), pltpu.VMEM((sub, tm), BF16)],
        compiler_params=_cp(("parallel", "arbitrary")), name="peer_dense",
    )(xt, u, u, vt, vt, cut, e1, tab, res, mods)


def peer_layer(x, norm_g, mods, tpc, wq, keys, u, vt, layer):
    q, ht = norm_mod_q(x, norm_g, mods, tpc, 3, wq)
    cut, e1, tab = peer_stats(q, keys)
    return peer_dense(ht, u, vt, layer, cut, e1, tab, x, mods, tpc, 5)


def kernel(x_prompt, x_sample, c, state_wkv, cache_k, cache_v, c_ctx, ada_w, ada_b, norm_mix, norm_ffn, fnet_w_in, fnet_w_out, rwkv_mu, rwkv_w_r, rwkv_w_k, rwkv_w_v, rwkv_w_o, rwkv_w0, rwkv_w1, rwkv_w2, rwkv_a0, rwkv_a1, rwkv_a2, rwkv_g1, rwkv_g2, rwkv_k_k, rwkv_k_a, rwkv_r_k, rwkv_lnx_g, rwkv_lnx_b, na_w_qkv, na_w_o, na_rpb, peer_w_q, peer_sub_keys, peer_u, peer_v, final_norm):
    nb_c, t_c, _ = x_prompt.shape
    nb_s, t_s, _ = x_sample.shape
    depth = ada_w.shape[0]
    past = cache_k.shape[2]
    bf = lambda w: w.astype(BF16)

    cond = jnp.concatenate([c_ctx[None, :], c, jnp.zeros((16 - 1 - nb_s, D), F32)], axis=0)
    mods_all = ada_all(cond, ada_w, ada_b).reshape(depth, 16, N_MOD, D)

    xc = x_prompt.reshape(nb_c * t_c, D)
    xs = x_sample.reshape(nb_s * t_s, D)
    tpc_c, tpc_s = nb_c * t_c, t_s
    new_wkv, new_k, new_v = [], [], []
    u_all = bf(peer_u)
    vt_all = cast_transposed(peer_v)

    for l in range(depth):
        kind, j = l % 3, l // 3
        mc = mods_all[l, 0:1]
        ms = mods_all[l, 1:1 + nb_s]
        if kind == 0:
            w_in, w_out = bf(fnet_w_in[j]), bf(fnet_w_out[j])
            outs = []
            for x, m, tpc, t in ((xc, mc, tpc_c, t_c), (xs, ms, tpc_s, t_s)):
                f = fnet_dft(norm_mod_mm(x, norm_mix[l], m, tpc, 0, w_in), t)
                outs.append(matmul_res(f, w_out, x, m, tpc, 2))
            xc, xs = outs
        elif kind == 1:
            p = dict(mu=rwkv_mu[j], w_r=bf(rwkv_w_r[j]), w_k=bf(rwkv_w_k[j]), w_v=bf(rwkv_w_v[j]), w_o=bf(rwkv_w_o[j]),
                     w0=rwkv_w0[j], w1=bf(rwkv_w1[j]), w2=bf(rwkv_w2[j]), a0=rwkv_a0[j], a1=bf(rwkv_a1[j]),
                     a2=bf(rwkv_a2[j]), g1=bf(rwkv_g1[j]), g2=bf(rwkv_g2[j]), k_k=rwkv_k_k[j].reshape(1, D),
                     k_a=rwkv_k_a[j].reshape(1, D), r_k=rwkv_r_k[j].reshape(1, D), lnx_g=rwkv_lnx_g[j],
                     lnx_b=rwkv_lnx_b[j])
            s_zero = jnp.zeros((nb_c, 2, N_HEADS, HEAD, HEAD), F32)
            xc, sc = rwkv_layer(xc, norm_mix[l], mc, tpc_c, t_c, s_zero, p)
            xs, _ = rwkv_layer(xs, norm_mix[l], ms, tpc_s, t_s, state_wkv[:, j], p)
            new_wkv.append(sc)
        else:
            w_qkv, w_o = bf(na_w_qkv[j]), bf(na_w_o[j])
            qc, kc, vc, kc32, vc32 = qkv_proj(xc, norm_mix[l], mc, tpc_c, w_qkv, True)
            qs, ks, vs = qkv_proj(xs, norm_mix[l], ms, tpc_s, w_qkv, False)
            new_k.append(kc32.reshape(nb_c, t_c, N_HEADS, HEAD))
            new_v.append(vc32.reshape(nb_c, t_c, N_HEADS, HEAD))
            oc = na_ctx_attn(qc, kc, vc, t_c)
            os_ = na_lat_attn(qs, ks, vs, bf(cache_k[:, j]).reshape(nb_s * past, D),
                              bf(cache_v[:, j]).reshape(nb_s * past, D), na_rpb[j], t_s, past)
            xc = matmul_res(oc, w_o, xc, mc, tpc_c, 2)
            xs = matmul_res(os_, w_o, xs, ms, tpc_s, 2)
        wq = bf(peer_w_q[l])
        keys = bf(peer_sub_keys[l]).reshape(2 * PEER_HEADS, PEER_KEYS, PEER_KEYS)
        xc = peer_layer(xc, norm_ffn[l], mc, tpc_c, wq, keys, u_all, vt_all, l)
        xs = peer_layer(xs, norm_ffn[l], ms, tpc_s, wq, keys, u_all, vt_all, l)

    y_prompt = rms_final(xc, final_norm).reshape(nb_c, t_c, D)
    y_sample = rms_final(xs, final_norm).reshape(nb_s, t_s, D)
    return (y_prompt, y_sample, jnp.stack(new_wkv, axis=1), jnp.stack(new_k, axis=1), jnp.stack(new_v, axis=1))
```

```python
import functools
import math

import numpy as np
import jax
import jax.numpy as jnp
from jax import lax
from jax.experimental import pallas as pl
from jax.experimental.pallas import tpu as pltpu

F32 = jnp.float32
BF16 = jnp.bfloat16

D = 1024
N_MOD = 6
EPS = 1e-6
HEAD = 64
N_HEADS = D // HEAD
LNX_EPS = 64e-5
GRID_W = 64
WIN_ROWS = 8
WIN_COLS = 16
NA_SCALE = HEAD ** -0.5
NEG_INF = -1e30
FNET_GROUPS = 4
FNET_GD = D // FNET_GROUPS
PEER_KEYS = 128
PEER_HEADS = 8
PEER_TOPK = 16
N_EXPERTS = PEER_KEYS * PEER_KEYS
PEER_ROWS = 32
SCAN_L = 128

VMEM_LIMIT = 56 * 1024 * 1024
PEER_VMEM_LIMIT = 58 * 1024 * 1024


def _cp(sem, vmem=VMEM_LIMIT):
    return pltpu.CompilerParams(dimension_semantics=sem, vmem_limit_bytes=vmem)


def _dot(a, b):
    return jnp.dot(a, b, preferred_element_type=F32)


def _dot_nt(a, b):
    return lax.dot_general(a, b, (((1,), (1,)), ((), ())), preferred_element_type=F32)


def _split_dot(x, w):
    hi = x.astype(BF16)
    lo = (x - hi.astype(F32)).astype(BF16)
    return _dot(hi, w) + _dot(lo, w)


def _ada_kernel(c_ref, w_ref, b_ref, o_ref):
    c = c_ref[...]
    s = c * jax.nn.sigmoid(c)
    o_ref[0] = _dot(s.astype(BF16), w_ref[0].astype(BF16)) + b_ref[0]


def ada_all(cond16, ada_w, ada_b):
    depth = ada_w.shape[0]
    tn = 1024
    return pl.pallas_call(
        _ada_kernel,
        out_shape=jax.ShapeDtypeStruct((depth, 16, N_MOD * D), F32),
        grid=(depth, N_MOD * D // tn),
        in_specs=[pl.BlockSpec((16, D), lambda l, j: (0, 0)),
                  pl.BlockSpec((1, D, tn), lambda l, j: (l, 0, j)),
                  pl.BlockSpec((1, 1, tn), lambda l, j: (l, 0, j))],
        out_specs=pl.BlockSpec((1, 16, tn), lambda l, j: (l, 0, j)),
        compiler_params=_cp(("parallel", "parallel")),
        name="ada",
    )(cond16, ada_w, ada_b.reshape(depth, 1, N_MOD * D))


def _modulate(x, g, shift, scale):
    y = x * lax.rsqrt(jnp.mean(x * x, axis=-1, keepdims=True) + EPS)
    return (y * g) * (1 + scale) + shift


def _norm_mod_q_kernel(x_ref, g_ref, m_ref, w_ref, q_ref, ht_ref, *, which):
    h = _modulate(x_ref[...], g_ref[...], m_ref[0, which:which + 1, :], m_ref[0, which + 1:which + 2, :])
    q_ref[...] = _dot(h.astype(BF16), w_ref[...]).astype(q_ref.dtype)
    ht_ref[...] = h.T.astype(ht_ref.dtype)


def _norm_mod_specs(tm, tpc):
    return [pl.BlockSpec((tm, D), lambda i: (i, 0)),
            pl.BlockSpec((1, D), lambda i: (0, 0)),
            pl.BlockSpec((1, N_MOD, D), lambda i: ((i * tm) // tpc, 0, 0))]


def _norm_mod_mm_kernel(x_ref, g_ref, m_ref, w_ref, o_ref, *, which):
    h = _modulate(x_ref[...], g_ref[...], m_ref[0, which:which + 1, :], m_ref[0, which + 1:which + 2, :])
    o_ref[...] = _dot(h.astype(BF16), w_ref[...]).astype(o_ref.dtype)


def norm_mod_mm(x, g, mods, tpc, which, w, tm=512):
    m, n = x.shape[0], w.shape[1]
    return pl.pallas_call(
        functools.partial(_norm_mod_mm_kernel, which=which),
        out_shape=jax.ShapeDtypeStruct((m, n), BF16),
        grid=(m // tm,), in_specs=_norm_mod_specs(tm, tpc) + [pl.BlockSpec((D, n), lambda i: (0, 0))],
        out_specs=pl.BlockSpec((tm, n), lambda i: (i, 0)),
        compiler_params=_cp(("parallel",)), name="norm_mod_mm",
    )(x, g.reshape(1, D), mods, w)


def norm_mod_q(x, g, mods, tpc, which, w, tm=512):
    m, n = x.shape[0], w.shape[1]
    return pl.pallas_call(
        functools.partial(_norm_mod_q_kernel, which=which),
        out_shape=(jax.ShapeDtypeStruct((m, n), BF16), jax.ShapeDtypeStruct((D, m), BF16)),
        grid=(m // tm,), in_specs=_norm_mod_specs(tm, tpc) + [pl.BlockSpec((D, n), lambda i: (0, 0))],
        out_specs=(pl.BlockSpec((tm, n), lambda i: (i, 0)), pl.BlockSpec((D, tm), lambda i: (0, i))),
        compiler_params=_cp(("parallel",)), name="norm_mod_q",
    )(x, g.reshape(1, D), mods, w)


def _final_norm_kernel(x_ref, g_ref, o_ref):
    x = x_ref[...]
    o_ref[...] = x * lax.rsqrt(jnp.mean(x * x, axis=-1, keepdims=True) + EPS) * g_ref[...]


def rms_final(x, g, tm=512):
    m = x.shape[0]
    return pl.pallas_call(
        _final_norm_kernel,
        out_shape=jax.ShapeDtypeStruct((m, D), F32),
        grid=(m // tm,),
        in_specs=[pl.BlockSpec((tm, D), lambda i: (i, 0)), pl.BlockSpec((1, D), lambda i: (0, 0))],
        out_specs=pl.BlockSpec((tm, D), lambda i: (i, 0)),
        compiler_params=_cp(("parallel",)), name="final_norm",
    )(x, g.reshape(1, D))


def _mm_res_kernel(a_ref, w_ref, res_ref, m_ref, o_ref, *, gate):
    o_ref[...] = res_ref[...] + m_ref[0, gate:gate + 1, :] * _dot(a_ref[...], w_ref[...])


def matmul_res(a, w, res, mods, tpc, gate, tm=512):
    m = a.shape[0]
    return pl.pallas_call(
        functools.partial(_mm_res_kernel, gate=gate),
        out_shape=jax.ShapeDtypeStruct((m, D), F32),
        grid=(m // tm,),
        in_specs=[pl.BlockSpec((tm, D), lambda i: (i, 0)),
                  pl.BlockSpec((D, D), lambda i: (0, 0)),
                  pl.BlockSpec((tm, D), lambda i: (i, 0)),
                  pl.BlockSpec((1, N_MOD, D), lambda i: ((i * tm) // tpc, 0, 0))],
        out_specs=pl.BlockSpec((tm, D), lambda i: (i, 0)),
        compiler_params=_cp(("parallel",)), name="matmul_res",
    )(a, w, res, mods)


def _dft_mats(t):
    def cs(n):
        k = np.arange(n)
        ang = 2.0 * np.pi * ((k[:, None] * k[None, :]) % n) / n
        s = 1.0 / math.sqrt(n)
        return np.cos(ang) * s, np.sin(ang) * s
    cc, sc = cs(FNET_GD)
    ct, st = cs(t)
    return (jnp.asarray(np.concatenate([cc, sc], axis=1), BF16), jnp.asarray(ct, BF16), jnp.asarray(st, BF16))


def _dft_kernel(u_ref, cs_ref, ct_ref, st_ref, o_ref):
    p = _dot(u_ref[...], cs_ref[...])
    pc = p[:, :FNET_GD].astype(BF16)
    ps = p[:, FNET_GD:].astype(BF16)
    o_ref[...] = (_dot(ct_ref[...], pc) - _dot(st_ref[...], ps)).astype(o_ref.dtype)


def fnet_dft(u, t):
    m = u.shape[0]
    cs, ct, st = _dft_mats(t)
    return pl.pallas_call(
        _dft_kernel,
        out_shape=jax.ShapeDtypeStruct((m, D), BF16),
        grid=(m // t, FNET_GROUPS),
        in_specs=[pl.BlockSpec((t, FNET_GD), lambda s, g: (s, g)),
                  pl.BlockSpec((FNET_GD, 2 * FNET_GD), lambda s, g: (0, 0)),
                  pl.BlockSpec((t, t), lambda s, g: (0, 0)),
                  pl.BlockSpec((t, t), lambda s, g: (0, 0))],
        out_specs=pl.BlockSpec((t, FNET_GD), lambda s, g: (s, g)),
        compiler_params=_cp(("parallel", "parallel")), name="fnet_dft",
    )(u, cs, ct, st)


def _head_select():
    return jnp.asarray((np.arange(D) // HEAD)[:, None] == np.arange(128)[None, :], BF16)


def _head_sum(x, sel):
    c = _split_dot(x, sel)
    hi = c.astype(BF16)
    lo = (c - hi.astype(F32)).astype(BF16)
    return _dot_nt(hi, sel) + _dot_nt(lo, sel)


def _rwkv_proj_kernel(x_ref, xp_ref, xn_ref, g_ref, m_ref, mu_ref, wr_ref, wk_ref, wv_ref, g1_ref, g2_ref,
                      w0_ref, w1_ref, w2_ref, a0_ref, a1_ref, a2_ref, kk_ref, ka_ref, rk_ref, sel_ref,
                      r_out, v_out, kkn_out, g_out, bonus_out, lw_out, kd_out, bd_out, *, tm, t):
    i = pl.program_id(0)
    shift = m_ref[0, 0:1, :]
    scale = m_ref[0, 1:2, :]
    g = g_ref[...]
    h = _modulate(x_ref[...], g, shift, scale)
    first = (i * tm) % t == 0
    last = ((i + 1) * tm) % t == 0
    hp = jnp.where(first, 0.0, _modulate(xp_ref[7:8, :], g, shift, scale))
    hn = jnp.where(last, 0.0, _modulate(xn_ref[0:1, :], g, shift, scale))
    row = lax.broadcasted_iota(jnp.int32, (tm, 1), 0)
    prev = jnp.where(row == 0, hp, pltpu.roll(h, 1, axis=0))
    nxt = jnp.where(row == tm - 1, hn, pltpu.roll(h, tm - 1, axis=0))
    xx = 0.5 * (prev + nxt) - h

    def mix(j):
        return (h + xx * mu_ref[j:j + 1, :]).astype(BF16)

    r = _dot(mix(0), wr_ref[...])
    k = _dot(mix(2), wk_ref[...])
    v = _dot(mix(3), wv_ref[...])
    gate = _dot(jax.nn.sigmoid(_dot(mix(5), g1_ref[...])).astype(BF16), g2_ref[...])
    xw = mix(1)
    xa = mix(4)
    sel = sel_ref[...]
    kk = k * kk_ref[...]
    kk = kk * lax.rsqrt(_head_sum(kk * kk, sel) + 1e-12)
    ksum = jnp.zeros_like(k)
    for j in range(2):
        w_raw = w0_ref[j:j + 1, :] + _dot(jnp.tanh(_dot(xw, w1_ref[j])).astype(BF16), w2_ref[j])
        lw_out[j] = -jnp.exp(-jax.nn.softplus(-w_raw) - 0.5)
        a = jax.nn.sigmoid(a0_ref[j:j + 1, :] + _dot(_dot(xa, a1_ref[j]).astype(BF16), a2_ref[j]))
        kd = k * (1 + (a - 1) * ka_ref[...])
        kd_out[j] = kd
        bd_out[j] = kk * a
        ksum = ksum + kd
    r_out[...] = r
    v_out[...] = v
    kkn_out[...] = kk
    g_out[...] = gate
    bonus_out[...] = _head_sum(r * ksum * rk_ref[...], sel) * v


def rwkv_proj(x, norm_g, mods, tpc, t, p, tm=256):
    m = x.shape[0]
    nb8 = m // 8
    full = lambda *shape: pl.BlockSpec(shape, lambda i: (0,) * len(shape))
    tok = pl.BlockSpec((tm, D), lambda i: (i, 0))
    tok2 = pl.BlockSpec((2, tm, D), lambda i: (0, i, 0))
    in_specs = [tok,
                pl.BlockSpec((8, D), lambda i: (jnp.maximum(i * (tm // 8) - 1, 0), 0)),
                pl.BlockSpec((8, D), lambda i: (jnp.minimum((i + 1) * (tm // 8), nb8 - 1), 0)),
                full(1, D),
                pl.BlockSpec((1, N_MOD, D), lambda i: ((i * tm) // tpc, 0, 0)),
                full(6, D), full(D, D), full(D, D), full(D, D), full(D, 128), full(128, D),
                full(2, D), full(2, D, 64), full(2, 64, D), full(2, D), full(2, D, 64), full(2, 64, D),
                full(1, D), full(1, D), full(1, D), full(D, 128)]
    sd = jax.ShapeDtypeStruct
    return pl.pallas_call(
        functools.partial(_rwkv_proj_kernel, tm=tm, t=t),
        out_shape=(sd((m, D), F32),) * 5 + (sd((2, m, D), F32),) * 3,
        grid=(m // tm,), in_specs=in_specs,
        out_specs=(tok,) * 5 + (tok2,) * 3,
        compiler_params=_cp(("parallel",)), name="rwkv_proj",
    )(x, x, x, norm_g.reshape(1, D), mods, p["mu"], p["w_r"], p["w_k"], p["w_v"], p["g1"], p["g2"],
      p["w0"], p["w1"], p["w2"], p["a0"], p["a1"], p["a2"], p["k_k"], p["k_a"], p["r_k"], _head_select())


def _rwkv_scan_kernel(r_ref, v_ref, kk_ref, lw_ref, kd_ref, b_ref, z0_ref, y_ref, zout_ref, z_scr, *, n_chunks):
    L = SCAN_L
    d = pl.program_id(1)
    c = pl.program_id(2)

    row = lax.broadcasted_iota(jnp.int32, (L, L), 0)
    col = lax.broadcasted_iota(jnp.int32, (L, L), 1)

    @pl.when(c == 0)
    def _():
        dup = (lax.broadcasted_iota(jnp.int32, (HEAD, L), 1) % HEAD
               == lax.broadcasted_iota(jnp.int32, (HEAD, L), 0)).astype(BF16)
        for p in range(N_HEADS // 2):
            both = jnp.concatenate([_split3_dot(z0_ref[0, 0, 2 * p], dup), _split3_dot(z0_ref[0, 0, 2 * p + 1], dup)],
                                   axis=0)
            z_scr[p] = jnp.where((row // HEAD) == (col // HEAD), both, 0.0)

    fwd = d == 0
    order = (col - row) * (1 - 2 * d)
    before = order < 0
    upto = order <= 0
    cum_mat = upto.astype(BF16)
    same_head = (row // HEAD) == (col // HEAD)
    lane = lax.broadcasted_iota(jnp.int32, (1, 2 * HEAD), 1)
    head_mask = (lane < HEAD, lane >= HEAD)
    n_double = int(math.log2(L))

    def prepare(p):
        sl = slice(p * 2 * HEAD, (p + 1) * 2 * HEAD)
        lw = lw_ref[0, :, sl]
        cum = _split_dot_left(cum_mat, lw)
        tot = jnp.where(fwd, cum[L - 1:L, :], cum[0:1, :])
        inv = jnp.exp(-cum)
        ar = jnp.concatenate([-kk_ref[:, sl] * jnp.exp(cum - lw), r_ref[:, sl] * jnp.exp(cum)], axis=0)
        bk = jnp.concatenate([b_ref[0, :, sl] * inv, kd_ref[0, :, sl] * inv], axis=0).astype(BF16)
        z = z_scr[p]
        base = _dot_nt(ar.astype(BF16), z.astype(BF16))
        return dict(ar=ar, bk=bk, z=z, base=base, v=v_ref[:, sl], tot=tot)

    def start_chain(pp, hm):
        g4 = _dot_nt(jnp.where(hm, pp["ar"], 0.0).astype(BF16), pp["bk"])
        vm = jnp.where(hm, pp["v"], 0.0).astype(BF16)
        n = jnp.where(before, g4[:L, :L], 0.0).astype(BF16)
        x = jnp.where(hm, pp["base"][:L], 0.0) + _dot(jnp.where(before, g4[:L, L:], 0.0).astype(BF16), vm)
        out = jnp.concatenate([jnp.where(upto, g4[L:, :L], 0.0), jnp.where(upto, g4[L:, L:], 0.0)], axis=1)
        return dict(n=n, x=x, out=out.astype(BF16), vm=vm)

    pairs = [prepare(p) for p in range(N_HEADS // 2)]
    chains = [[start_chain(pp, hm) for hm in head_mask] for pp in pairs]
    for it in range(n_double):
        for ch in (ch for pair in chains for ch in pair):
            nb = ch["n"]
            xb = ch["x"].astype(BF16)
            if it + 1 < n_double:
                res = _dot(nb, jnp.concatenate([xb, nb], axis=1))
                ch["x"] = ch["x"] + res[:, :2 * HEAD]
                ch["n"] = res[:, 2 * HEAD:].astype(BF16)
            else:
                ch["x"] = ch["x"] + _dot(nb, xb)
    for p, (pp, pair) in enumerate(zip(pairs, chains)):
        y = pp["base"][L:]
        for ch in pair:
            y = y + _dot(ch["out"], jnp.concatenate([ch["x"].astype(BF16), ch["vm"]], axis=0))
        uv = jnp.concatenate([pair[0]["x"] + pair[1]["x"], pp["v"]], axis=0)
        inc = _dot(uv.T.astype(BF16), pp["bk"])
        z_scr[p] = jnp.where(same_head, pp["z"] + inc, 0.0) * jnp.exp(pp["tot"])
        y_ref[0, :, p * 2 * HEAD:(p + 1) * 2 * HEAD] = y

    @pl.when(c == n_chunks - 1)
    def _():
        prow = lax.broadcasted_iota(jnp.int32, (L, HEAD), 0)
        pcol = lax.broadcasted_iota(jnp.int32, (L, HEAD), 1)
        pick_a = (prow == pcol).astype(BF16)
        pick_b = (prow == pcol + HEAD).astype(BF16)
        for p in range(N_HEADS // 2):
            z = z_scr[p]
            zout_ref[0, 0, 2 * p] = _split3_dot(z[:HEAD, :], pick_a)
            zout_ref[0, 0, 2 * p + 1] = _split3_dot(z[HEAD:, :], pick_b)


def _split3_dot(x, w):
    hi = x.astype(BF16)
    r1 = x - hi.astype(F32)
    mid = r1.astype(BF16)
    lo = (r1 - mid.astype(F32)).astype(BF16)
    return _dot(hi, w) + _dot(mid, w) + _dot(lo, w)


def _split_dot_left(w, x):
    hi = x.astype(BF16)
    lo = (x - hi.astype(F32)).astype(BF16)
    return _dot(w, hi) + _dot(w, lo)


def rwkv_scan(r, v, kk, lw, kd, bd, z0, t):
    m = r.shape[0]
    n_seq = m // t
    nc = t // SCAN_L

    def blk(s, d, c):
        return s * nc + c + d * (nc - 1 - 2 * c)

    tok = pl.BlockSpec((SCAN_L, D), lambda s, d, c: (blk(s, d, c), 0))
    tok2 = pl.BlockSpec((1, SCAN_L, D), lambda s, d, c: (d, blk(s, d, c), 0))
    zspec = pl.BlockSpec((1, 1, N_HEADS, HEAD, HEAD), lambda s, d, c: (s, d, 0, 0, 0))
    return pl.pallas_call(
        functools.partial(_rwkv_scan_kernel, n_chunks=nc),
        out_shape=(jax.ShapeDtypeStruct((2, m, D), F32), jax.ShapeDtypeStruct(z0.shape, F32)),
        grid=(n_seq, 2, nc),
        in_specs=[tok, tok, tok, tok2, tok2, tok2, zspec],
        out_specs=(tok2, zspec),
        scratch_shapes=[pltpu.VMEM((N_HEADS // 2, 2 * HEAD, 2 * HEAD), F32)],
        compiler_params=_cp(("parallel", "parallel", "arbitrary")), name="rwkv_scan",
    )(r, v, kk, lw, kd, bd, z0)


def _rwkv_post_kernel(y_ref, bonus_ref, g_ref, lg_ref, lb_ref, sel_ref, o_ref):
    sel = sel_ref[...]
    o = y_ref[0] + y_ref[1]
    cen = o - _head_sum(o, sel) * (1.0 / HEAD)
    var = _head_sum(cen * cen, sel) * (1.0 / HEAD)
    o = cen * lax.rsqrt(var + LNX_EPS) * lg_ref[...] + lb_ref[...] + bonus_ref[...]
    o_ref[...] = (o * g_ref[...]).astype(o_ref.dtype)


def rwkv_post(y, bonus, g, lnx_g, lnx_b, tm=256):
    m = bonus.shape[0]
    tok = pl.BlockSpec((tm, D), lambda i: (i, 0))
    row = pl.BlockSpec((1, D), lambda i: (0, 0))
    return pl.pallas_call(
        _rwkv_post_kernel,
        out_shape=jax.ShapeDtypeStruct((m, D), BF16),
        grid=(m // tm,),
        in_specs=[pl.BlockSpec((2, tm, D), lambda i: (0, i, 0)), tok, tok, row, row,
                  pl.BlockSpec((D, 128), lambda i: (0, 0))],
        out_specs=tok,
        compiler_params=_cp(("parallel",)), name="rwkv_post",
    )(y, bonus, g, lnx_g.reshape(1, D), lnx_b.reshape(1, D), _head_select())


def rwkv_layer(x, norm_g, mods, tpc, t, s0, p):
    r, v, kk, g, bonus, lw, kd, bd = rwkv_proj(x, norm_g, mods, tpc, t, p)
    y, zf = rwkv_scan(r, v, kk, lw, kd, bd, s0, t)
    o = rwkv_post(y, bonus, g, p["lnx_g"], p["lnx_b"])
    return matmul_res(o, p["w_o"], x, mods, tpc, 2), zf


def _softmax_rows(parts):
    m = parts[0].max(axis=-1, keepdims=True)
    for s in parts[1:]:
        m = jnp.maximum(m, s.max(axis=-1, keepdims=True))
    es = [jnp.exp(s - m) for s in parts]
    den = es[0].sum(axis=-1, keepdims=True)
    for e in es[1:]:
        den = den + e.sum(axis=-1, keepdims=True)
    inv = 1.0 / den
    return [(e * inv).astype(BF16) for e in es]


def _qkv_kernel(x_ref, g_ref, m_ref, w_ref, *out_refs, with_f32):
    h = _modulate(x_ref[...], g_ref[...], m_ref[0, 0:1, :], m_ref[0, 1:2, :]).astype(BF16)
    for idx in range(3):
        r = _dot(h, w_ref[:, idx * D:(idx + 1) * D])
        out_refs[idx][...] = r.astype(BF16)
        if with_f32 and idx > 0:
            out_refs[2 + idx][...] = r


def qkv_proj(x, g, mods, tpc, w, with_f32, tm=512):
    m = x.shape[0]
    tok = pl.BlockSpec((tm, D), lambda i: (i, 0))
    sd = jax.ShapeDtypeStruct
    n_f32 = 2 if with_f32 else 0
    return pl.pallas_call(
        functools.partial(_qkv_kernel, with_f32=with_f32),
        out_shape=(sd((m, D), BF16),) * 3 + (sd((m, D), F32),) * n_f32,
        grid=(m // tm,),
        in_specs=_norm_mod_specs(tm, tpc) + [pl.BlockSpec((D, 3 * D), lambda i: (0, 0))],
        out_specs=(tok,) * (3 + n_f32),
        compiler_params=_cp(("parallel",)), name="qkv_proj",
    )(x, g.reshape(1, D), mods, w)


def _pair_masks():
    lane = lax.broadcasted_iota(jnp.int32, (1, 2 * HEAD), 1)
    return lane < HEAD, lane >= HEAD


def _na_ctx_kernel(q_ref, k_ref, v_ref, o_ref):
    masks = _pair_masks()
    zero = jnp.zeros((), BF16)
    scores = []
    for p in range(N_HEADS // 2):
        sl = slice(p * 2 * HEAD, (p + 1) * 2 * HEAD)
        q = q_ref[:, sl]
        k = k_ref[:, sl]
        scores.append([_dot_nt(jnp.where(hm, q, zero), k) * NA_SCALE for hm in masks])
    probs = [[_softmax_rows([s])[0] for s in pair] for pair in scores]
    for p in range(N_HEADS // 2):
        sl = slice(p * 2 * HEAD, (p + 1) * 2 * HEAD)
        v = v_ref[:, sl]
        o_ref[:, sl] = jnp.where(masks[0], _dot(probs[p][0], v), _dot(probs[p][1], v)).astype(o_ref.dtype)


def na_ctx_attn(q, k, v, t):
    m = q.shape[0]
    seq = pl.BlockSpec((t, D), lambda b: (b, 0))
    return pl.pallas_call(
        _na_ctx_kernel,
        out_shape=jax.ShapeDtypeStruct((m, D), BF16),
        grid=(m // t,),
        in_specs=[seq, seq, seq],
        out_specs=seq,
        compiler_params=_cp(("parallel",)), name="na_ctx_attn",
    )(q, k, v)


def _win_start(r, rows):
    return jnp.clip(r - WIN_ROWS // 2, 0, rows - WIN_ROWS)


def _na_lat_kernel(q_ref, k_ref, v_ref, ck_ref, cv_ref, bias_ref, o_ref, *, rows):
    r = pl.program_id(1)
    start = pl.multiple_of(_win_start(r, rows) * GRID_W, GRID_W)
    n_loc = WIN_ROWS * GRID_W
    qc = lax.broadcasted_iota(jnp.int32, (GRID_W, n_loc), 0)
    kc = lax.broadcasted_iota(jnp.int32, (GRID_W, n_loc), 1) % GRID_W
    cs = jnp.clip(qc - WIN_COLS // 2, 0, GRID_W - WIN_COLS)
    valid = (kc >= cs) & (kc < cs + WIN_COLS)
    masks = _pair_masks()
    zero = jnp.zeros((), BF16)
    scores = []
    for p in range(N_HEADS // 2):
        sl = slice(p * 2 * HEAD, (p + 1) * 2 * HEAD)
        q = q_ref[:, sl]
        kw = k_ref[pl.ds(start, n_loc), sl]
        ck = ck_ref[:, sl]
        pair = []
        for i, hm in enumerate(masks):
            qm = jnp.where(hm, q, zero)
            s_loc = jnp.where(valid, _dot_nt(qm, kw) * NA_SCALE + bias_ref[0, 2 * p + i], NEG_INF)
            pair.append([s_loc, _dot_nt(qm, ck) * NA_SCALE])
        scores.append(pair)
    probs = [[_softmax_rows(parts) for parts in pair] for pair in scores]
    for p in range(N_HEADS // 2):
        sl = slice(p * 2 * HEAD, (p + 1) * 2 * HEAD)
        vw = v_ref[pl.ds(start, n_loc), sl]
        cv = cv_ref[:, sl]
        outs = [_dot(p_loc, vw) + _dot(p_ctx, cv) for p_loc, p_ctx in probs[p]]
        o_ref[:, sl] = jnp.where(masks[0], outs[0], outs[1]).astype(o_ref.dtype)


def _na_bias(rpb, rows):
    wr = min(WIN_ROWS, rows)
    qc = np.arange(GRID_W)
    col_off = np.clip(qc[None, :] - qc[:, None], -(WIN_COLS - 1), WIN_COLS - 1) + WIN_COLS - 1
    onehot = jnp.asarray(np.arange(2 * WIN_COLS - 1)[:, None, None] == col_off[None], F32)
    b = jnp.einsum('hrc,cqk->hrqk', rpb.astype(F32), onehot, precision=lax.Precision.HIGHEST)
    b = jnp.stack([b[:, ro0:ro0 + wr] for ro0 in range(WIN_ROWS)], axis=0)
    return jnp.transpose(b, (0, 1, 3, 2, 4)).reshape(WIN_ROWS, N_HEADS, GRID_W, wr * GRID_W)


def na_lat_attn(q, k, v, ck, cv, rpb, t, past):
    m = q.shape[0]
    rows = t // GRID_W
    assert rows >= WIN_ROWS
    n_seq = m // t
    n_loc = WIN_ROWS * GRID_W

    def bias_idx(b, r):
        return (_win_start(r, rows) - r + WIN_ROWS - 1, 0, 0, 0)

    return pl.pallas_call(
        functools.partial(_na_lat_kernel, rows=rows),
        out_shape=jax.ShapeDtypeStruct((m, D), BF16),
        grid=(n_seq, rows),
        in_specs=[pl.BlockSpec((GRID_W, D), lambda b, r: (b * rows + r, 0)),
                  pl.BlockSpec((t, D), lambda b, r: (b, 0)),
                  pl.BlockSpec((t, D), lambda b, r: (b, 0)),
                  pl.BlockSpec((past, D), lambda b, r: (b, 0)),
                  pl.BlockSpec((past, D), lambda b, r: (b, 0)),
                  pl.BlockSpec((1, N_HEADS, GRID_W, n_loc), bias_idx)],
        out_specs=pl.BlockSpec((GRID_W, D), lambda b, r: (b * rows + r, 0)),
        compiler_params=_cp(("parallel", "arbitrary")), name="na_lat_attn",
    )(q, k, v, ck, cv, _na_bias(rpb, rows))


def _sort16_net():
    def merge(lo, hi, r):
        step = r * 2
        if step < hi - lo:
            yield from merge(lo, hi, step)
            yield from merge(lo + r, hi, step)
            yield from ((i, i + r) for i in range(lo + r, hi - r, step))
        else:
            yield (lo, lo + r)

    def sort(lo, hi):
        if hi - lo >= 1:
            mid = lo + (hi - lo) // 2
            yield from sort(lo, mid)
            yield from sort(mid + 1, hi)
            yield from merge(lo, hi, 1)

    return tuple(sort(0, PEER_TOPK - 1))


_SORT16 = _sort16_net()
_BITONIC16 = tuple((i, i + d) for d in (8, 4, 2, 1) for i in range(PEER_TOPK) if not i & d)


def _exchange(x, net):
    x = list(x)
    for i, j in net:
        x[i], x[j] = jnp.maximum(x[i], x[j]), jnp.minimum(x[i], x[j])
    return x


def _merge_top16(a, b):
    return _exchange([jnp.maximum(a[i], b[PEER_TOPK - 1 - i]) for i in range(PEER_TOPK)], _BITONIC16)


def _merge_sublanes(x):
    for shift in (4, 2, 1):
        x = _merge_top16(x, [pltpu.roll(v, shift, axis=0) for v in x])
    return x


def _peer_stats_kernel(q_ref, keys_ref, cut_out, e1_out, tab_out):
    tm = q_ref.shape[0]
    sub = lax.broadcasted_iota(jnp.int32, (8, tm), 0)
    ninf = jnp.full((8, tm), -jnp.inf, F32)
    for h in range(PEER_HEADS):
        s, tops = [], []
        for c in range(2):
            hc = 2 * h + c
            q = q_ref[:, hc * PEER_KEYS:(hc + 1) * PEER_KEYS]
            sc = _dot_nt(keys_ref[hc], q)
            s.append(sc)
            groups = [sc[8 * i:8 * i + 8, :] for i in range(PEER_KEYS // 8)]
            tops.append(_merge_sublanes(_exchange(groups, _SORT16)))
        a1, a2 = tops
        a1col = a1[7]
        for jj in range(6, -1, -1):
            a1col = jnp.where(sub == jj, a1[jj], a1col)
        lists = [jnp.where(sub < min(8, PEER_TOPK // (k + 1)), a1col + a2[k], ninf) for k in range(PEER_TOPK)]
        tail = [a1[8 + k] + a2[0] for k in range(8)] + [ninf] * 8
        best = _merge_top16(_merge_sublanes(lists), tail)
        z = jnp.zeros((8, tm), F32)
        for b in best:
            z = z + jnp.exp(b - best[0])
        thr = best[PEER_TOPK - 1]
        m2 = a2[0][0:1, :]
        a1hi = a1[15]
        for jj in range(6, -1, -1):
            a1hi = jnp.where(sub == jj, a1[8 + jj], a1hi)
        ranked = [a1col, a1hi]
        cutc = [jnp.full((8, tm), jnp.inf, F32)] * 2
        for k in range(PEER_TOPK):
            e2k = jnp.exp(a2[k] - a2[0])
            cutc = [jnp.where(r + a2[k] >= thr, e2k, c) for r, c in zip(ranked, cutc)]
        cut = jnp.full((PEER_KEYS, tm), jnp.inf, F32)
        for jj in range(PEER_TOPK):
            cut = jnp.where(s[0] == a1[jj][0:1, :], cutc[jj // 8][jj % 8:jj % 8 + 1, :], cut)
        cut_out[h] = cut
        e1_out[h] = jnp.exp(s[0] - a1[0][0:1, :]) * (0.5 / z[0:1, :])
        e2 = jnp.exp(s[1] - m2)
        for lb in range(tm // 128):
            lanes = slice(lb * 128, (lb + 1) * 128)
            tab_out[lb, h] = e2[:, lanes].reshape(PEER_KEYS // 8, 8, 128)


def peer_stats(q, keys, tm=256):
    m = q.shape[0]
    nq = 2 * PEER_HEADS * PEER_KEYS
    sd = jax.ShapeDtypeStruct
    big = pl.BlockSpec((PEER_HEADS, PEER_KEYS, tm), lambda i: (0, 0, i))
    tab_shape = (PEER_HEADS, PEER_KEYS // 8, 8, 128)
    return pl.pallas_call(
        _peer_stats_kernel,
        out_shape=(sd((PEER_HEADS, PEER_KEYS, m), F32),) * 2 + (sd((m // 128,) + tab_shape, F32),),
        grid=(m // tm,),
        in_specs=[pl.BlockSpec((tm, nq), lambda i: (i, 0)),
                  pl.BlockSpec((2 * PEER_HEADS, PEER_KEYS, PEER_KEYS), lambda i: (0, 0, 0))],
        out_specs=(big,) * 2 + (pl.BlockSpec((tm // 128,) + tab_shape, lambda i: (i, 0, 0, 0, 0)),),
        compiler_params=_cp(("parallel",)), name="peer_stats",
    )(q, keys)


def _cast_t_kernel(x_ref, o_ref):
    o_ref[...] = x_ref[...].T.astype(o_ref.dtype)


def cast_transposed(w, te=512):
    n_l, n_e, _ = w.shape
    return pl.pallas_call(
        _cast_t_kernel,
        out_shape=jax.ShapeDtypeStruct((n_l, D, n_e), BF16),
        grid=(n_l, n_e // te),
        in_specs=[pl.BlockSpec((None, te, D), lambda l, e: (l, e, 0))],
        out_specs=pl.BlockSpec((None, D, te), lambda l, e: (l, 0, e)),
        compiler_params=_cp(("parallel", "parallel")), name="cast_transposed",
    )(w)


def _gelu_x2(x):
    return x * (1.0 + lax.erf(x * (1.0 / math.sqrt(2.0))))


def _peer_dense_kernel(xt_ref, u_ref, un_ref, vt_ref, vp_ref, cut_ref, e1_ref, tab_ref, res_ref, m_ref, o_ref,
                       acc_scr, a_scr, w_scr, *, tm, tn, sub, gate):
    j = pl.program_id(1)
    n_sub = tn // sub
    n_cb = tm // 128
    out_rows = D // n_cb

    def act_matmul(sb):
        return _dot(u_ref[sb * sub:(sb + 1) * sub, :], xt_ref[...])

    @pl.when(j == 0)
    def _():
        acc_scr[...] = jnp.zeros_like(acc_scr)
        w_scr[...] = jnp.zeros_like(w_scr)
        a_scr[...] = act_matmul(0)

    def weights(sb, cb, a):
        lanes = slice(cb * 128, (cb + 1) * 128)
        n_il = sub // PEER_KEYS
        n_rg = PEER_KEYS // PEER_ROWS
        parts = [[None] * n_rg for _ in range(n_il)]
        for rg in range(n_rg):
            grp = slice(rg * PEER_ROWS // 8, (rg + 1) * PEER_ROWS // 8)
            g = [jnp.zeros((PEER_ROWS, 128), F32)] * n_il
            for h in range(PEER_HEADS):
                e2 = tab_ref[cb, h, grp].reshape(PEER_ROWS, 128)
                for il in range(n_il):
                    i1 = sb * n_il + il
                    hit = e2 >= cut_ref[h, i1:i1 + 1, lanes]
                    g[il] = g[il] + jnp.where(hit, e2 * e1_ref[h, i1:i1 + 1, lanes], 0.0)
            for il in range(n_il):
                r0 = il * PEER_KEYS + rg * PEER_ROWS
                parts[il][rg] = (g[il] * _gelu_x2(a[r0:r0 + PEER_ROWS, lanes])).astype(BF16)
        return jnp.concatenate([p for row in parts for p in row], axis=0)

    def out_piece(sb, q, w):
        rows = slice(q * out_rows, (q + 1) * out_rows)
        if sb < 0:
            return _dot(vp_ref[rows, :], w)
        return _dot(vt_ref[rows, sb * sub:(sb + 1) * sub], w)

    a = a_scr[...]
    w_prev = w_scr[...]
    contrib = [None] * n_cb
    for sb in range(n_sub):
        cols = []
        for cb in range(n_cb):
            if cb == 0:
                a_next = act_matmul(sb + 1) if sb + 1 < n_sub else _dot(un_ref[...], xt_ref[...])
            piece = out_piece(sb - 1, cb, w_prev)
            contrib[cb] = piece if contrib[cb] is None else contrib[cb] + piece
            cols.append(weights(sb, cb, a))
        w_prev = jnp.concatenate(cols, axis=1)
        a = a_next
    for q in range(n_cb):
        acc_scr[q * out_rows:(q + 1) * out_rows, :] += contrib[q]
    a_scr[...] = a
    w_scr[...] = w_prev

    @pl.when(j == pl.num_programs(1) - 1)
    def _():
        for q in range(n_cb):
            acc_scr[q * out_rows:(q + 1) * out_rows, :] += out_piece(n_sub - 1, q, w_scr[...])
        o_ref[...] = res_ref[...] + m_ref[0, gate:gate + 1, :] * acc_scr[...].T


def peer_dense(xt, u, vt, layer, cut, e1, tab, res, mods, tpc, gate, tm=512, tn=4096, sub=256):
    m = xt.shape[1]
    n1 = tn // PEER_KEYS
    part = pl.BlockSpec((PEER_HEADS, n1, tm), lambda i, j: (0, j, i))
    full = pl.BlockSpec((tm // 128,) + tab.shape[1:], lambda i, j: (i, 0, 0, 0, 0))
    n_sub = tn // sub
    last_sub = N_EXPERTS // sub - 1
    return pl.pallas_call(
        functools.partial(_peer_dense_kernel, tm=tm, tn=tn, sub=sub, gate=gate),
        out_shape=jax.ShapeDtypeStruct((m, D), F32),
        grid=(m // tm, N_EXPERTS // tn),
        in_specs=[pl.BlockSpec((D, tm), lambda i, j: (0, i)),
                  pl.BlockSpec((None, tn, D), lambda i, j: (layer, j, 0)),
                  pl.BlockSpec((None, sub, D), lambda i, j: (layer, jnp.minimum((j + 1) * n_sub, last_sub), 0)),
                  pl.BlockSpec((None, D, tn), lambda i, j: (layer, 0, j)),
                  pl.BlockSpec((None, D, sub), lambda i, j: (layer, 0, jnp.maximum(j * n_sub - 1, 0))),
                  part, part, full,
                  pl.BlockSpec((tm, D), lambda i, j: (i, 0)),
                  pl.BlockSpec((1, N_MOD, D), lambda i, j: ((i * tm) // tpc, 0, 0))],
        out_specs=pl.BlockSpec((tm, D), lambda i, j: (i, 0)),
        scratch_shapes=[pltpu.VMEM((D, tm), F32), pltpu.VMEM((sub, tm), F32), pltpu.VMEM((sub, tm), BF16)],
        compiler_params=_cp(("parallel", "arbitrary"), vmem=PEER_VMEM_LIMIT), name="peer_dense",
    )(xt, u, u, vt, vt, cut, e1, tab, res, mods)


def peer_layer(x, norm_g, mods, tpc, wq, keys, u, vt, layer):
    q, ht = norm_mod_q(x, norm_g, mods, tpc, 3, wq)
    cut, e1, tab = peer_stats(q, keys)
    return peer_dense(ht, u, vt, layer, cut, e1, tab, x, mods, tpc, 5)


def kernel(x_prompt, x_sample, c, state_wkv, cache_k, cache_v, c_ctx, ada_w, ada_b, norm_mix, norm_ffn, fnet_w_in, fnet_w_out, rwkv_mu, rwkv_w_r, rwkv_w_k, rwkv_w_v, rwkv_w_o, rwkv_w0, rwkv_w1, rwkv_w2, rwkv_a0, rwkv_a1, rwkv_a2, rwkv_g1, rwkv_g2, rwkv_k_k, rwkv_k_a, rwkv_r_k, rwkv_lnx_g, rwkv_lnx_b, na_w_qkv, na_w_o, na_rpb, peer_w_q, peer_sub_keys, peer_u, peer_v, final_norm):
    nb_c, t_c, _ = x_prompt.shape
    nb_s, t_s, _ = x_sample.shape
    depth = ada_w.shape[0]
    past = cache_k.shape[2]
    bf = lambda w: w.astype(BF16)

    cond = jnp.concatenate([c_ctx[None, :], c, jnp.zeros((16 - 1 - nb_s, D), F32)], axis=0)
    mods_all = ada_all(cond, ada_w, ada_b).reshape(depth, 16, N_MOD, D)

    xc = x_prompt.reshape(nb_c * t_c, D)
    xs = x_sample.reshape(nb_s * t_s, D)
    tpc_c, tpc_s = nb_c * t_c, t_s
    new_wkv, new_k, new_v = [], [], []
    u_all = bf(peer_u)
    vt_all = cast_transposed(peer_v)

    for l in range(depth):
        kind, j = l % 3, l // 3
        mc = mods_all[l, 0:1]
        ms = mods_all[l, 1:1 + nb_s]
        if kind == 0:
            w_in, w_out = bf(fnet_w_in[j]), bf(fnet_w_out[j])
            outs = []
            for x, m, tpc, t in ((xc, mc, tpc_c, t_c), (xs, ms, tpc_s, t_s)):
                f = fnet_dft(norm_mod_mm(x, norm_mix[l], m, tpc, 0, w_in), t)
                outs.append(matmul_res(f, w_out, x, m, tpc, 2))
            xc, xs = outs
        elif kind == 1:
            p = dict(mu=rwkv_mu[j], w_r=bf(rwkv_w_r[j]), w_k=bf(rwkv_w_k[j]), w_v=bf(rwkv_w_v[j]), w_o=bf(rwkv_w_o[j]),
                     w0=rwkv_w0[j], w1=bf(rwkv_w1[j]), w2=bf(rwkv_w2[j]), a0=rwkv_a0[j], a1=bf(rwkv_a1[j]),
                     a2=bf(rwkv_a2[j]), g1=bf(rwkv_g1[j]), g2=bf(rwkv_g2[j]), k_k=rwkv_k_k[j].reshape(1, D),
                     k_a=rwkv_k_a[j].reshape(1, D), r_k=rwkv_r_k[j].reshape(1, D), lnx_g=rwkv_lnx_g[j],
                     lnx_b=rwkv_lnx_b[j])
            s_zero = jnp.zeros((nb_c, 2, N_HEADS, HEAD, HEAD), F32)
            xc, sc = rwkv_layer(xc, norm_mix[l], mc, tpc_c, t_c, s_zero, p)
            xs, _ = rwkv_layer(xs, norm_mix[l], ms, tpc_s, t_s, state_wkv[:, j], p)
            new_wkv.append(sc)
        else:
            w_qkv, w_o = bf(na_w_qkv[j]), bf(na_w_o[j])
            qc, kc, vc, kc32, vc32 = qkv_proj(xc, norm_mix[l], mc, tpc_c, w_qkv, True)
            qs, ks, vs = qkv_proj(xs, norm_mix[l], ms, tpc_s, w_qkv, False)
            new_k.append(kc32.reshape(nb_c, t_c, N_HEADS, HEAD))
            new_v.append(vc32.reshape(nb_c, t_c, N_HEADS, HEAD))
            oc = na_ctx_attn(qc, kc, vc, t_c)
            os_ = na_lat_attn(qs, ks, vs, bf(cache_k[:, j]).reshape(nb_s * past, D),
                              bf(cache_v[:, j]).reshape(nb_s * past, D), na_rpb[j], t_s, past)
            xc = matmul_res(oc, w_o, xc, mc, tpc_c, 2)
            xs = matmul_res(os_, w_o, xs, ms, tpc_s, 2)
        wq = bf(peer_w_q[l])
        keys = bf(peer_sub_keys[l]).reshape(2 * PEER_HEADS, PEER_KEYS, PEER_KEYS)
        xc = peer_layer(xc, norm_ffn[l], mc, tpc_c, wq, keys, u_all, vt_all, l)
        xs = peer_layer(xs, norm_ffn[l], ms, tpc_s, wq, keys, u_all, vt_all, l)

    y_prompt = rms_final(xc, final_norm).reshape(nb_c, t_c, D)
    y_sample = rms_final(xs, final_norm).reshape(nb_s, t_s, D)
    return (y_prompt, y_sample, jnp.stack(new_wkv, axis=1), jnp.stack(new_k, axis=1), jnp.stack(new_v, axis=1))
```

```python
import functools
import math

import numpy as np
import jax
import jax.numpy as jnp
from jax import lax
from jax.experimental import pallas as pl
from jax.experimental.pallas import tpu as pltpu

F32 = jnp.float32
BF16 = jnp.bfloat16

D = 1024
N_MOD = 6
EPS = 1e-6
HEAD = 64
N_HEADS = D // HEAD
LNX_EPS = 64e-5
GRID_W = 64
WIN_ROWS = 8
WIN_COLS = 16
NA_SCALE = HEAD ** -0.5
NEG_INF = -1e30
FNET_GROUPS = 4
FNET_GD = D // FNET_GROUPS
PEER_KEYS = 128
PEER_HEADS = 8
PEER_TOPK = 16
N_EXPERTS = PEER_KEYS * PEER_KEYS
PEER_ROWS = 32
SCAN_L = 128

VMEM_LIMIT = 56 * 1024 * 1024


def _cp(sem, vmem=VMEM_LIMIT):
    return pltpu.CompilerParams(dimension_semantics=sem, vmem_limit_bytes=vmem)


def _dot(a, b):
    return jnp.dot(a, b, preferred_element_type=F32)


def _dot_nt(a, b):
    return lax.dot_general(a, b, (((1,), (1,)), ((), ())), preferred_element_type=F32)


def _split_dot(x, w):
    hi = x.astype(BF16)
    lo = (x - hi.astype(F32)).astype(BF16)
    return _dot(hi, w) + _dot(lo, w)


def _ada_kernel(c_ref, w_ref, b_ref, o_ref):
    c = c_ref[...]
    s = c * jax.nn.sigmoid(c)
    o_ref[0] = _dot(s.astype(BF16), w_ref[0].astype(BF16)) + b_ref[0]


def ada_all(cond16, ada_w, ada_b):
    depth = ada_w.shape[0]
    tn = 1024
    return pl.pallas_call(
        _ada_kernel,
        out_shape=jax.ShapeDtypeStruct((depth, 16, N_MOD * D), F32),
        grid=(depth, N_MOD * D // tn),
        in_specs=[pl.BlockSpec((16, D), lambda l, j: (0, 0)),
                  pl.BlockSpec((1, D, tn), lambda l, j: (l, 0, j)),
                  pl.BlockSpec((1, 1, tn), lambda l, j: (l, 0, j))],
        out_specs=pl.BlockSpec((1, 16, tn), lambda l, j: (l, 0, j)),
        compiler_params=_cp(("parallel", "parallel")),
        name="ada",
    )(cond16, ada_w, ada_b.reshape(depth, 1, N_MOD * D))


def _modulate(x, g, shift, scale):
    y = x * lax.rsqrt(jnp.mean(x * x, axis=-1, keepdims=True) + EPS)
    return (y * g) * (1 + scale) + shift


def _norm_mod_q_kernel(x_ref, g_ref, m_ref, w_ref, q_ref, ht_ref, *, which):
    h = _modulate(x_ref[...], g_ref[...], m_ref[0, which:which + 1, :], m_ref[0, which + 1:which + 2, :])
    q_ref[...] = _dot(h.astype(BF16), w_ref[...]).astype(q_ref.dtype)
    ht_ref[...] = h.T.astype(ht_ref.dtype)


def _norm_mod_specs(tm, tpc):
    return [pl.BlockSpec((tm, D), lambda i: (i, 0)),
            pl.BlockSpec((1, D), lambda i: (0, 0)),
            pl.BlockSpec((1, N_MOD, D), lambda i: ((i * tm) // tpc, 0, 0))]


def _norm_mod_mm_kernel(x_ref, g_ref, m_ref, w_ref, o_ref, *, which):
    h = _modulate(x_ref[...], g_ref[...], m_ref[0, which:which + 1, :], m_ref[0, which + 1:which + 2, :])
    o_ref[...] = _dot(h.astype(BF16), w_ref[...]).astype(o_ref.dtype)


def norm_mod_mm(x, g, mods, tpc, which, w, tm=512):
    m, n = x.shape[0], w.shape[1]
    return pl.pallas_call(
        functools.partial(_norm_mod_mm_kernel, which=which),
        out_shape=jax.ShapeDtypeStruct((m, n), BF16),
        grid=(m // tm,), in_specs=_norm_mod_specs(tm, tpc) + [pl.BlockSpec((D, n), lambda i: (0, 0))],
        out_specs=pl.BlockSpec((tm, n), lambda i: (i, 0)),
        compiler_params=_cp(("parallel",)), name="norm_mod_mm",
    )(x, g.reshape(1, D), mods, w)


def norm_mod_q(x, g, mods, tpc, which, w, tm=512):
    m, n = x.shape[0], w.shape[1]
    return pl.pallas_call(
        functools.partial(_norm_mod_q_kernel, which=which),
        out_shape=(jax.ShapeDtypeStruct((m, n), BF16), jax.ShapeDtypeStruct((D, m), BF16)),
        grid=(m // tm,), in_specs=_norm_mod_specs(tm, tpc) + [pl.BlockSpec((D, n), lambda i: (0, 0))],
        out_specs=(pl.BlockSpec((tm, n), lambda i: (i, 0)), pl.BlockSpec((D, tm), lambda i: (0, i))),
        compiler_params=_cp(("parallel",)), name="norm_mod_q",
    )(x, g.reshape(1, D), mods, w)


def _final_norm_kernel(x_ref, g_ref, o_ref):
    x = x_ref[...]
    o_ref[...] = x * lax.rsqrt(jnp.mean(x * x, axis=-1, keepdims=True) + EPS) * g_ref[...]


def rms_final(x, g, tm=512):
    m = x.shape[0]
    return pl.pallas_call(
        _final_norm_kernel,
        out_shape=jax.ShapeDtypeStruct((m, D), F32),
        grid=(m // tm,),
        in_specs=[pl.BlockSpec((tm, D), lambda i: (i, 0)), pl.BlockSpec((1, D), lambda i: (0, 0))],
        out_specs=pl.BlockSpec((tm, D), lambda i: (i, 0)),
        compiler_params=_cp(("parallel",)), name="final_norm",
    )(x, g.reshape(1, D))


def _mm_res_kernel(a_ref, w_ref, res_ref, m_ref, o_ref, *, gate):
    o_ref[...] = res_ref[...] + m_ref[0, gate:gate + 1, :] * _dot(a_ref[...], w_ref[...])


def matmul_res(a, w, res, mods, tpc, gate, tm=512):
    m = a.shape[0]
    return pl.pallas_call(
        functools.partial(_mm_res_kernel, gate=gate),
        out_shape=jax.ShapeDtypeStruct((m, D), F32),
        grid=(m // tm,),
        in_specs=[pl.BlockSpec((tm, D), lambda i: (i, 0)),
                  pl.BlockSpec((D, D), lambda i: (0, 0)),
                  pl.BlockSpec((tm, D), lambda i: (i, 0)),
                  pl.BlockSpec((1, N_MOD, D), lambda i: ((i * tm) // tpc, 0, 0))],
        out_specs=pl.BlockSpec((tm, D), lambda i: (i, 0)),
        compiler_params=_cp(("parallel",)), name="matmul_res",
    )(a, w, res, mods)


def _dft_mats(t):
    def cs(n):
        k = np.arange(n)
        ang = 2.0 * np.pi * ((k[:, None] * k[None, :]) % n) / n
        s = 1.0 / math.sqrt(n)
        return np.cos(ang) * s, np.sin(ang) * s
    cc, sc = cs(FNET_GD)
    ct, st = cs(t)
    return (jnp.asarray(np.concatenate([cc, sc], axis=1), BF16), jnp.asarray(ct, BF16), jnp.asarray(st, BF16))


def _dft_kernel(u_ref, cs_ref, ct_ref, st_ref, o_ref):
    p = _dot(u_ref[...], cs_ref[...])
    pc = p[:, :FNET_GD].astype(BF16)
    ps = p[:, FNET_GD:].astype(BF16)
    o_ref[...] = (_dot(ct_ref[...], pc) - _dot(st_ref[...], ps)).astype(o_ref.dtype)


def fnet_dft(u, t):
    m = u.shape[0]
    cs, ct, st = _dft_mats(t)
    return pl.pallas_call(
        _dft_kernel,
        out_shape=jax.ShapeDtypeStruct((m, D), BF16),
        grid=(m // t, FNET_GROUPS),
        in_specs=[pl.BlockSpec((t, FNET_GD), lambda s, g: (s, g)),
                  pl.BlockSpec((FNET_GD, 2 * FNET_GD), lambda s, g: (0, 0)),
                  pl.BlockSpec((t, t), lambda s, g: (0, 0)),
                  pl.BlockSpec((t, t), lambda s, g: (0, 0))],
        out_specs=pl.BlockSpec((t, FNET_GD), lambda s, g: (s, g)),
        compiler_params=_cp(("parallel", "parallel")), name="fnet_dft",
    )(u, cs, ct, st)


def _head_select():
    return jnp.asarray((np.arange(D) // HEAD)[:, None] == np.arange(128)[None, :], BF16)


def _head_sum(x, sel):
    c = _split_dot(x, sel)
    hi = c.astype(BF16)
    lo = (c - hi.astype(F32)).astype(BF16)
    return _dot_nt(hi, sel) + _dot_nt(lo, sel)


def _rwkv_proj_kernel(x_ref, xp_ref, xn_ref, g_ref, m_ref, mu_ref, wr_ref, wk_ref, wv_ref, g1_ref, g2_ref,
                      w0_ref, w1_ref, w2_ref, a0_ref, a1_ref, a2_ref, kk_ref, ka_ref, rk_ref, sel_ref,
                      r_out, v_out, kkn_out, g_out, bonus_out, lw_out, kd_out, bd_out, *, tm, t):
    i = pl.program_id(0)
    shift = m_ref[0, 0:1, :]
    scale = m_ref[0, 1:2, :]
    g = g_ref[...]
    h = _modulate(x_ref[...], g, shift, scale)
    first = (i * tm) % t == 0
    last = ((i + 1) * tm) % t == 0
    hp = jnp.where(first, 0.0, _modulate(xp_ref[7:8, :], g, shift, scale))
    hn = jnp.where(last, 0.0, _modulate(xn_ref[0:1, :], g, shift, scale))
    row = lax.broadcasted_iota(jnp.int32, (tm, 1), 0)
    prev = jnp.where(row == 0, hp, pltpu.roll(h, 1, axis=0))
    nxt = jnp.where(row == tm - 1, hn, pltpu.roll(h, tm - 1, axis=0))
    xx = 0.5 * (prev + nxt) - h

    def mix(j):
        return (h + xx * mu_ref[j:j + 1, :]).astype(BF16)

    r = _dot(mix(0), wr_ref[...])
    k = _dot(mix(2), wk_ref[...])
    v = _dot(mix(3), wv_ref[...])
    gate = _dot(jax.nn.sigmoid(_dot(mix(5), g1_ref[...])).astype(BF16), g2_ref[...])
    xw = mix(1)
    xa = mix(4)
    sel = sel_ref[...]
    kk = k * kk_ref[...]
    kk = kk * lax.rsqrt(_head_sum(kk * kk, sel) + 1e-12)
    ksum = jnp.zeros_like(k)
    for j in range(2):
        w_raw = w0_ref[j:j + 1, :] + _dot(jnp.tanh(_dot(xw, w1_ref[j])).astype(BF16), w2_ref[j])
        lw_out[j] = -jnp.exp(-jax.nn.softplus(-w_raw) - 0.5)
        a = jax.nn.sigmoid(a0_ref[j:j + 1, :] + _dot(_dot(xa, a1_ref[j]).astype(BF16), a2_ref[j]))
        kd = k * (1 + (a - 1) * ka_ref[...])
        kd_out[j] = kd
        bd_out[j] = kk * a
        ksum = ksum + kd
    r_out[...] = r
    v_out[...] = v
    kkn_out[...] = kk
    g_out[...] = gate
    bonus_out[...] = _head_sum(r * ksum * rk_ref[...], sel) * v


def rwkv_proj(x, norm_g, mods, tpc, t, p, tm=256):
    m = x.shape[0]
    nb8 = m // 8
    full = lambda *shape: pl.BlockSpec(shape, lambda i: (0,) * len(shape))
    tok = pl.BlockSpec((tm, D), lambda i: (i, 0))
    tok2 = pl.BlockSpec((2, tm, D), lambda i: (0, i, 0))
    in_specs = [tok,
                pl.BlockSpec((8, D), lambda i: (jnp.maximum(i * (tm // 8) - 1, 0), 0)),
                pl.BlockSpec((8, D), lambda i: (jnp.minimum((i + 1) * (tm // 8), nb8 - 1), 0)),
                full(1, D),
                pl.BlockSpec((1, N_MOD, D), lambda i: ((i * tm) // tpc, 0, 0)),
                full(6, D), full(D, D), full(D, D), full(D, D), full(D, 128), full(128, D),
                full(2, D), full(2, D, 64), full(2, 64, D), full(2, D), full(2, D, 64), full(2, 64, D),
                full(1, D), full(1, D), full(1, D), full(D, 128)]
    sd = jax.ShapeDtypeStruct
    return pl.pallas_call(
        functools.partial(_rwkv_proj_kernel, tm=tm, t=t),
        out_shape=(sd((m, D), F32),) * 5 + (sd((2, m, D), F32),) * 3,
        grid=(m // tm,), in_specs=in_specs,
        out_specs=(tok,) * 5 + (tok2,) * 3,
        compiler_params=_cp(("parallel",)), name="rwkv_proj",
    )(x, x, x, norm_g.reshape(1, D), mods, p["mu"], p["w_r"], p["w_k"], p["w_v"], p["g1"], p["g2"],
      p["w0"], p["w1"], p["w2"], p["a0"], p["a1"], p["a2"], p["k_k"], p["k_a"], p["r_k"], _head_select())


def _rwkv_scan_kernel(r_ref, v_ref, kk_ref, lw_ref, kd_ref, b_ref, z0_ref, y_ref, zout_ref, z_scr, *, n_chunks):
    L = SCAN_L
    d = pl.program_id(1)
    c = pl.program_id(2)

    row = lax.broadcasted_iota(jnp.int32, (L, L), 0)
    col = lax.broadcasted_iota(jnp.int32, (L, L), 1)

    @pl.when(c == 0)
    def _():
        dup = (lax.broadcasted_iota(jnp.int32, (HEAD, L), 1) % HEAD
               == lax.broadcasted_iota(jnp.int32, (HEAD, L), 0)).astype(BF16)
        for p in range(N_HEADS // 2):
            both = jnp.concatenate([_split3_dot(z0_ref[0, 0, 2 * p], dup), _split3_dot(z0_ref[0, 0, 2 * p + 1], dup)],
                                   axis=0)
            z_scr[p] = jnp.where((row // HEAD) == (col // HEAD), both, 0.0)

    fwd = d == 0
    order = (col - row) * (1 - 2 * d)
    before = order < 0
    upto = order <= 0
    cum_mat = upto.astype(BF16)
    same_head = (row // HEAD) == (col // HEAD)
    lane = lax.broadcasted_iota(jnp.int32, (1, 2 * HEAD), 1)
    head_mask = (lane < HEAD, lane >= HEAD)
    n_double = int(math.log2(L))

    def prepare(p):
        sl = slice(p * 2 * HEAD, (p + 1) * 2 * HEAD)
        lw = lw_ref[0, :, sl]
        cum = _split_dot_left(cum_mat, lw)
        tot = jnp.where(fwd, cum[L - 1:L, :], cum[0:1, :])
        inv = jnp.exp(-cum)
        ar = jnp.concatenate([-kk_ref[:, sl] * jnp.exp(cum - lw), r_ref[:, sl] * jnp.exp(cum)], axis=0)
        bk = jnp.concatenate([b_ref[0, :, sl] * inv, kd_ref[0, :, sl] * inv], axis=0).astype(BF16)
        z = z_scr[p]
        base = _dot_nt(ar.astype(BF16), z.astype(BF16))
        return dict(ar=ar, bk=bk, z=z, base=base, v=v_ref[:, sl], tot=tot)

    def start_chain(pp, hm):
        g4 = _dot_nt(jnp.where(hm, pp["ar"], 0.0).astype(BF16), pp["bk"])
        vm = jnp.where(hm, pp["v"], 0.0).astype(BF16)
        n = jnp.where(before, g4[:L, :L], 0.0).astype(BF16)
        x = jnp.where(hm, pp["base"][:L], 0.0) + _dot(jnp.where(before, g4[:L, L:], 0.0).astype(BF16), vm)
        out = jnp.concatenate([jnp.where(upto, g4[L:, :L], 0.0), jnp.where(upto, g4[L:, L:], 0.0)], axis=1)
        return dict(n=n, x=x, out=out.astype(BF16), vm=vm)

    pairs = [prepare(p) for p in range(N_HEADS // 2)]
    chains = [[start_chain(pp, hm) for hm in head_mask] for pp in pairs]
    for it in range(n_double):
        for ch in (ch for pair in chains for ch in pair):
            nb = ch["n"]
            xb = ch["x"].astype(BF16)
            if it + 1 < n_double:
                res = _dot(nb, jnp.concatenate([xb, nb], axis=1))
                ch["x"] = ch["x"] + res[:, :2 * HEAD]
                ch["n"] = res[:, 2 * HEAD:].astype(BF16)
            else:
                ch["x"] = ch["x"] + _dot(nb, xb)
    for p, (pp, pair) in enumerate(zip(pairs, chains)):
        y = pp["base"][L:]
        for ch in pair:
            y = y + _dot(ch["out"], jnp.concatenate([ch["x"].astype(BF16), ch["vm"]], axis=0))
        uv = jnp.concatenate([pair[0]["x"] + pair[1]["x"], pp["v"]], axis=0)
        inc = _dot(uv.T.astype(BF16), pp["bk"])
        z_scr[p] = jnp.where(same_head, pp["z"] + inc, 0.0) * jnp.exp(pp["tot"])
        y_ref[0, :, p * 2 * HEAD:(p + 1) * 2 * HEAD] = y

    @pl.when(c == n_chunks - 1)
    def _():
        prow = lax.broadcasted_iota(jnp.int32, (L, HEAD), 0)
        pcol = lax.broadcasted_iota(jnp.int32, (L, HEAD), 1)
        pick_a = (prow == pcol).astype(BF16)
        pick_b = (prow == pcol + HEAD).astype(BF16)
        for p in range(N_HEADS // 2):
            z = z_scr[p]
            zout_ref[0, 0, 2 * p] = _split3_dot(z[:HEAD, :], pick_a)
            zout_ref[0, 0, 2 * p + 1] = _split3_dot(z[HEAD:, :], pick_b)


def _split3_dot(x, w):
    hi = x.astype(BF16)
    r1 = x - hi.astype(F32)
    mid = r1.astype(BF16)
    lo = (r1 - mid.astype(F32)).astype(BF16)
    return _dot(hi, w) + _dot(mid, w) + _dot(lo, w)


def _split_dot_left(w, x):
    hi = x.astype(BF16)
    lo = (x - hi.astype(F32)).astype(BF16)
    return _dot(w, hi) + _dot(w, lo)


def rwkv_scan(r, v, kk, lw, kd, bd, z0, t):
    m = r.shape[0]
    n_seq = m // t
    nc = t // SCAN_L

    def blk(s, d, c):
        return s * nc + c + d * (nc - 1 - 2 * c)

    tok = pl.BlockSpec((SCAN_L, D), lambda s, d, c: (blk(s, d, c), 0))
    tok2 = pl.BlockSpec((1, SCAN_L, D), lambda s, d, c: (d, blk(s, d, c), 0))
    zspec = pl.BlockSpec((1, 1, N_HEADS, HEAD, HEAD), lambda s, d, c: (s, d, 0, 0, 0))
    return pl.pallas_call(
        functools.partial(_rwkv_scan_kernel, n_chunks=nc),
        out_shape=(jax.ShapeDtypeStruct((2, m, D), F32), jax.ShapeDtypeStruct(z0.shape, F32)),
        grid=(n_seq, 2, nc),
        in_specs=[tok, tok, tok, tok2, tok2, tok2, zspec],
        out_specs=(tok2, zspec),
        scratch_shapes=[pltpu.VMEM((N_HEADS // 2, 2 * HEAD, 2 * HEAD), F32)],
        compiler_params=_cp(("parallel", "parallel", "arbitrary")), name="rwkv_scan",
    )(r, v, kk, lw, kd, bd, z0)


def _rwkv_out_kernel(y_ref, bonus_ref, g_ref, lg_ref, lb_ref, sel_ref, w_ref, res_ref, m_ref, o_ref):
    sel = sel_ref[...]
    o = y_ref[0] + y_ref[1]
    cen = o - _head_sum(o, sel) * (1.0 / HEAD)
    var = _head_sum(cen * cen, sel) * (1.0 / HEAD)
    o = cen * lax.rsqrt(var + LNX_EPS) * lg_ref[...] + lb_ref[...] + bonus_ref[...]
    o_ref[...] = res_ref[...] + m_ref[0, 2:3, :] * _dot((o * g_ref[...]).astype(BF16), w_ref[...])


def rwkv_out(y, bonus, g, lnx_g, lnx_b, w_o, res, mods, tpc, tm=256):
    m = bonus.shape[0]
    tok = pl.BlockSpec((tm, D), lambda i: (i, 0))
    row = pl.BlockSpec((1, D), lambda i: (0, 0))
    return pl.pallas_call(
        _rwkv_out_kernel,
        out_shape=jax.ShapeDtypeStruct((m, D), F32),
        grid=(m // tm,),
        in_specs=[pl.BlockSpec((2, tm, D), lambda i: (0, i, 0)), tok, tok, row, row,
                  pl.BlockSpec((D, 128), lambda i: (0, 0)), pl.BlockSpec((D, D), lambda i: (0, 0)), tok,
                  pl.BlockSpec((1, N_MOD, D), lambda i: ((i * tm) // tpc, 0, 0))],
        out_specs=tok,
        compiler_params=_cp(("parallel",)), name="rwkv_out",
    )(y, bonus, g, lnx_g.reshape(1, D), lnx_b.reshape(1, D), _head_select(), w_o, res, mods)


def rwkv_layer(x, norm_g, mods, tpc, t, s0, p):
    r, v, kk, g, bonus, lw, kd, bd = rwkv_proj(x, norm_g, mods, tpc, t, p)
    y, zf = rwkv_scan(r, v, kk, lw, kd, bd, s0, t)
    return rwkv_out(y, bonus, g, p["lnx_g"], p["lnx_b"], p["w_o"], x, mods, tpc), zf


def _softmax_rows(parts):
    m = parts[0].max(axis=-1, keepdims=True)
    for s in parts[1:]:
        m = jnp.maximum(m, s.max(axis=-1, keepdims=True))
    es = [jnp.exp(s - m) for s in parts]
    den = es[0].sum(axis=-1, keepdims=True)
    for e in es[1:]:
        den = den + e.sum(axis=-1, keepdims=True)
    inv = 1.0 / den
    return [(e * inv).astype(BF16) for e in es]


def _qkv_kernel(x_ref, g_ref, m_ref, w_ref, *out_refs, with_f32):
    h = _modulate(x_ref[...], g_ref[...], m_ref[0, 0:1, :], m_ref[0, 1:2, :]).astype(BF16)
    for idx in range(3):
        r = _dot(h, w_ref[:, idx * D:(idx + 1) * D])
        out_refs[idx][...] = r.astype(BF16)
        if with_f32 and idx > 0:
            out_refs[2 + idx][...] = r


def qkv_proj(x, g, mods, tpc, w, with_f32, tm=512):
    m = x.shape[0]
    tok = pl.BlockSpec((tm, D), lambda i: (i, 0))
    sd = jax.ShapeDtypeStruct
    n_f32 = 2 if with_f32 else 0
    return pl.pallas_call(
        functools.partial(_qkv_kernel, with_f32=with_f32),
        out_shape=(sd((m, D), BF16),) * 3 + (sd((m, D), F32),) * n_f32,
        grid=(m // tm,),
        in_specs=_norm_mod_specs(tm, tpc) + [pl.BlockSpec((D, 3 * D), lambda i: (0, 0))],
        out_specs=(tok,) * (3 + n_f32),
        compiler_params=_cp(("parallel",)), name="qkv_proj",
    )(x, g.reshape(1, D), mods, w)


def _pair_masks():
    lane = lax.broadcasted_iota(jnp.int32, (1, 2 * HEAD), 1)
    return lane < HEAD, lane >= HEAD


def _na_ctx_kernel(q_ref, k_ref, v_ref, o_ref):
    masks = _pair_masks()
    zero = jnp.zeros((), BF16)
    scores = []
    for p in range(N_HEADS // 2):
        sl = slice(p * 2 * HEAD, (p + 1) * 2 * HEAD)
        q = q_ref[:, sl]
        k = k_ref[:, sl]
        scores.append([_dot_nt(jnp.where(hm, q, zero), k) * NA_SCALE for hm in masks])
    probs = [[_softmax_rows([s])[0] for s in pair] for pair in scores]
    for p in range(N_HEADS // 2):
        sl = slice(p * 2 * HEAD, (p + 1) * 2 * HEAD)
        v = v_ref[:, sl]
        o_ref[:, sl] = jnp.where(masks[0], _dot(probs[p][0], v), _dot(probs[p][1], v)).astype(o_ref.dtype)


def na_ctx_attn(q, k, v, t):
    m = q.shape[0]
    seq = pl.BlockSpec((t, D), lambda b: (b, 0))
    return pl.pallas_call(
        _na_ctx_kernel,
        out_shape=jax.ShapeDtypeStruct((m, D), BF16),
        grid=(m // t,),
        in_specs=[seq, seq, seq],
        out_specs=seq,
        compiler_params=_cp(("parallel",)), name="na_ctx_attn",
    )(q, k, v)


def _win_start(r, rows):
    return jnp.clip(r - WIN_ROWS // 2, 0, rows - WIN_ROWS)


def _na_lat_kernel(q_ref, k_ref, v_ref, ck_ref, cv_ref, bias_ref, o_ref, *, rows):
    r = pl.program_id(1)
    start = pl.multiple_of(_win_start(r, rows) * GRID_W, GRID_W)
    n_loc = WIN_ROWS * GRID_W
    qc = lax.broadcasted_iota(jnp.int32, (GRID_W, n_loc), 0)
    kc = lax.broadcasted_iota(jnp.int32, (GRID_W, n_loc), 1) % GRID_W
    cs = jnp.clip(qc - WIN_COLS // 2, 0, GRID_W - WIN_COLS)
    valid = (kc >= cs) & (kc < cs + WIN_COLS)
    masks = _pair_masks()
    zero = jnp.zeros((), BF16)
    scores = []
    for p in range(N_HEADS // 2):
        sl = slice(p * 2 * HEAD, (p + 1) * 2 * HEAD)
        q = q_ref[:, sl]
        kw = k_ref[pl.ds(start, n_loc), sl]
        ck = ck_ref[:, sl]
        pair = []
        for i, hm in enumerate(masks):
            qm = jnp.where(hm, q, zero)
            s_loc = jnp.where(valid, _dot_nt(qm, kw) * NA_SCALE + bias_ref[0, 2 * p + i], NEG_INF)
            pair.append([s_loc, _dot_nt(qm, ck) * NA_SCALE])
        scores.append(pair)
    probs = [[_softmax_rows(parts) for parts in pair] for pair in scores]
    for p in range(N_HEADS // 2):
        sl = slice(p * 2 * HEAD, (p + 1) * 2 * HEAD)
        vw = v_ref[pl.ds(start, n_loc), sl]
        cv = cv_ref[:, sl]
        outs = [_dot(p_loc, vw) + _dot(p_ctx, cv) for p_loc, p_ctx in probs[p]]
        o_ref[:, sl] = jnp.where(masks[0], outs[0], outs[1]).astype(o_ref.dtype)


def _na_bias(rpb, rows):
    wr = min(WIN_ROWS, rows)
    qc = np.arange(GRID_W)
    col_off = np.clip(qc[None, :] - qc[:, None], -(WIN_COLS - 1), WIN_COLS - 1) + WIN_COLS - 1
    onehot = jnp.asarray(np.arange(2 * WIN_COLS - 1)[:, None, None] == col_off[None], F32)
    b = jnp.einsum('hrc,cqk->hrqk', rpb.astype(F32), onehot, precision=lax.Precision.HIGHEST)
    b = jnp.stack([b[:, ro0:ro0 + wr] for ro0 in range(WIN_ROWS)], axis=0)
    return jnp.transpose(b, (0, 1, 3, 2, 4)).reshape(WIN_ROWS, N_HEADS, GRID_W, wr * GRID_W)


def na_lat_attn(q, k, v, ck, cv, rpb, t, past):
    m = q.shape[0]
    rows = t // GRID_W
    assert rows >= WIN_ROWS
    n_seq = m // t
    n_loc = WIN_ROWS * GRID_W

    def bias_idx(b, r):
        return (_win_start(r, rows) - r + WIN_ROWS - 1, 0, 0, 0)

    return pl.pallas_call(
        functools.partial(_na_lat_kernel, rows=rows),
        out_shape=jax.ShapeDtypeStruct((m, D), BF16),
        grid=(n_seq, rows),
        in_specs=[pl.BlockSpec((GRID_W, D), lambda b, r: (b * rows + r, 0)),
                  pl.BlockSpec((t, D), lambda b, r: (b, 0)),
                  pl.BlockSpec((t, D), lambda b, r: (b, 0)),
                  pl.BlockSpec((past, D), lambda b, r: (b, 0)),
                  pl.BlockSpec((past, D), lambda b, r: (b, 0)),
                  pl.BlockSpec((1, N_HEADS, GRID_W, n_loc), bias_idx)],
        out_specs=pl.BlockSpec((GRID_W, D), lambda b, r: (b * rows + r, 0)),
        compiler_params=_cp(("parallel", "arbitrary")), name="na_lat_attn",
    )(q, k, v, ck, cv, _na_bias(rpb, rows))


def _sort16_net():
    def merge(lo, hi, r):
        step = r * 2
        if step < hi - lo:
            yield from merge(lo, hi, step)
            yield from merge(lo + r, hi, step)
            yield from ((i, i + r) for i in range(lo + r, hi - r, step))
        else:
            yield (lo, lo + r)

    def sort(lo, hi):
        if hi - lo >= 1:
            mid = lo + (hi - lo) // 2
            yield from sort(lo, mid)
            yield from sort(mid + 1, hi)
            yield from merge(lo, hi, 1)

    return tuple(sort(0, PEER_TOPK - 1))


_SORT16 = _sort16_net()
_BITONIC16 = tuple((i, i + d) for d in (8, 4, 2, 1) for i in range(PEER_TOPK) if not i & d)


def _exchange(x, net):
    x = list(x)
    for i, j in net:
        x[i], x[j] = jnp.maximum(x[i], x[j]), jnp.minimum(x[i], x[j])
    return x


def _merge_top16(a, b):
    return _exchange([jnp.maximum(a[i], b[PEER_TOPK - 1 - i]) for i in range(PEER_TOPK)], _BITONIC16)


def _merge_sublanes(x):
    for shift in (4, 2, 1):
        x = _merge_top16(x, [pltpu.roll(v, shift, axis=0) for v in x])
    return x


def _peer_stats_kernel(q_ref, keys_ref, cut_out, e1_out, tab_out):
    tm = q_ref.shape[0]
    sub = lax.broadcasted_iota(jnp.int32, (8, tm), 0)
    ninf = jnp.full((8, tm), -jnp.inf, F32)
    for h in range(PEER_HEADS):
        s, tops = [], []
        for c in range(2):
            hc = 2 * h + c
            q = q_ref[:, hc * PEER_KEYS:(hc + 1) * PEER_KEYS]
            sc = _dot_nt(keys_ref[hc], q)
            s.append(sc)
            groups = [sc[8 * i:8 * i + 8, :] for i in range(PEER_KEYS // 8)]
            tops.append(_merge_sublanes(_exchange(groups, _SORT16)))
        a1, a2 = tops
        a1col = a1[7]
        for jj in range(6, -1, -1):
            a1col = jnp.where(sub == jj, a1[jj], a1col)
        lists = [jnp.where(sub < min(8, PEER_TOPK // (k + 1)), a1col + a2[k], ninf) for k in range(PEER_TOPK)]
        tail = [a1[8 + k] + a2[0] for k in range(8)] + [ninf] * 8
        best = _merge_top16(_merge_sublanes(lists), tail)
        z = jnp.zeros((8, tm), F32)
        for b in best:
            z = z + jnp.exp(b - best[0])
        thr = best[PEER_TOPK - 1]
        m2 = a2[0][0:1, :]
        a1hi = a1[15]
        for jj in range(6, -1, -1):
            a1hi = jnp.where(sub == jj, a1[8 + jj], a1hi)
        ranked = [a1col, a1hi]
        cutc = [jnp.full((8, tm), jnp.inf, F32)] * 2
        for k in range(PEER_TOPK):
            e2k = jnp.exp(a2[k] - a2[0])
            cutc = [jnp.where(r + a2[k] >= thr, e2k, c) for r, c in zip(ranked, cutc)]
        cut = jnp.full((PEER_KEYS, tm), jnp.inf, F32)
        for jj in range(PEER_TOPK):
            cut = jnp.where(s[0] == a1[jj][0:1, :], cutc[jj // 8][jj % 8:jj % 8 + 1, :], cut)
        cut_out[h] = cut
        e1_out[h] = jnp.exp(s[0] - a1[0][0:1, :]) * (0.5 / z[0:1, :])
        e2 = jnp.exp(s[1] - m2)
        for lb in range(tm // 128):
            lanes = slice(lb * 128, (lb + 1) * 128)
            tab_out[lb, h] = e2[:, lanes].reshape(PEER_KEYS // 8, 8, 128)


def peer_stats(q, keys, tm=256):
    m = q.shape[0]
    nq = 2 * PEER_HEADS * PEER_KEYS
    sd = jax.ShapeDtypeStruct
    big = pl.BlockSpec((PEER_HEADS, PEER_KEYS, tm), lambda i: (0, 0, i))
    tab_shape = (PEER_HEADS, PEER_KEYS // 8, 8, 128)
    return pl.pallas_call(
        _peer_stats_kernel,
        out_shape=(sd((PEER_HEADS, PEER_KEYS, m), F32),) * 2 + (sd((m // 128,) + tab_shape, F32),),
        grid=(m // tm,),
        in_specs=[pl.BlockSpec((tm, nq), lambda i: (i, 0)),
                  pl.BlockSpec((2 * PEER_HEADS, PEER_KEYS, PEER_KEYS), lambda i: (0, 0, 0))],
        out_specs=(big,) * 2 + (pl.BlockSpec((tm // 128,) + tab_shape, lambda i: (i, 0, 0, 0, 0)),),
        compiler_params=_cp(("parallel",)), name="peer_stats",
    )(q, keys)


def _cast_t_kernel(x_ref, o_ref):
    o_ref[...] = x_ref[...].T.astype(o_ref.dtype)


def cast_transposed(w, te=512):
    n_l, n_e, _ = w.shape
    return pl.pallas_call(
        _cast_t_kernel,
        out_shape=jax.ShapeDtypeStruct((n_l, D, n_e), BF16),
        grid=(n_l, n_e // te),
        in_specs=[pl.BlockSpec((None, te, D), lambda l, e: (l, e, 0))],
        out_specs=pl.BlockSpec((None, D, te), lambda l, e: (l, 0, e)),
        compiler_params=_cp(("parallel", "parallel")), name="cast_transposed",
    )(w)


def _gelu_x2(x):
    return x * (1.0 + lax.erf(x * (1.0 / math.sqrt(2.0))))


def _peer_dense_kernel(xt_ref, u_ref, un_ref, vt_ref, vp_ref, cut_ref, e1_ref, tab_ref, res_ref, m_ref, o_ref,
                       acc_scr, a_scr, w_scr, *, tm, tn, sub, gate):
    j = pl.program_id(1)
    n_sub = tn // sub
    n_cb = tm // 128
    out_rows = D // n_cb

    def act_matmul(sb):
        return _dot(u_ref[sb * sub:(sb + 1) * sub, :], xt_ref[...])

    @pl.when(j == 0)
    def _():
        acc_scr[...] = jnp.zeros_like(acc_scr)
        w_scr[...] = jnp.zeros_like(w_scr)
        a_scr[...] = act_matmul(0)

    def weights(sb, cb, a):
        lanes = slice(cb * 128, (cb + 1) * 128)
        n_il = sub // PEER_KEYS
        n_rg = PEER_KEYS // PEER_ROWS
        parts = [[None] * n_rg for _ in range(n_il)]
        for rg in range(n_rg):
            grp = slice(rg * PEER_ROWS // 8, (rg + 1) * PEER_ROWS // 8)
            g = [jnp.zeros((PEER_ROWS, 128), F32)] * n_il
            for h in range(PEER_HEADS):
                e2 = tab_ref[cb, h, grp].reshape(PEER_ROWS, 128)
                for il in range(n_il):
                    i1 = sb * n_il + il
                    hit = e2 >= cut_ref[h, i1:i1 + 1, lanes]
                    g[il] = g[il] + jnp.where(hit, e2 * e1_ref[h, i1:i1 + 1, lanes], 0.0)
            for il in range(n_il):
                r0 = il * PEER_KEYS + rg * PEER_ROWS
                parts[il][rg] = (g[il] * _gelu_x2(a[r0:r0 + PEER_ROWS, lanes])).astype(BF16)
        return jnp.concatenate([p for row in parts for p in row], axis=0)

    def out_piece(sb, q, w):
        rows = slice(q * out_rows, (q + 1) * out_rows)
        if sb < 0:
            return _dot(vp_ref[rows, :], w)
        return _dot(vt_ref[rows, sb * sub:(sb + 1) * sub], w)

    a = a_scr[...]
    w_prev = w_scr[...]
    contrib = [None] * n_cb
    for sb in range(n_sub):
        cols = []
        for cb in range(n_cb):
            if cb == 0:
                a_next = act_matmul(sb + 1) if sb + 1 < n_sub else _dot(un_ref[...], xt_ref[...])
            piece = out_piece(sb - 1, cb, w_prev)
            contrib[cb] = piece if contrib[cb] is None else contrib[cb] + piece
            cols.append(weights(sb, cb, a))
        w_prev = jnp.concatenate(cols, axis=1)
        a = a_next
    for q in range(n_cb):
        acc_scr[q * out_rows:(q + 1) * out_rows, :] += contrib[q]
    a_scr[...] = a
    w_scr[...] = w_prev

    @pl.when(j == pl.num_programs(1) - 1)
    def _():
        for q in range(n_cb):
            acc_scr[q * out_rows:(q + 1) * out_rows, :] += out_piece(n_sub - 1, q, w_scr[...])
        o_ref[...] = res_ref[...] + m_ref[0, gate:gate + 1, :] * acc_scr[...].T


def peer_dense(xt, u, vt, layer, cut, e1, tab, res, mods, tpc, gate, tm=512, tn=2048, sub=256):
    m = xt.shape[1]
    n1 = tn // PEER_KEYS
    part = pl.BlockSpec((PEER_HEADS, n1, tm), lambda i, j: (0, j, i))
    full = pl.BlockSpec((tm // 128,) + tab.shape[1:], lambda i, j: (i, 0, 0, 0, 0))
    n_sub = tn // sub
    last_sub = N_EXPERTS // sub - 1
    return pl.pallas_call(
        functools.partial(_peer_dense_kernel, tm=tm, tn=tn, sub=sub, gate=gate),
        out_shape=jax.ShapeDtypeStruct((m, D), F32),
        grid=(m // tm, N_EXPERTS // tn),
        in_specs=[pl.BlockSpec((D, tm), lambda i, j: (0, i)),
                  pl.BlockSpec((None, tn, D), lambda i, j: (layer, j, 0)),
                  pl.BlockSpec((None, sub, D), lambda i, j: (layer, jnp.minimum((j + 1) * n_sub, last_sub), 0)),
                  pl.BlockSpec((None, D, tn), lambda i, j: (layer, 0, j)),
                  pl.BlockSpec((None, D, sub), lambda i, j: (layer, 0, jnp.maximum(j * n_sub - 1, 0))),
                  part, part, full,
                  pl.BlockSpec((tm, D), lambda i, j: (i, 0)),
                  pl.BlockSpec((1, N_MOD, D), lambda i, j: ((i * tm) // tpc, 0, 0))],
        out_specs=pl.BlockSpec((tm, D), lambda i, j: (i, 0)),
        scratch_shapes=[pltpu.VMEM((D, tm), F32), pltpu.VMEM((sub, tm), F32), pltpu.VMEM((sub, tm), BF16)],
        compiler_params=_cp(("parallel", "arbitrary")), name="peer_dense",
    )(xt, u, u, vt, vt, cut, e1, tab, res, mods)


def peer_layer(x, norm_g, mods, tpc, wq, keys, u, vt, layer):
    q, ht = norm_mod_q(x, norm_g, mods, tpc, 3, wq)
    cut, e1, tab = peer_stats(q, keys)
    return peer_dense(ht, u, vt, layer, cut, e1, tab, x, mods, tpc, 5)


def kernel(x_prompt, x_sample, c, state_wkv, cache_k, cache_v, c_ctx, ada_w, ada_b, norm_mix, norm_ffn, fnet_w_in, fnet_w_out, rwkv_mu, rwkv_w_r, rwkv_w_k, rwkv_w_v, rwkv_w_o, rwkv_w0, rwkv_w1, rwkv_w2, rwkv_a0, rwkv_a1, rwkv_a2, rwkv_g1, rwkv_g2, rwkv_k_k, rwkv_k_a, rwkv_r_k, rwkv_lnx_g, rwkv_lnx_b, na_w_qkv, na_w_o, na_rpb, peer_w_q, peer_sub_keys, peer_u, peer_v, final_norm):
    nb_c, t_c, _ = x_prompt.shape
    nb_s, t_s, _ = x_sample.shape
    depth = ada_w.shape[0]
    past = cache_k.shape[2]
    bf = lambda w: w.astype(BF16)

    cond = jnp.concatenate([c_ctx[None, :], c, jnp.zeros((16 - 1 - nb_s, D), F32)], axis=0)
    mods_all = ada_all(cond, ada_w, ada_b).reshape(depth, 16, N_MOD, D)

    xc = x_prompt.reshape(nb_c * t_c, D)
    xs = x_sample.reshape(nb_s * t_s, D)
    tpc_c, tpc_s = nb_c * t_c, t_s
    new_wkv, new_k, new_v = [], [], []
    u_all = bf(peer_u)
    vt_all = cast_transposed(peer_v)

    for l in range(depth):
        kind, j = l % 3, l // 3
        mc = mods_all[l, 0:1]
        ms = mods_all[l, 1:1 + nb_s]
        if kind == 0:
            w_in, w_out = bf(fnet_w_in[j]), bf(fnet_w_out[j])
            outs = []
            for x, m, tpc, t in ((xc, mc, tpc_c, t_c), (xs, ms, tpc_s, t_s)):
                f = fnet_dft(norm_mod_mm(x, norm_mix[l], m, tpc, 0, w_in), t)
                outs.append(matmul_res(f, w_out, x, m, tpc, 2))
            xc, xs = outs
        elif kind == 1:
            p = dict(mu=rwkv_mu[j], w_r=bf(rwkv_w_r[j]), w_k=bf(rwkv_w_k[j]), w_v=bf(rwkv_w_v[j]), w_o=bf(rwkv_w_o[j]),
                     w0=rwkv_w0[j], w1=bf(rwkv_w1[j]), w2=bf(rwkv_w2[j]), a0=rwkv_a0[j], a1=bf(rwkv_a1[j]),
                     a2=bf(rwkv_a2[j]), g1=bf(rwkv_g1[j]), g2=bf(rwkv_g2[j]), k_k=rwkv_k_k[j].reshape(1, D),
                     k_a=rwkv_k_a[j].reshape(1, D), r_k=rwkv_r_k[j].reshape(1, D), lnx_g=rwkv_lnx_g[j],
                     lnx_b=rwkv_lnx_b[j])
            s_zero = jnp.zeros((nb_c, 2, N_HEADS, HEAD, HEAD), F32)
            xc, sc = rwkv_layer(xc, norm_mix[l], mc, tpc_c, t_c, s_zero, p)
            xs, _ = rwkv_layer(xs, norm_mix[l], ms, tpc_s, t_s, state_wkv[:, j], p)
            new_wkv.append(sc)
        else:
            w_qkv, w_o = bf(na_w_qkv[j]), bf(na_w_o[j])
            qc, kc, vc, kc32, vc32 = qkv_proj(xc, norm_mix[l], mc, tpc_c, w_qkv, True)
            qs, ks, vs = qkv_proj(xs, norm_mix[l], ms, tpc_s, w_qkv, False)
            new_k.append(kc32.reshape(nb_c, t_c, N_HEADS, HEAD))
            new_v.append(vc32.reshape(nb_c, t_c, N_HEADS, HEAD))
            oc = na_ctx_attn(qc, kc, vc, t_c)
            os_ = na_lat_attn(qs, ks, vs, bf(cache_k[:, j]).reshape(nb_s * past, D),
                              bf(cache_v[:, j]).reshape(nb_s * past, D), na_rpb[j], t_s, past)
            xc = matmul_res(oc, w_o, xc, mc, tpc_c, 2)
            xs = matmul_res(os_, w_o, xs, ms, tpc_s, 2)
        wq = bf(peer_w_q[l])
        keys = bf(peer_sub_keys[l]).reshape(2 * PEER_HEADS, PEER_KEYS, PEER_KEYS)
        xc = peer_layer(xc, norm_ffn[l], mc, tpc_c, wq, keys, u_all, vt_all, l)
        xs = peer_layer(xs, norm_ffn[l], ms, tpc_s, wq, keys, u_all, vt_all, l)

    y_prompt = rms_final(xc, final_norm).reshape(nb_c, t_c, D)
    y_sample = rms_final(xs, final_norm).reshape(nb_s, t_s, D)
    return (y_prompt, y_sample, jnp.stack(new_wkv, axis=1), jnp.stack(new_k, axis=1), jnp.stack(new_v, axis=1))
```
